```python
import math
import jax, jax.numpy as jnp
from jax import lax
import numpy as np

D_MODEL = 1024
BATCH = 4
SEQ = 4096
DEPTH = 1

MEM_LEN = 256
DSA_HEADS = 8
DSA_HEAD_DIM = 64
DSA_WIDTH = DSA_HEADS * DSA_HEAD_DIM
DSA_Q_RANK = 256
DSA_KV_RANK = 128
IDX_HEADS = 8
IDX_DIM = 32
INDEX_TOPK_MAX = 256
QBLOCK = 128
NUM_BUCKETS = 32
MAX_DISTANCE = 128
RWKV_HEADS = 8
RWKV_HEAD_DIM = 64
RWKV_WIDTH = RWKV_HEADS * RWKV_HEAD_DIM
DECAY_RANK = 64
AAA_RANK = 64
GATE_RANK = 128
RWKV_COLS = 3 * RWKV_WIDTH + DECAY_RANK + AAA_RANK + GATE_RANK
GN_EPS = 64e-5
MEM_HEADS = 4
MEM_HEAD_DIM = 128
MEM_WIDTH = MEM_HEADS * MEM_HEAD_DIM
N_BRANCHES = 3
IN_COLS = (DSA_Q_RANK, DSA_KV_RANK, IDX_DIM, IDX_HEADS, RWKV_COLS, MEM_WIDTH, N_BRANCHES * D_MODEL)
IN_WIDTH = DSA_Q_RANK + DSA_KV_RANK + IDX_DIM + IDX_HEADS + RWKV_COLS + MEM_WIDTH + N_BRANCHES * D_MODEL
PEER_HEADS = 8
PEER_N_KEYS = 128
PEER_N_EXPERTS = PEER_N_KEYS * PEER_N_KEYS
PEER_QUERY_DIM = 256
PEER_HALF = PEER_QUERY_DIM // 2
PEER_TOPK = 16
PEER_BLOCK = 128
DEEPNORM_ALPHA = (2 * DEPTH) ** 0.25
DEEPNORM_BETA = (8 * DEPTH) ** -0.25
LN_EPS = 1e-5
RMS_EPS = 1e-6

kernel_name = "hybrid_dsa_rwkv7_mem_peer_deepnorm"


def split_cols(a, widths):
    offsets = np.cumsum(np.array(widths))[:-1].tolist()
    return jnp.split(a, offsets, axis=-1)


def layernorm(x, g, b):
    xf = x.astype(jnp.float32)
    mu = xf.mean(-1, keepdims=True)
    var = jnp.square(xf - mu).mean(-1, keepdims=True)
    return ((xf - mu) * lax.rsqrt(var + LN_EPS) * g + b).astype(x.dtype)


def rmsnorm(x, g):
    xf = x.astype(jnp.float32)
    return (xf * lax.rsqrt(jnp.square(xf).mean(-1, keepdims=True) + RMS_EPS) * g).astype(x.dtype)


def t5_bucket(dist):
    n = jnp.maximum(dist, 0)
    max_exact = NUM_BUCKETS // 2
    nf = jnp.maximum(n, 1).astype(jnp.float32)
    large = max_exact + (jnp.log(nf / max_exact) / math.log(MAX_DISTANCE / max_exact)
                         * (NUM_BUCKETS - max_exact)).astype(jnp.int32)
    large = jnp.minimum(large, NUM_BUCKETS - 1)
    return jnp.where(n < max_exact, n, large)


def dsa_sparse_attention(c_q, c_kv, k_idx, w_idx, q_norm_g, kv_norm_g, w_uq, w_uk, w_uv, w_idx_q, rel_bias):
    B, S, _ = c_q.shape
    f32 = jnp.float32
    topk = min(INDEX_TOPK_MAX, S // 4)
    nblk = S // QBLOCK
    c_q = rmsnorm(c_q, q_norm_g)
    c_kv = rmsnorm(c_kv, kv_norm_g)
    q = (c_q @ w_uq).reshape(B, S, DSA_HEADS, DSA_HEAD_DIM)
    q_lat = jnp.einsum("bshd,chd->bshc", q, w_uk) * (DSA_HEAD_DIM ** -0.5)
    q_idx = (c_q @ w_idx_q).reshape(B, S, IDX_HEADS, IDX_DIM)
    w_idx = w_idx * (IDX_HEADS ** -0.5 * IDX_DIM ** -0.5)
    key_pos = jnp.arange(S)

    def to_blocks(a):
        return jnp.moveaxis(a.reshape((B, nblk, QBLOCK) + a.shape[2:]), 1, 0)

    def block_fn(args):
        blk, qi, wi, ql = args
        q_pos = blk * QBLOCK + jnp.arange(QBLOCK)
        logits = jnp.einsum("bqhd,bsd->bqhs", qi, k_idx)
        score = jnp.einsum("bqh,bqhs->bqs", wi, jax.nn.relu(logits)).astype(f32)
        causal = key_pos[None, :] <= q_pos[:, None]
        score = jnp.where(causal[None], score, -jnp.inf)
        _, idx = lax.top_k(score, topk)
        valid = idx <= q_pos[None, :, None]
        kv_sel = jax.vmap(lambda c, i: c[i])(c_kv, idx)
        att = jnp.einsum("bqhc,bqkc->bqhk", ql, kv_sel).astype(f32)
        bias = jnp.moveaxis(rel_bias[t5_bucket(q_pos[None, :, None] - idx)], -1, 2).astype(f32)
        att = jnp.where(valid[:, :, None, :], att + bias, -jnp.inf)
        p = jax.nn.softmax(att, axis=-1).astype(kv_sel.dtype)
        o_lat = jnp.einsum("bqhk,bqkc->bqhc", p, kv_sel)
        return jnp.einsum("bqhc,chd->bqhd", o_lat, w_uv)

    out = lax.map(block_fn, (jnp.arange(nblk), to_blocks(q_idx), to_blocks(w_idx), to_blocks(q_lat)))
    return jnp.moveaxis(out, 0, 1).reshape(B, S, DSA_WIDTH)


def rwkv7_scan(r, decay, k, v, kk, a):
    B, S, H, N = r.shape

    def step(state, inp):
        r_t, w_t, k_t, v_t, kk_t, a_t = inp
        sa = jnp.einsum("bhvk,bhk->bhv", state, -kk_t)
        state = (state * w_t[:, :, None, :]
                 + sa[..., None] * (kk_t * a_t)[:, :, None, :]
                 + v_t[..., None] * k_t[:, :, None, :])
        return state, jnp.einsum("bhvk,bhk->bhv", state, r_t)

    seqs = (jnp.moveaxis(r, 1, 0), jnp.moveaxis(decay, 1, 0), jnp.moveaxis(k, 1, 0),
            jnp.moveaxis(v, 1, 0), jnp.moveaxis(kk, 1, 0), jnp.moveaxis(a, 1, 0))
    _, ys = lax.scan(step, jnp.zeros((B, H, N, N), jnp.float32), seqs)
    return jnp.moveaxis(ys, 0, 1)


def rwkv7_time_mix(cols, mu, w0, w2, a0, a2, g2, k_k, k_a, r_k, lnx_g, lnx_b):
    B, S, _ = cols.shape
    f32 = jnp.float32
    prev = jnp.pad(cols, ((0, 0), (1, 0), (0, 0)))[:, :-1]
    cols = cols + (prev - cols) * mu
    r, k, v, wl, al, gl = split_cols(cols, (RWKV_WIDTH, RWKV_WIDTH, RWKV_WIDTH, DECAY_RANK, AAA_RANK, GATE_RANK))
    w = -jax.nn.softplus(-(w0 + jnp.tanh(wl) @ w2)) - 0.5
    decay = jnp.exp(-jnp.exp(w.astype(f32)))
    a = jax.nn.sigmoid(a0 + al @ a2)
    g = jax.nn.sigmoid(gl) @ g2

    def heads(t):
        return t.reshape(B, S, RWKV_HEADS, RWKV_HEAD_DIM)

    kk = heads(k * k_k).astype(f32)
    kk = kk / jnp.maximum(jnp.sqrt(jnp.sum(jnp.square(kk), -1, keepdims=True)), 1e-12)
    k = k * (1 + (a - 1) * k_a)
    r_h, k_h, v_h = heads(r), heads(k), heads(v)
    y = rwkv7_scan(r_h.astype(f32), heads(decay), k_h.astype(f32), v_h.astype(f32), kk, heads(a).astype(f32))
    mu_y = y.mean(-1, keepdims=True)
    var_y = jnp.square(y - mu_y).mean(-1, keepdims=True)
    yn = ((y - mu_y) * lax.rsqrt(var_y + GN_EPS)).reshape(B, S, RWKV_WIDTH) * lnx_g + lnx_b
    bonus = (jnp.sum((r_h * k_h * r_k).astype(f32), -1, keepdims=True) * v_h.astype(f32)).reshape(B, S, RWKV_WIDTH)
    return ((yn + bonus) * g).astype(cols.dtype)


def memory_cross_attention(q_cols, mem, w_mem_kv):
    B, S, _ = q_cols.shape
    M = mem.shape[1]
    q = q_cols.reshape(B, S, MEM_HEADS, MEM_HEAD_DIM)
    k, v = split_cols(mem @ w_mem_kv, (MEM_WIDTH, MEM_WIDTH))
    k = k.reshape(B, M, MEM_HEADS, MEM_HEAD_DIM)
    v = v.reshape(B, M, MEM_HEADS, MEM_HEAD_DIM)
    logits = jnp.einsum("bshd,bmhd->bhsm", q, k).astype(jnp.float32) * (MEM_HEAD_DIM ** -0.5)
    p = jax.nn.softmax(logits, axis=-1).astype(v.dtype)
    return jnp.einsum("bhsm,bmhd->bshd", p, v).reshape(B, S, MEM_WIDTH)


def peer_ffn(h, w_peer_q, peer_keys, peer_u, peer_v):
    B, S, D = h.shape
    q = (h @ w_peer_q).reshape(B, S, PEER_HEADS, 2, PEER_HALF)
    sub = jnp.einsum("bshpd,pnd->bshpn", q, peer_keys).astype(jnp.float32)
    s1, i1 = lax.top_k(sub[..., 0, :], PEER_TOPK)
    s2, i2 = lax.top_k(sub[..., 1, :], PEER_TOPK)
    cand_s = (s1[..., :, None] + s2[..., None, :]).reshape(B, S, PEER_HEADS, PEER_TOPK * PEER_TOPK)
    cand_i = (i1[..., :, None] * PEER_N_KEYS + i2[..., None, :]).reshape(B, S, PEER_HEADS, PEER_TOPK * PEER_TOPK)
    top_s, pos = lax.top_k(cand_s, PEER_TOPK)
    experts = jnp.take_along_axis(cand_i, pos, axis=-1)
    gates = jax.nn.softmax(top_s, axis=-1).astype(h.dtype)
    nb = (B * S) // PEER_BLOCK
    hk = PEER_HEADS * PEER_TOPK

    def block_fn(args):
        ht, et, gt = args
        act = jax.nn.gelu(jnp.einsum("td,tkd->tk", ht, peer_u[et]), approximate=False)
        return jnp.einsum("tk,tkd->td", gt * act, peer_v[et])

    y = lax.map(block_fn, (h.reshape(nb, PEER_BLOCK, D), experts.reshape(nb, PEER_BLOCK, hk),
                           gates.reshape(nb, PEER_BLOCK, hk)))
    return y.reshape(B, S, D)


def hybrid_layer(x, mem, rel_bias, w_in, q_norm_g, kv_norm_g, w_uq, w_uk, w_uv, w_idx_q,
                 rwkv_mu, rwkv_w0, rwkv_w2, rwkv_a0, rwkv_a2, rwkv_g2, rwkv_k_k, rwkv_k_a, rwkv_r_k,
                 rwkv_lnx_g, rwkv_lnx_b, w_mem_kv, w_br_dsa, w_br_rwkv, w_br_mem, w_out, ln1_g, ln1_b,
                 w_peer_q, peer_keys, peer_u, peer_v, ln2_g, ln2_b):
    B, S, D = x.shape
    cols = x @ w_in
    c_q, c_kv, k_idx, w_idx, rwkv_cols, mem_q, gate_cols = split_cols(cols, IN_COLS)
    y_dsa = dsa_sparse_attention(c_q, c_kv, k_idx, w_idx, q_norm_g, kv_norm_g, w_uq, w_uk, w_uv, w_idx_q, rel_bias)
    y_rwkv = rwkv7_time_mix(rwkv_cols, rwkv_mu, rwkv_w0, rwkv_w2, rwkv_a0, rwkv_a2, rwkv_g2,
                            rwkv_k_k, rwkv_k_a, rwkv_r_k, rwkv_lnx_g, rwkv_lnx_b)
    y_mem = memory_cross_attention(mem_q, mem, w_mem_kv)
    gates = jax.nn.sigmoid(gate_cols.astype(jnp.float32)).astype(x.dtype).reshape(B, S, N_BRANCHES, D)
    merged = (gates[:, :, 0] * (y_dsa @ w_br_dsa)
              + gates[:, :, 1] * (y_rwkv @ w_br_rwkv)
              + gates[:, :, 2] * (y_mem @ w_br_mem))
    x = layernorm(DEEPNORM_ALPHA * x + merged @ w_out, ln1_g, ln1_b)
    x = layernorm(DEEPNORM_ALPHA * x + peer_ffn(x, w_peer_q, peer_keys, peer_u, peer_v), ln2_g, ln2_b)
    return x


def setup_inputs(seed: int = 0) -> dict:
    key = jax.random.key(seed)
    ks = jax.random.split(key, 48)
    counter = [0]
    f32 = jnp.float32
    L = DEPTH

    def nxt():
        counter[0] += 1
        return ks[counter[0] - 1]

    def nrm(shape, scale):
        return jax.random.normal(nxt(), shape, f32) * scale

    def gain(shape):
        return 1.0 + 0.02 * jax.random.normal(nxt(), shape, f32)

    return {
        "x": nrm((BATCH, SEQ, D_MODEL), 1.0),
        "mem": nrm((BATCH, MEM_LEN, D_MODEL), 1.0),
        "rel_bias": nrm((NUM_BUCKETS, DSA_HEADS), 0.5),
        "w_in": nrm((L, D_MODEL, IN_WIDTH), D_MODEL ** -0.5),
        "q_norm_g": gain((L, DSA_Q_RANK)),
        "kv_norm_g": gain((L, DSA_KV_RANK)),
        "w_uq": nrm((L, DSA_Q_RANK, DSA_WIDTH), DSA_Q_RANK ** -0.5),
        "w_uk": nrm((L, DSA_KV_RANK, DSA_HEADS, DSA_HEAD_DIM), DSA_HEAD_DIM ** -0.5),
        "w_uv": nrm((L, DSA_KV_RANK, DSA_HEADS, DSA_HEAD_DIM), DSA_KV_RANK ** -0.5),
        "w_idx_q": nrm((L, DSA_Q_RANK, IDX_HEADS * IDX_DIM), DSA_Q_RANK ** -0.5),
        "rwkv_mu": jax.random.uniform(nxt(), (L, RWKV_COLS), f32),
        "rwkv_w0": nrm((L, RWKV_WIDTH), 0.5),
        "rwkv_w2": nrm((L, DECAY_RANK, RWKV_WIDTH), 0.1),
        "rwkv_a0": nrm((L, RWKV_WIDTH), 0.1),
        "rwkv_a2": nrm((L, AAA_RANK, RWKV_WIDTH), 0.1),
        "rwkv_g2": nrm((L, GATE_RANK, RWKV_WIDTH), GATE_RANK ** -0.5),
        "rwkv_k_k": 0.85 + nrm((L, RWKV_WIDTH), 0.02),
        "rwkv_k_a": gain((L, RWKV_WIDTH)),
        "rwkv_r_k": nrm((L, RWKV_HEADS, RWKV_HEAD_DIM), 0.1),
        "rwkv_lnx_g": gain((L, RWKV_WIDTH)),
        "rwkv_lnx_b": nrm((L, RWKV_WIDTH), 0.02),
        "w_mem_kv": nrm((L, D_MODEL, 2 * MEM_WIDTH), D_MODEL ** -0.5),
        "w_br_dsa": nrm((L, DSA_WIDTH, D_MODEL), DSA_WIDTH ** -0.5),
        "w_br_rwkv": nrm((L, RWKV_WIDTH, D_MODEL), RWKV_WIDTH ** -0.5),
        "w_br_mem": nrm((L, MEM_WIDTH, D_MODEL), MEM_WIDTH ** -0.5),
        "w_out": nrm((L, D_MODEL, D_MODEL), D_MODEL ** -0.5 * DEEPNORM_BETA),
        "ln1_g": gain((L, D_MODEL)),
        "ln1_b": nrm((L, D_MODEL), 0.02),
        "w_peer_q": nrm((L, D_MODEL, PEER_HEADS * PEER_QUERY_DIM), D_MODEL ** -0.5),
        "peer_keys": nrm((L, 2, PEER_N_KEYS, PEER_HALF), PEER_HALF ** -0.5),
        "peer_u": nrm((L, PEER_N_EXPERTS, D_MODEL), D_MODEL ** -0.5),
        "peer_v": nrm((L, PEER_N_EXPERTS, D_MODEL), PEER_HEADS ** -0.5 * DEEPNORM_BETA),
        "ln2_g": gain((L, D_MODEL)),
        "ln2_b": nrm((L, D_MODEL), 0.02),
    }


def reference(x, mem, rel_bias, w_in, q_norm_g, kv_norm_g, w_uq, w_uk, w_uv, w_idx_q,
              rwkv_mu, rwkv_w0, rwkv_w2, rwkv_a0, rwkv_a2, rwkv_g2, rwkv_k_k, rwkv_k_a, rwkv_r_k,
              rwkv_lnx_g, rwkv_lnx_b, w_mem_kv, w_br_dsa, w_br_rwkv, w_br_mem, w_out, ln1_g, ln1_b,
              w_peer_q, peer_keys, peer_u, peer_v, ln2_g, ln2_b):
    for l in range(DEPTH):
        x = hybrid_layer(x, mem, rel_bias, w_in[l], q_norm_g[l], kv_norm_g[l], w_uq[l], w_uk[l], w_uv[l],
                         w_idx_q[l], rwkv_mu[l], rwkv_w0[l], rwkv_w2[l], rwkv_a0[l], rwkv_a2[l], rwkv_g2[l],
                         rwkv_k_k[l], rwkv_k_a[l], rwkv_r_k[l], rwkv_lnx_g[l], rwkv_lnx_b[l], w_mem_kv[l],
                         w_br_dsa[l], w_br_rwkv[l], w_br_mem[l], w_out[l], ln1_g[l], ln1_b[l],
                         w_peer_q[l], peer_keys[l], peer_u[l], peer_v[l], ln2_g[l], ln2_b[l])
    return x
```

```python
import functools
import math

import jax
import jax.numpy as jnp
from jax import lax
from jax.experimental import pallas as pl
from jax.experimental.pallas import tpu as pltpu

F32 = jnp.float32
BF16 = jnp.bfloat16
I32 = jnp.int32

DSA_HEADS = 8
DSA_HEAD_DIM = 64
DSA_Q_RANK = 256
DSA_KV_RANK = 128
IDX_HEADS = 8
IDX_DIM = 32
INDEX_TOPK_MAX = 256
NUM_BUCKETS = 32
MAX_DISTANCE = 128
RWKV_HEADS = 8
RWKV_HEAD_DIM = 64
RWKV_WIDTH = RWKV_HEADS * RWKV_HEAD_DIM
DECAY_RANK = 64
AAA_RANK = 64
GATE_RANK = 128
GN_EPS = 64e-5
MEM_HEADS = 4
MEM_HEAD_DIM = 128
PEER_HEADS = 8
PEER_N_KEYS = 128
PEER_HALF = 128
PEER_TOPK = 16
LN_EPS = 1e-5
RMS_EPS = 1e-6

LANES = 128
VMEM_LIMIT = 56 * 1024 * 1024

INT_MIN = -(2 ** 31)
NEG_INF = float("-inf")


def _cparams(sem, vmem=VMEM_LIMIT):
    return pltpu.CompilerParams(dimension_semantics=sem, vmem_limit_bytes=vmem)


def _dot(a, b):
    return jnp.dot(a, b, preferred_element_type=F32)


def _dot_nt(a, b):
    return lax.dot_general(a, b, (((1,), (1,)), ((), ())), preferred_element_type=F32)


def _split2(x):
    hi = x.astype(BF16)
    lo = (x - hi.astype(F32)).astype(BF16)
    return hi, lo


def _dot3(a, b, nt=False):
    d = _dot_nt if nt else _dot
    ah, al = _split2(a)
    bh, bl = _split2(b)
    return d(ah, bh) + (d(ah, bl) + d(al, bh))


def _linear_kernel(x_ref, w_ref, o_ref, *, act):
    y = _dot(x_ref[...].astype(BF16), w_ref[...])
    if act == "sigmoid":
        y = jax.nn.sigmoid(y)
    o_ref[...] = y.astype(o_ref.dtype)


def _linear(x, w, out_dtype, tm, tn, act=None, name="linear"):
    m, k = x.shape
    n = w.shape[1]
    return pl.pallas_call(
        functools.partial(_linear_kernel, act=act),
        out_shape=jax.ShapeDtypeStruct((m, n), out_dtype),
        grid=(m // tm, n // tn),
        in_specs=[pl.BlockSpec((tm, k), lambda i, j: (i, 0)),
                  pl.BlockSpec((k, tn), lambda i, j: (0, j))],
        out_specs=pl.BlockSpec((tm, tn), lambda i, j: (i, j)),
        compiler_params=_cparams(("parallel", "parallel")),
        name=name,
    )(x, w)


def _dsa_prep_kernel(dc_ref, qg_ref, kvg_ref, wuq_ref, wuk_ref, wiq_ref,
                     ql_ref, qi_ref, wi_ref, ki_ref, kv_ref):
    dc = dc_ref[0]
    c_q = dc[:, :DSA_Q_RANK]
    c_kv = dc[:, DSA_Q_RANK:DSA_Q_RANK + DSA_KV_RANK]
    off = DSA_Q_RANK + DSA_KV_RANK
    k_idx = dc[:, off:off + IDX_DIM]
    w_idx = dc[:, off + IDX_DIM:off + IDX_DIM + IDX_HEADS]
    c_q = c_q * lax.rsqrt(jnp.mean(c_q * c_q, axis=-1, keepdims=True) + RMS_EPS) * qg_ref[...]
    c_kv = c_kv * lax.rsqrt(jnp.mean(c_kv * c_kv, axis=-1, keepdims=True) + RMS_EPS) * kvg_ref[...]
    cqb = c_q.astype(BF16)
    scale = DSA_HEAD_DIM ** -0.5
    for h in range(DSA_HEADS):
        q_h = _dot(cqb, wuq_ref[h])
        ql_ref[0, h] = (_dot(q_h.astype(BF16), wuk_ref[h]) * scale).astype(BF16)
        qi_ref[0, h] = _dot(cqb, wiq_ref[h]).astype(BF16)
    wi_ref[0] = w_idx * (IDX_HEADS ** -0.5 * IDX_DIM ** -0.5)
    ki_ref[0] = k_idx.astype(BF16)
    kv_ref[0] = c_kv.astype(BF16)


def _dsa_prep(dcols, q_norm_g, kv_norm_g, w_uq, w_uk, w_idx_q, tm):
    b, s, wd = dcols.shape
    h = DSA_HEADS
    wuq_h = w_uq.reshape(DSA_Q_RANK, h, DSA_HEAD_DIM).transpose(1, 0, 2).astype(BF16)
    wuk_h = w_uk.transpose(1, 2, 0).astype(BF16)
    wiq_h = w_idx_q.reshape(DSA_Q_RANK, IDX_HEADS, IDX_DIM).transpose(1, 0, 2).astype(BF16)
    full = lambda shape: pl.BlockSpec(shape, lambda i, j: (0,) * len(shape))
    return pl.pallas_call(
        _dsa_prep_kernel,
        out_shape=(jax.ShapeDtypeStruct((b, h, s, DSA_KV_RANK), BF16),
                   jax.ShapeDtypeStruct((b, IDX_HEADS, s, IDX_DIM), BF16),
                   jax.ShapeDtypeStruct((b, s, IDX_HEADS), F32),
                   jax.ShapeDtypeStruct((b, s, IDX_DIM), BF16),
                   jax.ShapeDtypeStruct((b, s, DSA_KV_RANK), BF16)),
        grid=(b, s // tm),
        in_specs=[pl.BlockSpec((1, tm, wd), lambda i, j: (i, j, 0)),
                  full((1, DSA_Q_RANK)), full((1, DSA_KV_RANK)),
                  full(wuq_h.shape), full(wuk_h.shape), full(wiq_h.shape)],
        out_specs=(pl.BlockSpec((1, h, tm, DSA_KV_RANK), lambda i, j: (i, 0, j, 0)),
                   pl.BlockSpec((1, IDX_HEADS, tm, IDX_DIM), lambda i, j: (i, 0, j, 0)),
                   pl.BlockSpec((1, tm, IDX_HEADS), lambda i, j: (i, j, 0)),
                   pl.BlockSpec((1, tm, IDX_DIM), lambda i, j: (i, j, 0)),
                   pl.BlockSpec((1, tm, DSA_KV_RANK), lambda i, j: (i, j, 0))),
        compiler_params=_cparams(("parallel", "parallel")),
        name="dsa_prep",
    )(dcols, q_norm_g.reshape(1, -1), kv_norm_g.reshape(1, -1), wuq_h, wuk_h, wiq_h)


DSA_QB = 128
DSA_KB = 128
DSA_SC = 256


def _t5_bias_tile(rel_ref, h, off):
    row = lax.broadcasted_iota(I32, (DSA_QB, DSA_KB), 0)
    col = lax.broadcasted_iota(I32, (DSA_QB, DSA_KB), 1)
    n = jnp.maximum(row - col + off, 0)
    max_exact = NUM_BUCKETS // 2
    nf = jnp.maximum(n, 1).astype(F32)
    large = max_exact + (jnp.log(nf / max_exact) / math.log(MAX_DISTANCE / max_exact)
                         * (NUM_BUCKETS - max_exact)).astype(I32)
    large = jnp.minimum(large, NUM_BUCKETS - 1)
    bucket = jnp.where(n < max_exact, n, large)
    last = rel_ref[NUM_BUCKETS - 1, h]
    out = jnp.zeros((DSA_QB, DSA_KB), F32)
    for bk in range(NUM_BUCKETS - 1):
        out = jnp.where(bucket == bk, rel_ref[bk, h] - last, out)
    return out


def _dsa_attn_kernel(rel_ref, qi_ref, wi_ref, ql_ref, ki_ref, kv_ref, wuv_ref, o_ref,
                     key_sc, bias_sc, m_sc, l_sc, acc_sc, *, topk, seq):
    H = DSA_HEADS
    QB, KB, SC = DSA_QB, DSA_KB, DSA_SC
    qb = pl.program_id(1)
    q0 = qb * QB

    @pl.when((pl.program_id(0) == 0) & (qb == 0))
    def _():
        for h in range(H):
            bias_sc[0, h] = _t5_bias_tile(rel_ref, h, 0)
            bias_sc[1, h] = _t5_bias_tile(rel_ref, h, KB)

    n_sc = (q0 + QB + SC - 1) // SC
    wi = wi_ref[0]
    row_g = q0 + lax.broadcasted_iota(I32, (QB, SC), 0)
    col_l = lax.broadcasted_iota(I32, (QB, SC), 1)

    def score_body(j, carry):
        k0 = pl.multiple_of(j * SC, SC)
        ks = ki_ref[0, pl.ds(k0, SC), :]
        acc = jnp.zeros((QB, SC), F32)
        for h in range(IDX_HEADS):
            lg = _dot_nt(qi_ref[0, h], ks)
            acc = acc + wi[:, h:h + 1] * jnp.maximum(lg, 0.0)
        acc = jnp.where(acc == 0.0, 0.0, acc)
        bits = pltpu.bitcast(acc, I32)
        key = bits ^ ((bits >> 31) & 0x7FFFFFFF)
        key = jnp.where(col_l + k0 <= row_g, key, INT_MIN)
        key_sc[:, pl.ds(k0, SC)] = key
        return carry

    lax.fori_loop(0, n_sc, score_body, 0)

    def count(pred):
        def body(j, acc):
            k0 = pl.multiple_of(j * SC, SC)
            return acc + jnp.where(pred(key_sc[:, pl.ds(k0, SC)], k0), 1, 0)
        acc = lax.fori_loop(0, n_sc, body, jnp.zeros((QB, SC), I32))
        return jnp.sum(acc, axis=1, keepdims=True)

    def bis_body(i, ru):
        bit = lax.shift_left(jnp.int32(1), 31 - i)
        cand = jnp.broadcast_to((ru | bit) ^ INT_MIN, (QB, SC))
        cnt = count(lambda blk, k0: blk >= cand)
        return jnp.where(cnt >= topk, ru | bit, ru)

    ru = lax.fori_loop(0, 32, bis_body, jnp.zeros((QB, 1), I32))
    tau = jnp.maximum(ru ^ INT_MIN, INT_MIN + 1)
    tau_b = jnp.broadcast_to(tau, (QB, SC))
    cnt_ge = count(lambda blk, k0: blk >= tau_b)

    @pl.when(jnp.max(cnt_ge) > topk)
    def _():
        cnt_gt = count(lambda blk, k0: blk > tau_b)
        need = topk - cnt_gt

        def pos_body(_, lohi):
            lo, hi = lohi
            mid = (lo + hi) >> 1
            mid_b = jnp.broadcast_to(mid, (QB, SC))
            f = count(lambda blk, k0: (blk == tau_b) & (col_l + k0 <= mid_b))
            ok = f >= need
            return jnp.where(ok, lo, mid + 1), jnp.where(ok, mid, hi)

        lo, _hi = lax.fori_loop(0, max(1, (seq - 1).bit_length()), pos_body,
                                (jnp.zeros((QB, 1), I32), jnp.full((QB, 1), seq - 1, I32)))
        jstar = jnp.broadcast_to(lo, (QB, SC))

        def fix_body(j, carry):
            k0 = pl.multiple_of(j * SC, SC)
            blk = key_sc[:, pl.ds(k0, SC)]
            key_sc[:, pl.ds(k0, SC)] = jnp.where((blk == tau_b) & (col_l + k0 > jstar), INT_MIN, blk)
            return carry

        lax.fori_loop(0, n_sc, fix_body, 0)

    m_sc[...] = jnp.full(m_sc.shape, NEG_INF, F32)
    l_sc[...] = jnp.zeros(l_sc.shape, F32)
    acc_sc[...] = jnp.zeros(acc_sc.shape, F32)
    ql = ql_ref[0].reshape(H * QB, DSA_KV_RANK)
    tau_k = jnp.broadcast_to(tau, (QB, KB))

    def attend(j, bias):
        k0 = pl.multiple_of(j * KB, KB)
        kvb = kv_ref[0, pl.ds(k0, KB), :]
        s = _dot_nt(ql, kvb).reshape(H, QB, KB)
        if bias is not None:
            s = s + bias
        mask = key_sc[:, pl.ds(k0, KB)] >= tau_k
        s = jnp.where(mask[None], s, NEG_INF)
        m_old = m_sc[...]
        m_new = jnp.maximum(m_old, jnp.max(s, axis=-1, keepdims=True))
        m_safe = jnp.where(m_new == NEG_INF, 0.0, m_new)
        p = jnp.exp(s - m_safe)
        alpha = jnp.exp(m_old - m_safe)
        l_sc[...] = alpha * l_sc[...] + jnp.sum(p, axis=-1, keepdims=True)
        pv = _dot(p.reshape(H * QB, KB).astype(BF16), kvb).reshape(H, QB, DSA_KV_RANK)
        acc_sc[...] = alpha * acc_sc[...] + pv
        m_sc[...] = m_new

    def far_body(j, carry):
        attend(j, None)
        return carry

    lax.fori_loop(0, jnp.maximum(qb - 1, 0), far_body, 0)

    @pl.when(qb >= 1)
    def _():
        attend(qb - 1, bias_sc[1])

    attend(qb, bias_sc[0])

    o_lat = acc_sc[...] / l_sc[...]
    y = jnp.zeros((QB, DSA_HEADS * DSA_HEAD_DIM), F32)
    for h in range(H):
        y = y + _dot(o_lat[h].astype(BF16), wuv_ref[h])
    o_ref[0] = y.astype(o_ref.dtype)


def _dsa_attn(rel_bias, q_idx, w_idx, q_lat, k_idx, c_kv, w_uv):
    b, h, s, _ = q_lat.shape
    topk = min(INDEX_TOPK_MAX, s // 4)
    width = DSA_HEADS * DSA_HEAD_DIM
    eye = jnp.eye(DSA_HEADS, dtype=F32)
    wuv_e = jnp.einsum("chd,hg->hcgd", w_uv, eye).reshape(DSA_HEADS, DSA_KV_RANK, width).astype(BF16)
    QB = DSA_QB
    s_pad = -(-s // DSA_SC) * DSA_SC
    return pl.pallas_call(
        functools.partial(_dsa_attn_kernel, topk=topk, seq=s),
        out_shape=jax.ShapeDtypeStruct((b, s, width), BF16),
        grid=(b, s // QB),
        in_specs=[pl.BlockSpec(memory_space=pltpu.SMEM),
                  pl.BlockSpec((1, IDX_HEADS, QB, IDX_DIM), lambda i, j: (i, 0, j, 0)),
                  pl.BlockSpec((1, QB, IDX_HEADS), lambda i, j: (i, j, 0)),
                  pl.BlockSpec((1, h, QB, DSA_KV_RANK), lambda i, j: (i, 0, j, 0)),
                  pl.BlockSpec((1, s, IDX_DIM), lambda i, j: (i, 0, 0)),
                  pl.BlockSpec((1, s, DSA_KV_RANK), lambda i, j: (i, 0, 0)),
                  pl.BlockSpec(wuv_e.shape, lambda i, j: (0, 0, 0))],
        out_specs=pl.BlockSpec((1, QB, width), lambda i, j: (i, j, 0)),
        scratch_shapes=[pltpu.VMEM((QB, s_pad), I32),
                        pltpu.VMEM((2, h, QB, DSA_KB), F32),
                        pltpu.VMEM((h, QB, 1), F32),
                        pltpu.VMEM((h, QB, 1), F32),
                        pltpu.VMEM((h, QB, DSA_KV_RANK), F32)],
        compiler_params=_cparams(("arbitrary", "arbitrary")),
        name="dsa_attn",
    )(rel_bias, q_idx, w_idx, q_lat, k_idx, c_kv, wuv_e)


def _rwkv_prep_kernel(c_ref, p_ref, mu_ref, w0_ref, a0_ref, kk_ref, ka_ref, rk_ref,
                      wwa_ref, g2_ref, ones_ref,
                      r_o, k_o, v_o, kk_o, b_o, lw_o, bonus_o, g_o, *, tiles_per_seq):
    W = RWKV_WIDTH
    cols = c_ref[...]
    tm = cols.shape[0]
    first = (pl.program_id(0) % tiles_per_seq) == 0
    prev_row = jnp.where(first, 0.0, p_ref[7:8, :])
    rolled = pltpu.roll(cols, 1, 0)
    row = lax.broadcasted_iota(I32, cols.shape, 0)
    prev = jnp.where(row == 0, prev_row, rolled)
    xs = cols + (prev - cols) * mu_ref[...]
    r = xs[:, 0:W]
    k = xs[:, W:2 * W]
    v = xs[:, 2 * W:3 * W]
    lora = xs[:, 3 * W:3 * W + DECAY_RANK + AAA_RANK]
    gl = xs[:, 3 * W + DECAY_RANK + AAA_RANK:]
    lane = lax.broadcasted_iota(I32, lora.shape, 1)
    lora = jnp.where(lane < DECAY_RANK, jnp.tanh(lora), lora)
    wa = _dot(lora.astype(BF16), wwa_ref[...])
    w = -jax.nn.softplus(-(w0_ref[...] + wa[:, :W])) - 0.5
    lw = -jnp.exp(w)
    a = jax.nn.sigmoid(a0_ref[...] + wa[:, W:])
    g = _dot(jax.nn.sigmoid(gl).astype(BF16), g2_ref[...])
    ones_bd = ones_ref[...]

    def head_sum(t):
        hi, lo = _split2(t)
        return _dot(hi, ones_bd) + _dot(lo, ones_bd)

    kk = k * kk_ref[...]
    kk = kk / jnp.maximum(jnp.sqrt(head_sum(kk * kk)), 1e-12)
    k2 = k * (1.0 + (a - 1.0) * ka_ref[...])
    bonus = head_sum(r * k2 * rk_ref[...]) * v
    bvec = kk * a
    n = RWKV_HEAD_DIM
    for h in range(RWKV_HEADS):
        sl = slice(h * n, (h + 1) * n)
        r_o[0, h] = r[:, sl]
        k_o[0, h] = k2[:, sl]
        v_o[0, h] = v[:, sl]
        kk_o[0, h] = kk[:, sl]
        b_o[0, h] = bvec[:, sl]
        lw_o[0, h] = lw[:, sl]
        bonus_o[0, h] = bonus[:, sl]
        g_o[0, h] = g[:, sl]


def _rwkv_prep(cols, b, s, mu, w0, w2, a0, a2, g2, k_k, k_a, r_k, tm):
    t, wc = cols.shape
    W = RWKV_WIDTH
    wwa = jnp.zeros((DECAY_RANK + AAA_RANK, 2 * W), F32)
    wwa = wwa.at[:DECAY_RANK, :W].set(w2).at[DECAY_RANK:, W:].set(a2).astype(BF16)
    head_id = jnp.arange(W) // RWKV_HEAD_DIM
    ones_bd = (head_id[:, None] == head_id[None, :]).astype(BF16)
    tiles = s // tm
    row = lambda a: a.reshape(1, -1)
    vec = lambda n: pl.BlockSpec((1, n), lambda i: (0, 0))
    hm = jax.ShapeDtypeStruct((b, RWKV_HEADS, s, RWKV_HEAD_DIM), F32)
    hm_spec = pl.BlockSpec((1, RWKV_HEADS, tm, RWKV_HEAD_DIM),
                           lambda i: (i // tiles, 0, i % tiles, 0))
    return pl.pallas_call(
        functools.partial(_rwkv_prep_kernel, tiles_per_seq=tiles),
        out_shape=(hm,) * 8,
        grid=(t // tm,),
        in_specs=[pl.BlockSpec((tm, wc), lambda i: (i, 0)),
                  pl.BlockSpec((8, wc), lambda i: (jnp.maximum(i * (tm // 8) - 1, 0), 0)),
                  vec(wc), vec(W), vec(W), vec(W), vec(W), vec(W),
                  pl.BlockSpec(wwa.shape, lambda i: (0, 0)),
                  pl.BlockSpec((GATE_RANK, W), lambda i: (0, 0)),
                  pl.BlockSpec((W, W), lambda i: (0, 0))],
        out_specs=(hm_spec,) * 8,
        compiler_params=_cparams(("parallel",)),
        name="rwkv_prep",
    )(cols, cols, row(mu), row(w0), row(a0), row(k_k), row(k_a), row(r_k.reshape(-1)),
      wwa, g2.astype(BF16), ones_bd)


RWKV_CHUNK = 64


def _rwkv_chunk_kernel(r_ref, k_ref, v_ref, kk_ref, b_ref, lw_ref, bonus_ref, g_ref,
                       lg_ref, lb_ref, o_ref, s_sc):
    C = RWKV_CHUNK
    N = RWKV_HEAD_DIM

    @pl.when(pl.program_id(1) == 0)
    def _():
        s_sc[...] = jnp.zeros(s_sc.shape, F32)

    ti = lax.broadcasted_iota(I32, (C, C), 0)
    si = lax.broadcasted_iota(I32, (C, C), 1)
    strict = si < ti
    incl = si <= ti
    tri = jnp.where(incl, 1.0, 0.0).astype(BF16)
    eye = jnp.where(si == ti, 1.0, 0.0).astype(F32)

    for h in range(RWKV_HEADS):
        r = r_ref[0, h]
        k = k_ref[0, h]
        v = v_ref[0, h]
        kk = kk_ref[0, h]
        b = b_ref[0, h]
        lw = lw_ref[0, h]
        hi = lw.astype(BF16)
        r1 = lw - hi.astype(F32)
        mid = r1.astype(BF16)
        lo = (r1 - mid.astype(F32)).astype(BF16)
        cum = _dot(tri, hi) + (_dot(tri, mid) + _dot(tri, lo))
        cum_last = cum[C - 1:C, :]
        p_in = jnp.exp(cum)
        p_inv = jnp.exp(-cum)
        a_t = -kk * jnp.exp(cum - lw)
        b_t = b * p_inv
        k_t = k * p_inv
        r_t = r * p_in
        b_e = b * jnp.exp(cum_last - cum)
        k_e = k * jnp.exp(cum_last - cum)
        p_c = jnp.exp(cum_last)

        ar = jnp.concatenate([a_t, r_t], axis=0)
        sb = _dot3(ar, b_t, nt=True)
        sk = _dot3(ar, k_t, nt=True)
        l_ab = jnp.where(strict, sb[:C], 0.0)
        l_ak = jnp.where(strict, sk[:C], 0.0)
        m_rb = jnp.where(incl, sb[C:], 0.0)
        m_rk = jnp.where(incl, sk[C:], 0.0)

        t_inv = eye + l_ab
        lp = l_ab
        steps = max(1, (C - 1).bit_length()) - 1
        for _ in range(steps):
            lp = _dot3(lp, lp)
            t_inv = t_inv + _dot3(t_inv, lp)

        a_bar = _dot3(t_inv, a_t)
        u0 = _dot3(t_inv, _dot3(l_ak, v))
        r_hat = r_t + _dot3(m_rb, a_bar)
        y0 = _dot3(m_rb, u0) + _dot3(m_rk, v)

        s_prev = s_sc[h]
        ws = _dot3(jnp.concatenate([a_bar, r_hat], axis=0), s_prev, nt=True)
        wc = ws[:C] + u0
        y = ws[C:] + y0
        s_sc[h] = s_prev * p_c + _dot3(wc.T, b_e) + _dot3(v.T, k_e)

        mu_y = jnp.mean(y, axis=-1, keepdims=True)
        yc = y - mu_y
        var_y = jnp.mean(yc * yc, axis=-1, keepdims=True)
        yn = yc * lax.rsqrt(var_y + GN_EPS) * lg_ref[h] + lb_ref[h]
        o_ref[0, h] = ((yn + bonus_ref[0, h]) * g_ref[0, h]).astype(o_ref.dtype)


def _rwkv_chunk(r, k, v, kk, bvec, lw, bonus, g, lnx_g, lnx_b):
    b, h, s, n = r.shape
    C = RWKV_CHUNK
    spec = pl.BlockSpec((1, h, C, n), lambda i, j: (i, 0, j, 0))
    pspec = pl.BlockSpec((h, 1, n), lambda i, j: (0, 0, 0))
    return pl.pallas_call(
        _rwkv_chunk_kernel,
        out_shape=jax.ShapeDtypeStruct((b, h, s, n), BF16),
        grid=(b, s // C),
        in_specs=[spec] * 8 + [pspec, pspec],
        out_specs=spec,
        scratch_shapes=[pltpu.VMEM((h, n, n), F32)],
        compiler_params=_cparams(("arbitrary", "arbitrary")),
        name="rwkv_chunk",
    )(r, k, v, kk, bvec, lw, bonus, g, lnx_g.reshape(h, 1, n), lnx_b.reshape(h, 1, n))


def _mem_attn_kernel(q_ref, kv_ref, o_ref):
    W = MEM_HEADS * MEM_HEAD_DIM
    scale = MEM_HEAD_DIM ** -0.5
    for h in range(MEM_HEADS):
        sl = slice(h * MEM_HEAD_DIM, (h + 1) * MEM_HEAD_DIM)
        q = q_ref[0, :, sl]
        k = kv_ref[0, :, sl]
        v = kv_ref[0, :, W + h * MEM_HEAD_DIM:W + (h + 1) * MEM_HEAD_DIM]
        s = _dot_nt(q, k) * scale
        s = s - jnp.max(s, axis=-1, keepdims=True)
        p = jnp.exp(s)
        p = p / jnp.sum(p, axis=-1, keepdims=True)
        o_ref[0, :, sl] = _dot(p.astype(BF16), v).astype(o_ref.dtype)


def _mem_attn(q, kv, tq):
    b, s, w = q.shape
    m = kv.shape[1]
    return pl.pallas_call(
        _mem_attn_kernel,
        out_shape=jax.ShapeDtypeStruct((b, s, w), BF16),
        grid=(b, s // tq),
        in_specs=[pl.BlockSpec((1, tq, w), lambda i, j: (i, j, 0)),
                  pl.BlockSpec((1, m, 2 * w), lambda i, j: (i, 0, 0))],
        out_specs=pl.BlockSpec((1, tq, w), lambda i, j: (i, j, 0)),
        compiler_params=_cparams(("parallel", "parallel")),
        name="mem_attn",
    )(q, kv)


def _layernorm(z, g, b):
    mu = jnp.mean(z, axis=-1, keepdims=True)
    zc = z - mu
    var = jnp.mean(zc * zc, axis=-1, keepdims=True)
    return zc * lax.rsqrt(var + LN_EPS) * g + b


def _merge_kernel(x_ref, yd_ref, yr_ref, ym_ref, gt_ref, wd_ref, wr_ref, wm_ref, wo_ref,
                  g_ref, b_ref, x1_ref, x1b_ref, x1t_ref, *, alpha):
    d = x_ref.shape[-1]
    br_d = _dot(yd_ref[0], wd_ref[...])
    br_r = jnp.zeros_like(br_d)
    for h in range(RWKV_HEADS):
        br_r = br_r + _dot(yr_ref[0, h], wr_ref[h])
    br_m = _dot(ym_ref[0], wm_ref[...])
    gt = gt_ref[0]
    merged = (gt[:, 0:d].astype(F32) * br_d + gt[:, d:2 * d].astype(F32) * br_r
              + gt[:, 2 * d:3 * d].astype(F32) * br_m)
    z = alpha * x_ref[0] + _dot(merged.astype(BF16), wo_ref[...])
    x1 = _layernorm(z, g_ref[...], b_ref[...])
    x1_ref[0] = x1
    x1b_ref[0] = x1.astype(BF16)
    x1t_ref[0] = x1.T.astype(BF16)


def _merge(x, y_dsa, y_rwkv, y_mem, gates, w_br_dsa, w_br_rwkv, w_br_mem, w_out, ln_g, ln_b, alpha, tm):
    b, s, d = x.shape
    n = RWKV_HEAD_DIM
    full2 = lambda a: pl.BlockSpec(a.shape, lambda i, j: (0,) * a.ndim)
    wd = w_br_dsa.astype(BF16)
    wr = w_br_rwkv.reshape(RWKV_HEADS, n, d).astype(BF16)
    wm = w_br_mem.astype(BF16)
    wo = w_out.astype(BF16)
    g2 = ln_g.reshape(1, d)
    b2 = ln_b.reshape(1, d)
    tok = lambda w: pl.BlockSpec((1, tm, w), lambda i, j: (i, j, 0))
    return pl.pallas_call(
        functools.partial(_merge_kernel, alpha=alpha),
        out_shape=(jax.ShapeDtypeStruct((b, s, d), F32),
                   jax.ShapeDtypeStruct((b, s, d), BF16),
                   jax.ShapeDtypeStruct((b, d, s), BF16)),
        grid=(b, s // tm),
        in_specs=[tok(d), tok(y_dsa.shape[-1]),
                  pl.BlockSpec((1, RWKV_HEADS, tm, n), lambda i, j: (i, 0, j, 0)),
                  tok(y_mem.shape[-1]), tok(3 * d),
                  full2(wd), full2(wr), full2(wm), full2(wo), full2(g2), full2(b2)],
        out_specs=(tok(d), tok(d), pl.BlockSpec((1, d, tm), lambda i, j: (i, 0, j))),
        compiler_params=_cparams(("parallel", "parallel")),
        name="merge",
    )(x, y_dsa, y_rwkv, y_mem, gates, wd, wr, wm, wo, g2, b2)


PEER_RANKS = PEER_TOPK + 1


def _peer_route_kernel(x_ref, wq_ref, keys_ref, thr_o, c1_o, s2_o, p2_o, a1_sc, a2_sc):
    tt = x_ref.shape[1]
    nk = PEER_N_KEYS
    R = PEER_RANKS
    q = _dot(x_ref[0], wq_ref[...])
    keys0 = keys_ref[0]
    keys1 = keys_ref[1]

    def top_ranks(sub, sc):
        x = sub
        for rnk in range(R):
            m = jnp.max(x, axis=0, keepdims=True)
            sc[rnk:rnk + 1, :] = m
            x = jnp.where(x == m, NEG_INF, x)

    for h in range(PEER_HEADS):
        q1 = q[:, (2 * h) * PEER_HALF:(2 * h + 1) * PEER_HALF]
        q2 = q[:, (2 * h + 1) * PEER_HALF:(2 * h + 2) * PEER_HALF]
        s1 = _dot3(keys0, q1, nt=True)
        s2 = _dot3(keys1, q2, nt=True)
        top_ranks(s1, a1_sc)
        top_ranks(s2, a2_sc)
        pieces = []
        for i in range(R):
            nj = R // (i + 1)
            pieces.append(a1_sc[i:i + 1, :] + a2_sc[0:nj, :])
        best = pieces[0][0:1, :]
        zsum = jnp.zeros((1, tt), F32)
        v_prev = best
        v_cur = best
        work = pieces
        for rnk in range(R):
            m = work[0].max(axis=0, keepdims=True)
            for pc in work[1:]:
                m = jnp.maximum(m, pc.max(axis=0, keepdims=True))
            v_prev, v_cur = v_cur, m
            if rnk < PEER_TOPK:
                zsum = zsum + jnp.exp(m - best)
            work = [jnp.where(pc == m, NEG_INF, pc) for pc in work]
        thr = 0.5 * (v_prev + v_cur)
        m1 = a1_sc[0:1, :]
        m2 = a2_sc[0:1, :]
        thr_o[0, h] = thr - s1
        c1_o[0, h] = jnp.exp(s1 - m1) / zsum
        s2_o[0, h] = s2
        p2_o[0, h] = jnp.exp(s2 - m2)


def _peer_route(x1b, w_peer_q, peer_keys, tt):
    b, s, d = x1b.shape
    nk = PEER_N_KEYS
    wq = w_peer_q.astype(BF16)
    out = jax.ShapeDtypeStruct((b, PEER_HEADS, nk, s), F32)
    ospec = pl.BlockSpec((1, PEER_HEADS, nk, tt), lambda i, j: (i, 0, 0, j))
    return pl.pallas_call(
        _peer_route_kernel,
        out_shape=(out,) * 4,
        grid=(b, s // tt),
        in_specs=[pl.BlockSpec((1, tt, d), lambda i, j: (i, j, 0)),
                  pl.BlockSpec(wq.shape, lambda i, j: (0, 0)),
                  pl.BlockSpec(peer_keys.shape, lambda i, j: (0, 0, 0))],
        out_specs=(ospec,) * 4,
        scratch_shapes=[pltpu.VMEM((24, tt), F32), pltpu.VMEM((24, tt), F32)],
        compiler_params=_cparams(("parallel", "parallel")),
        name="peer_route",
    )(x1b, wq, peer_keys)


PEER_ET = 512


def _peer_ffn_kernel(x1_ref, x1t_ref, thr_ref, c1_ref, s2_ref, p2_ref, u_ref, vt_ref,
                     g_ref, b_ref, o_ref, acc_sc, *, alpha):
    e = pl.program_id(2)
    nk = PEER_N_KEYS
    tt = x1t_ref.shape[2]

    @pl.when(e == 0)
    def _():
        acc_sc[...] = jnp.zeros(acc_sc.shape, F32)

    act = _dot(u_ref[...], x1t_ref[0])
    act = 0.5 * act * (1.0 + lax.erf(act * (2.0 ** -0.5)))
    rows = PEER_ET // nk
    gate_rows = []
    for rr in range(rows):
        i1 = e * rows + rr
        gsum = jnp.zeros((nk, tt), F32)
        for h in range(PEER_HEADS):
            thr = thr_ref[0, h, pl.ds(i1, 1), :]
            c1 = c1_ref[0, h, pl.ds(i1, 1), :]
            sel = s2_ref[0, h] >= thr
            gsum = gsum + jnp.where(sel, p2_ref[0, h], 0.0) * c1
        gate_rows.append(gsum)
    gate = jnp.concatenate(gate_rows, axis=0)
    w = (gate * act).astype(BF16)
    acc_sc[...] += _dot(vt_ref[...], w)

    @pl.when(e == pl.num_programs(2) - 1)
    def _():
        z = alpha * x1_ref[0] + acc_sc[...].T
        o_ref[0] = _layernorm(z, g_ref[...], b_ref[...])


def _peer_ffn(x1, x1t, thr, c1, s2, p2, peer_u, peer_v, ln_g, ln_b, alpha, tt):
    b, s, d = x1.shape
    ne = peer_u.shape[0]
    nk = PEER_N_KEYS
    u = peer_u.astype(BF16)
    vt = peer_v.T.astype(BF16)
    rspec = pl.BlockSpec((1, PEER_HEADS, nk, tt), lambda i, j, e: (i, 0, 0, j))
    return pl.pallas_call(
        functools.partial(_peer_ffn_kernel, alpha=alpha),
        out_shape=jax.ShapeDtypeStruct((b, s, d), F32),
        grid=(b, s // tt, ne // PEER_ET),
        in_specs=[pl.BlockSpec((1, tt, d), lambda i, j, e: (i, j, 0)),
                  pl.BlockSpec((1, d, tt), lambda i, j, e: (i, 0, j)),
                  rspec, rspec, rspec, rspec,
                  pl.BlockSpec((PEER_ET, d), lambda i, j, e: (e, 0)),
                  pl.BlockSpec((d, PEER_ET), lambda i, j, e: (0, e)),
                  pl.BlockSpec((1, d), lambda i, j, e: (0, 0)),
                  pl.BlockSpec((1, d), lambda i, j, e: (0, 0))],
        out_specs=pl.BlockSpec((1, tt, d), lambda i, j, e: (i, j, 0)),
        scratch_shapes=[pltpu.VMEM((d, tt), F32)],
        compiler_params=_cparams(("parallel", "parallel", "arbitrary")),
        name="peer_ffn",
    )(x1, x1t, thr, c1, s2, p2, u, vt, ln_g.reshape(1, d), ln_b.reshape(1, d))


def _tile(n, pref):
    t = min(n, pref)
    while n % t:
        t //= 2
    return t


def _layer(x, mem, rel_bias, w_in, q_norm_g, kv_norm_g, w_uq, w_uk, w_uv, w_idx_q,
           rwkv_mu, rwkv_w0, rwkv_w2, rwkv_a0, rwkv_a2, rwkv_g2, rwkv_k_k, rwkv_k_a, rwkv_r_k,
           rwkv_lnx_g, rwkv_lnx_b, w_mem_kv, w_br_dsa, w_br_rwkv, w_br_mem, w_out, ln1_g, ln1_b,
           w_peer_q, peer_keys, peer_u, peer_v, ln2_g, ln2_b, alpha):
    b, s, d = x.shape
    t = b * s
    tm = _tile(s, 512)
    x2 = x.reshape(t, d)
    n_dsa = DSA_Q_RANK + DSA_KV_RANK + IDX_DIM + IDX_HEADS
    n_rwkv = 3 * RWKV_WIDTH + DECAY_RANK + AAA_RANK + GATE_RANK
    n_mem = MEM_HEADS * MEM_HEAD_DIM
    o1, o2, o3 = n_dsa, n_dsa + n_rwkv, n_dsa + n_rwkv + n_mem
    wb = w_in.astype(BF16)
    dcols = _linear(x2, wb[:, :o1], F32, tm, n_dsa, name="in_dsa")
    rcols = _linear(x2, wb[:, o1:o2], F32, tm, n_rwkv // 2, name="in_rwkv")
    mem_q = _linear(x2, wb[:, o2:o3], BF16, tm, n_mem, name="in_memq")
    gates = _linear(x2, wb[:, o3:], BF16, tm, d, act="sigmoid", name="in_gates")

    q_lat, q_idx, w_idx, k_idx, c_kv = _dsa_prep(dcols.reshape(b, s, n_dsa), q_norm_g, kv_norm_g,
                                                 w_uq, w_uk, w_idx_q, tm)
    y_dsa = _dsa_attn(rel_bias, q_idx, w_idx, q_lat, k_idx, c_kv, w_uv)

    prep = _rwkv_prep(rcols, b, s, rwkv_mu, rwkv_w0, rwkv_w2, rwkv_a0, rwkv_a2, rwkv_g2,
                      rwkv_k_k, rwkv_k_a, rwkv_r_k, tm)
    y_rwkv = _rwkv_chunk(*prep, rwkv_lnx_g, rwkv_lnx_b)

    m = mem.shape[1]
    mkv = _linear(mem.reshape(b * m, d), w_mem_kv.astype(BF16), BF16, _tile(b * m, 512), 2 * n_mem,
                  name="mem_kv").reshape(b, m, 2 * n_mem)
    y_mem = _mem_attn(mem_q.reshape(b, s, n_mem), mkv, tm)

    x1, x1b, x1t = _merge(x, y_dsa, y_rwkv, y_mem, gates.reshape(b, s, 3 * d),
                          w_br_dsa, w_br_rwkv, w_br_mem, w_out, ln1_g, ln1_b, alpha, tm)

    tt = _tile(s, 512)
    thr, c1, s2, p2 = _peer_route(x1b, w_peer_q, peer_keys, tt)
    return _peer_ffn(x1, x1t, thr, c1, s2, p2, peer_u, peer_v, ln2_g, ln2_b, alpha, tt)


def kernel(x, mem, rel_bias, w_in, q_norm_g, kv_norm_g, w_uq, w_uk, w_uv, w_idx_q, rwkv_mu, rwkv_w0, rwkv_w2, rwkv_a0, rwkv_a2, rwkv_g2, rwkv_k_k, rwkv_k_a, rwkv_r_k, rwkv_lnx_g, rwkv_lnx_b, w_mem_kv, w_br_dsa, w_br_rwkv, w_br_mem, w_out, ln1_g, ln1_b, w_peer_q, peer_keys, peer_u, peer_v, ln2_g, ln2_b):
    depth = w_in.shape[0]
    alpha = (2 * depth) ** 0.25
    for l in range(depth):
        x = _layer(x, mem, rel_bias, w_in[l], q_norm_g[l], kv_norm_g[l], w_uq[l], w_uk[l], w_uv[l],
                   w_idx_q[l], rwkv_mu[l], rwkv_w0[l], rwkv_w2[l], rwkv_a0[l], rwkv_a2[l], rwkv_g2[l],
                   rwkv_k_k[l], rwkv_k_a[l], rwkv_r_k[l], rwkv_lnx_g[l], rwkv_lnx_b[l], w_mem_kv[l],
                   w_br_dsa[l], w_br_rwkv[l], w_br_mem[l], w_out[l], ln1_g[l], ln1_b[l],
                   w_peer_q[l], peer_keys[l], peer_u[l], peer_v[l], ln2_g[l], ln2_b[l], alpha)
    return x
```

```python
import functools
import math

import jax
import jax.numpy as jnp
from jax import lax
from jax.experimental import pallas as pl
from jax.experimental.pallas import tpu as pltpu

F32 = jnp.float32
BF16 = jnp.bfloat16
I32 = jnp.int32

DSA_HEADS = 8
DSA_HEAD_DIM = 64
DSA_Q_RANK = 256
DSA_KV_RANK = 128
IDX_HEADS = 8
IDX_DIM = 32
INDEX_TOPK_MAX = 256
NUM_BUCKETS = 32
MAX_DISTANCE = 128
RWKV_HEADS = 8
RWKV_HEAD_DIM = 64
RWKV_WIDTH = RWKV_HEADS * RWKV_HEAD_DIM
DECAY_RANK = 64
AAA_RANK = 64
GATE_RANK = 128
GN_EPS = 64e-5
MEM_HEADS = 4
MEM_HEAD_DIM = 128
PEER_HEADS = 8
PEER_N_KEYS = 128
PEER_HALF = 128
PEER_TOPK = 16
LN_EPS = 1e-5
RMS_EPS = 1e-6

LANES = 128
VMEM_LIMIT = 56 * 1024 * 1024

INT_MIN = -(2 ** 31)
NEG_INF = float("-inf")


def _cparams(sem, vmem=VMEM_LIMIT):
    return pltpu.CompilerParams(dimension_semantics=sem, vmem_limit_bytes=vmem)


def _dot(a, b):
    return jnp.dot(a, b, preferred_element_type=F32)


def _dot_nt(a, b):
    return lax.dot_general(a, b, (((1,), (1,)), ((), ())), preferred_element_type=F32)


def _split2(x):
    hi = x.astype(BF16)
    lo = (x - hi.astype(F32)).astype(BF16)
    return hi, lo


def _dot3(a, b, nt=False):
    d = _dot_nt if nt else _dot
    ah, al = _split2(a)
    bh, bl = _split2(b)
    return d(ah, bh) + (d(ah, bl) + d(al, bh))


def _linear_kernel(x_ref, w_ref, o_ref, *, act):
    y = _dot(x_ref[...].astype(BF16), w_ref[...])
    if act == "sigmoid":
        y = jax.nn.sigmoid(y)
    o_ref[...] = y.astype(o_ref.dtype)


def _linear(x, w, out_dtype, tm, tn, act=None, name="linear"):
    m, k = x.shape
    n = w.shape[1]
    return pl.pallas_call(
        functools.partial(_linear_kernel, act=act),
        out_shape=jax.ShapeDtypeStruct((m, n), out_dtype),
        grid=(m // tm, n // tn),
        in_specs=[pl.BlockSpec((tm, k), lambda i, j: (i, 0)),
                  pl.BlockSpec((k, tn), lambda i, j: (0, j))],
        out_specs=pl.BlockSpec((tm, tn), lambda i, j: (i, j)),
        compiler_params=_cparams(("parallel", "parallel")),
        name=name,
    )(x, w)


def _dsa_prep_kernel(dc_ref, qg_ref, kvg_ref, wuq_ref, wuk_ref, wiq_ref,
                     ql_ref, qi_ref, wi_ref, ki_ref, kv_ref):
    dc = dc_ref[0]
    c_q = dc[:, :DSA_Q_RANK]
    c_kv = dc[:, DSA_Q_RANK:DSA_Q_RANK + DSA_KV_RANK]
    off = DSA_Q_RANK + DSA_KV_RANK
    k_idx = dc[:, off:off + IDX_DIM]
    w_idx = dc[:, off + IDX_DIM:off + IDX_DIM + IDX_HEADS]
    c_q = c_q * lax.rsqrt(jnp.mean(c_q * c_q, axis=-1, keepdims=True) + RMS_EPS) * qg_ref[...]
    c_kv = c_kv * lax.rsqrt(jnp.mean(c_kv * c_kv, axis=-1, keepdims=True) + RMS_EPS) * kvg_ref[...]
    cqb = c_q.astype(BF16)
    scale = DSA_HEAD_DIM ** -0.5
    for h in range(DSA_HEADS):
        q_h = _dot(cqb, wuq_ref[h])
        ql_ref[0, h] = (_dot(q_h.astype(BF16), wuk_ref[h]) * scale).astype(BF16)
        qi_ref[0, h] = _dot(cqb, wiq_ref[h]).astype(BF16)
    wi_ref[0] = w_idx * (IDX_HEADS ** -0.5 * IDX_DIM ** -0.5)
    ki_ref[0] = k_idx.astype(BF16)
    kv_ref[0] = c_kv.astype(BF16)


def _dsa_prep(dcols, q_norm_g, kv_norm_g, w_uq, w_uk, w_idx_q, tm):
    b, s, wd = dcols.shape
    h = DSA_HEADS
    wuq_h = w_uq.reshape(DSA_Q_RANK, h, DSA_HEAD_DIM).transpose(1, 0, 2).astype(BF16)
    wuk_h = w_uk.transpose(1, 2, 0).astype(BF16)
    wiq_h = w_idx_q.reshape(DSA_Q_RANK, IDX_HEADS, IDX_DIM).transpose(1, 0, 2).astype(BF16)
    full = lambda shape: pl.BlockSpec(shape, lambda i, j: (0,) * len(shape))
    return pl.pallas_call(
        _dsa_prep_kernel,
        out_shape=(jax.ShapeDtypeStruct((b, h, s, DSA_KV_RANK), BF16),
                   jax.ShapeDtypeStruct((b, IDX_HEADS, s, IDX_DIM), BF16),
                   jax.ShapeDtypeStruct((b, s, IDX_HEADS), F32),
                   jax.ShapeDtypeStruct((b, s, IDX_DIM), BF16),
                   jax.ShapeDtypeStruct((b, s, DSA_KV_RANK), BF16)),
        grid=(b, s // tm),
        in_specs=[pl.BlockSpec((1, tm, wd), lambda i, j: (i, j, 0)),
                  full((1, DSA_Q_RANK)), full((1, DSA_KV_RANK)),
                  full(wuq_h.shape), full(wuk_h.shape), full(wiq_h.shape)],
        out_specs=(pl.BlockSpec((1, h, tm, DSA_KV_RANK), lambda i, j: (i, 0, j, 0)),
                   pl.BlockSpec((1, IDX_HEADS, tm, IDX_DIM), lambda i, j: (i, 0, j, 0)),
                   pl.BlockSpec((1, tm, IDX_HEADS), lambda i, j: (i, j, 0)),
                   pl.BlockSpec((1, tm, IDX_DIM), lambda i, j: (i, j, 0)),
                   pl.BlockSpec((1, tm, DSA_KV_RANK), lambda i, j: (i, j, 0))),
        compiler_params=_cparams(("parallel", "parallel")),
        name="dsa_prep",
    )(dcols, q_norm_g.reshape(1, -1), kv_norm_g.reshape(1, -1), wuq_h, wuk_h, wiq_h)


DSA_QB = 128
DSA_KB = 128
DSA_SC = 512


def _t5_bias_tile(rel_ref, h, off):
    row = lax.broadcasted_iota(I32, (DSA_QB, DSA_KB), 0)
    col = lax.broadcasted_iota(I32, (DSA_QB, DSA_KB), 1)
    n = jnp.maximum(row - col + off, 0)
    max_exact = NUM_BUCKETS // 2
    nf = jnp.maximum(n, 1).astype(F32)
    large = max_exact + (jnp.log(nf / max_exact) / math.log(MAX_DISTANCE / max_exact)
                         * (NUM_BUCKETS - max_exact)).astype(I32)
    large = jnp.minimum(large, NUM_BUCKETS - 1)
    bucket = jnp.where(n < max_exact, n, large)
    last = rel_ref[NUM_BUCKETS - 1, h]
    out = jnp.zeros((DSA_QB, DSA_KB), F32)
    for bk in range(NUM_BUCKETS - 1):
        out = jnp.where(bucket == bk, rel_ref[bk, h] - last, out)
    return out


def _fold_lanes(x, op):
    out = x[..., :LANES]
    for c in range(1, x.shape[-1] // LANES):
        out = op(out, x[..., c * LANES:(c + 1) * LANES])
    return out


def _dsa_attn_kernel(rel_ref, qi_ref, wi_ref, ql_ref, ki_ref, kv_ref, wuv_ref, o_ref,
                     key_sc, bias_sc, mx_sc, ls_sc, acc_sc, *, topk, seq):
    H = DSA_HEADS
    QB, KB, SC = DSA_QB, DSA_KB, DSA_SC
    qb = pl.program_id(1)
    q0 = qb * QB

    @pl.when((pl.program_id(0) == 0) & (qb == 0))
    def _():
        for h in range(H):
            bias_sc[0, h] = _t5_bias_tile(rel_ref, h, 0)
            bias_sc[1, h] = _t5_bias_tile(rel_ref, h, KB)

    n_sc = (q0 + QB + SC - 1) // SC
    wi = wi_ref[0]
    HALF = SC // 2
    row_g = q0 + lax.broadcasted_iota(I32, (QB, HALF), 0)
    col_h = lax.broadcasted_iota(I32, (QB, HALF), 1)

    def score_body(j, carry):
        for c in range(2):
            k0 = pl.multiple_of(j * SC + c * HALF, HALF)
            ks = ki_ref[0, pl.ds(k0, HALF), :]
            acc = jnp.zeros((QB, HALF), F32)
            for h in range(IDX_HEADS):
                lg = _dot_nt(qi_ref[0, h], ks)
                acc = acc + wi[:, h:h + 1] * jnp.maximum(lg, 0.0)
            acc = jnp.where(acc == 0.0, 0.0, acc)
            bits = pltpu.bitcast(acc, I32)
            key = bits ^ ((bits >> 31) & 0x7FFFFFFF)
            key_sc[:, pl.ds(k0, HALF)] = jnp.where(col_h + k0 <= row_g, key, INT_MIN)
        return carry

    lax.fori_loop(0, n_sc, score_body, 0)

    col_l = lax.broadcasted_iota(I32, (QB, LANES), 1)

    def count(ind):
        def body(j, acc):
            for c in range(SC // LANES):
                k0 = pl.multiple_of(j * SC + c * LANES, LANES)
                acc = acc + ind(key_sc[:, pl.ds(k0, LANES)], k0)
            return acc
        acc = lax.fori_loop(0, n_sc, body, jnp.zeros((QB, LANES), I32))
        return jnp.sum(acc, axis=1, keepdims=True)

    def bis_body(i, ru):
        bit = lax.shift_left(jnp.int32(1), 31 - i)
        cand = jnp.broadcast_to((ru | bit) ^ INT_MIN, (QB, LANES))
        cnt = count(lambda blk, k0: jnp.where(blk >= cand, 1, 0))
        return jnp.where(cnt >= topk, ru | bit, ru)

    ru = lax.fori_loop(0, 32, bis_body, jnp.zeros((QB, 1), I32))
    tau = jnp.maximum(ru ^ INT_MIN, INT_MIN + 1)
    tau_b = jnp.broadcast_to(tau, (QB, LANES))
    cnt_ge = count(lambda blk, k0: jnp.where(blk >= tau_b, 1, 0))

    @pl.when(jnp.max(cnt_ge) > topk)
    def _():
        cnt_gt = count(lambda blk, k0: jnp.where(blk > tau_b, 1, 0))
        need = topk - cnt_gt

        def pos_body(_, lohi):
            lo, hi = lohi
            mid = (lo + hi) >> 1
            mid_b = jnp.broadcast_to(mid, (QB, LANES))
            f = count(lambda blk, k0: jnp.where(blk == tau_b, jnp.where(col_l + k0 <= mid_b, 1, 0), 0))
            ok = f >= need
            return jnp.where(ok, lo, mid + 1), jnp.where(ok, mid, hi)

        lo, _hi = lax.fori_loop(0, max(1, (seq - 1).bit_length()), pos_body,
                                (jnp.zeros((QB, 1), I32), jnp.full((QB, 1), seq - 1, I32)))
        jstar = jnp.broadcast_to(lo, (QB, LANES))

        def fix_body(j, carry):
            for c in range(SC // LANES):
                k0 = pl.multiple_of(j * SC + c * LANES, LANES)
                blk = key_sc[:, pl.ds(k0, LANES)]
                dropped = jnp.where(col_l + k0 > jstar, INT_MIN, blk)
                key_sc[:, pl.ds(k0, LANES)] = jnp.where(blk == tau_b, dropped, blk)
            return carry

        lax.fori_loop(0, n_sc, fix_body, 0)

    ql = ql_ref[0].reshape(H * QB, DSA_KV_RANK)
    near0 = jnp.maximum(qb - 1, 0) * KB
    n_far = (near0 + SC - 1) // SC
    tau_s = jnp.broadcast_to(tau, (QB, SC))
    tau_k = jnp.broadcast_to(tau, (QB, KB))
    col_s = lax.broadcasted_iota(I32, (QB, SC), 1)

    def far_logits(j):
        k0 = pl.multiple_of(j * SC, SC)
        kvb = kv_ref[0, pl.ds(k0, SC), :]
        blk = key_sc[:, pl.ds(k0, SC)]
        madd = jnp.where(blk >= tau_s, jnp.where(col_s + k0 < near0, 0.0, NEG_INF), NEG_INF)
        return _dot_nt(ql, kvb).reshape(H, QB, SC) + madd[None], kvb

    def near_logits(jb, bias):
        k0 = pl.multiple_of(jb * KB, KB)
        kvb = kv_ref[0, pl.ds(k0, KB), :]
        madd = jnp.where(key_sc[:, pl.ds(k0, KB)] >= tau_k, 0.0, NEG_INF)
        return _dot_nt(ql, kvb).reshape(H, QB, KB) + (bias + madd[None]), kvb

    mx_sc[...] = jnp.full(mx_sc.shape, NEG_INF, F32)

    def far_max(j, carry):
        s, _ = far_logits(j)
        mx_sc[...] = jnp.maximum(mx_sc[...], _fold_lanes(s, jnp.maximum))
        return carry

    lax.fori_loop(0, n_far, far_max, 0)

    @pl.when(qb >= 1)
    def _():
        s, _ = near_logits(qb - 1, bias_sc[1])
        mx_sc[...] = jnp.maximum(mx_sc[...], s)

    s, _ = near_logits(qb, bias_sc[0])
    mx_sc[...] = jnp.maximum(mx_sc[...], s)
    mx_sc[...] = jnp.broadcast_to(jnp.max(mx_sc[...], axis=-1, keepdims=True), mx_sc.shape)

    ls_sc[...] = jnp.zeros(ls_sc.shape, F32)
    acc_sc[...] = jnp.zeros(acc_sc.shape, F32)

    def far_acc(j, carry):
        s, kvb = far_logits(j)
        m_b = mx_sc[...]
        p = jnp.concatenate([jnp.exp(s[..., c * LANES:(c + 1) * LANES] - m_b)
                             for c in range(SC // LANES)], axis=-1)
        ls_sc[...] += _fold_lanes(p, jnp.add)
        acc_sc[...] += _dot(p.reshape(H * QB, SC).astype(BF16), kvb).reshape(H, QB, DSA_KV_RANK)
        return carry

    lax.fori_loop(0, n_far, far_acc, 0)

    def near_acc(jb, bias):
        s, kvb = near_logits(jb, bias)
        p = jnp.exp(s - mx_sc[...])
        ls_sc[...] += p
        acc_sc[...] += _dot(p.reshape(H * QB, KB).astype(BF16), kvb).reshape(H, QB, DSA_KV_RANK)

    @pl.when(qb >= 1)
    def _():
        near_acc(qb - 1, bias_sc[1])

    near_acc(qb, bias_sc[0])

    o_lat = acc_sc[...] / jnp.sum(ls_sc[...], axis=-1, keepdims=True)
    y = jnp.zeros((QB, DSA_HEADS * DSA_HEAD_DIM), F32)
    for h in range(H):
        y = y + _dot(o_lat[h].astype(BF16), wuv_ref[h])
    o_ref[0] = y.astype(o_ref.dtype)


def _dsa_attn(rel_bias, q_idx, w_idx, q_lat, k_idx, c_kv, w_uv):
    b, h, s, _ = q_lat.shape
    topk = min(INDEX_TOPK_MAX, s // 4)
    width = DSA_HEADS * DSA_HEAD_DIM
    eye = jnp.eye(DSA_HEADS, dtype=F32)
    wuv_e = jnp.einsum("chd,hg->hcgd", w_uv, eye).reshape(DSA_HEADS, DSA_KV_RANK, width).astype(BF16)
    QB = DSA_QB
    assert s % DSA_SC == 0
    return pl.pallas_call(
        functools.partial(_dsa_attn_kernel, topk=topk, seq=s),
        out_shape=jax.ShapeDtypeStruct((b, s, width), BF16),
        grid=(b, s // QB),
        in_specs=[pl.BlockSpec(memory_space=pltpu.SMEM),
                  pl.BlockSpec((1, IDX_HEADS, QB, IDX_DIM), lambda i, j: (i, 0, j, 0)),
                  pl.BlockSpec((1, QB, IDX_HEADS), lambda i, j: (i, j, 0)),
                  pl.BlockSpec((1, h, QB, DSA_KV_RANK), lambda i, j: (i, 0, j, 0)),
                  pl.BlockSpec((1, s, IDX_DIM), lambda i, j: (i, 0, 0)),
                  pl.BlockSpec((1, s, DSA_KV_RANK), lambda i, j: (i, 0, 0)),
                  pl.BlockSpec(wuv_e.shape, lambda i, j: (0, 0, 0))],
        out_specs=pl.BlockSpec((1, QB, width), lambda i, j: (i, j, 0)),
        scratch_shapes=[pltpu.VMEM((QB, s), I32),
                        pltpu.VMEM((2, h, QB, DSA_KB), F32),
                        pltpu.VMEM((h, QB, LANES), F32),
                        pltpu.VMEM((h, QB, LANES), F32),
                        pltpu.VMEM((h, QB, DSA_KV_RANK), F32)],
        compiler_params=_cparams(("arbitrary", "arbitrary")),
        name="dsa_attn",
    )(rel_bias, q_idx, w_idx, q_lat, k_idx, c_kv, wuv_e)


def _rwkv_prep_kernel(c_ref, p_ref, mu_ref, w0_ref, a0_ref, kk_ref, ka_ref, rk_ref,
                      wwa_ref, g2_ref, ones_ref,
                      r_o, k_o, v_o, kk_o, b_o, lw_o, bonus_o, g_o, *, tiles_per_seq):
    W = RWKV_WIDTH
    cols = c_ref[...]
    first = (pl.program_id(0) % tiles_per_seq) == 0
    prev_row = jnp.where(first, 0.0, p_ref[7:8, :])
    rolled = pltpu.roll(cols, 1, 0)
    row = lax.broadcasted_iota(I32, cols.shape, 0)
    prev = jnp.where(row == 0, prev_row, rolled)
    xs = cols + (prev - cols) * mu_ref[...]
    r = xs[:, 0:W]
    k = xs[:, W:2 * W]
    v = xs[:, 2 * W:3 * W]
    lora = xs[:, 3 * W:3 * W + DECAY_RANK + AAA_RANK]
    gl = xs[:, 3 * W + DECAY_RANK + AAA_RANK:]
    lane = lax.broadcasted_iota(I32, lora.shape, 1)
    lora = jnp.where(lane < DECAY_RANK, jnp.tanh(lora), lora)
    wa = _dot(lora.astype(BF16), wwa_ref[...])
    w = -jax.nn.softplus(-(w0_ref[...] + wa[:, :W])) - 0.5
    a = jax.nn.sigmoid(a0_ref[...] + wa[:, W:])
    ones_bd = ones_ref[...]

    def head_sum(t):
        hi, lo = _split2(t)
        return _dot(hi, ones_bd) + _dot(lo, ones_bd)

    kk = k * kk_ref[...]
    kk = kk / jnp.maximum(jnp.sqrt(head_sum(kk * kk)), 1e-12)
    k2 = k * (1.0 + (a - 1.0) * ka_ref[...])
    r_o[...] = r
    k_o[...] = k2
    v_o[...] = v
    kk_o[...] = kk
    b_o[...] = kk * a
    lw_o[...] = -jnp.exp(w)
    bonus_o[...] = head_sum(r * k2 * rk_ref[...]) * v
    g_o[...] = _dot(jax.nn.sigmoid(gl).astype(BF16), g2_ref[...])


def _rwkv_prep(cols, s, mu, w0, w2, a0, a2, g2, k_k, k_a, r_k, tm):
    t, wc = cols.shape
    W = RWKV_WIDTH
    wwa = jnp.zeros((DECAY_RANK + AAA_RANK, 2 * W), F32)
    wwa = wwa.at[:DECAY_RANK, :W].set(w2).at[DECAY_RANK:, W:].set(a2).astype(BF16)
    head_id = jnp.arange(W) // RWKV_HEAD_DIM
    ones_bd = (head_id[:, None] == head_id[None, :]).astype(BF16)
    row = lambda a: a.reshape(1, -1)
    vec = lambda n: pl.BlockSpec((1, n), lambda i: (0, 0))
    tok = jax.ShapeDtypeStruct((t, W), F32)
    tok_spec = pl.BlockSpec((tm, W), lambda i: (i, 0))
    return pl.pallas_call(
        functools.partial(_rwkv_prep_kernel, tiles_per_seq=s // tm),
        out_shape=(tok,) * 8,
        grid=(t // tm,),
        in_specs=[pl.BlockSpec((tm, wc), lambda i: (i, 0)),
                  pl.BlockSpec((8, wc), lambda i: (jnp.maximum(i * (tm // 8) - 1, 0), 0)),
                  vec(wc), vec(W), vec(W), vec(W), vec(W), vec(W),
                  pl.BlockSpec(wwa.shape, lambda i: (0, 0)),
                  pl.BlockSpec((GATE_RANK, W), lambda i: (0, 0)),
                  pl.BlockSpec((W, W), lambda i: (0, 0))],
        out_specs=(tok_spec,) * 8,
        compiler_params=_cparams(("parallel",)),
        name="rwkv_prep",
    )(cols, cols, row(mu), row(w0), row(a0), row(k_k), row(k_a), row(r_k.reshape(-1)),
      wwa, g2.astype(BF16), ones_bd)


RWKV_CHUNK = 64
RWKV_GROUP = 8


def _rwkv_chunk_kernel(r_ref, k_ref, v_ref, kk_ref, b_ref, lw_ref, bonus_ref, g_ref,
                       lg_ref, lb_ref, o_ref, st_sc):
    C = RWKV_CHUNK
    N = RWKV_HEAD_DIM
    P = 2 * N
    nb = r_ref.shape[0]
    npair = RWKV_WIDTH // P

    @pl.when(pl.program_id(0) == 0)
    def _():
        st_sc[...] = jnp.zeros(st_sc.shape, F32)

    row_c = lax.broadcasted_iota(I32, (C, P), 0)
    lane_c = lax.broadcasted_iota(I32, (C, P), 1)
    tcol = lane_c & (N - 1)
    strict = tcol < row_c
    incl = tcol <= row_c
    lane_lo_n = lax.broadcasted_iota(I32, (N, P), 1) < N
    r2 = lax.broadcasted_iota(I32, (P, P), 0)
    l2 = lax.broadcasted_iota(I32, (P, P), 1)
    bdmask = (r2 < N) == (l2 < N)
    ones_bd = jnp.where(bdmask, 1.0, 0.0).astype(BF16)
    ones2 = jnp.concatenate([ones_bd, ones_bd], axis=0)
    ti = lax.broadcasted_iota(I32, (C, C), 0)
    si = lax.broadcasted_iota(I32, (C, C), 1)
    tri = jnp.where(si <= ti, 1.0, 0.0).astype(BF16)
    tri3 = jnp.concatenate([tri, tri, tri], axis=1)

    def bd(x):
        return jnp.where(bdmask, jnp.concatenate([x, x], axis=0), 0.0).astype(BF16)

    def head_mean(t):
        hi, lo = _split2(t)
        return _dot(jnp.concatenate([hi, lo], axis=1), ones2) * (1.0 / N)

    units = [(bi, p) for bi in range(nb) for p in range(npair)]
    nlev = max(1, (C - 1).bit_length())
    for g0 in range(0, len(units), RWKV_GROUP):
        grp = units[g0:g0 + RWKV_GROUP]
        n = len(grp)
        ld = lambda ref: [ref[bi, :, p * P:(p + 1) * P] for bi, p in grp]
        r, k, v, kk, b, lw = ld(r_ref), ld(k_ref), ld(v_ref), ld(kk_ref), ld(b_ref), ld(lw_ref)

        cum = []
        for u in range(n):
            hi = lw[u].astype(BF16)
            r1 = lw[u] - hi.astype(F32)
            mid = r1.astype(BF16)
            lo = (r1 - mid.astype(F32)).astype(BF16)
            cum.append(_dot(tri3, jnp.concatenate([hi, mid, lo], axis=0)))
        cum_last = [c[C - 1:C, :] for c in cum]
        p_inv = [jnp.exp(-c) for c in cum]
        a_t = [-kk[u] * jnp.exp(cum[u] - lw[u]) for u in range(n)]
        r_t = [r[u] * jnp.exp(cum[u]) for u in range(n)]
        dec = [jnp.exp(cum_last[u] - cum[u]) for u in range(n)]
        ar = [jnp.concatenate([a_t[u], r_t[u]], axis=0).astype(BF16) for u in range(n)]
        sb = [_dot_nt(ar[u], bd(b[u] * p_inv[u])) for u in range(n)]
        sk = [_dot_nt(ar[u], bd(k[u] * p_inv[u])) for u in range(n)]
        bd_v = [bd(v[u]) for u in range(n)]
        lp = [jnp.where(strict, sb[u][:C], 0.0) for u in range(n)]
        l_ak = [jnp.where(strict, sk[u][:C], 0.0).astype(BF16) for u in range(n)]
        m_rb = [jnp.where(incl, sb[u][C:], 0.0).astype(BF16) for u in range(n)]
        m_rk = [jnp.where(incl, sk[u][C:], 0.0).astype(BF16) for u in range(n)]

        xa = list(a_t)
        xu = [_dot(l_ak[u], bd_v[u]) for u in range(n)]
        for lev in range(nlev):
            last = lev == nlev - 1
            for u in range(n):
                parts = [bd(xa[u]), bd(xu[u])] + ([] if last else [bd(lp[u])])
                res = _dot(lp[u].astype(BF16), jnp.concatenate(parts, axis=1))
                xa[u] = xa[u] + res[:, :P]
                xu[u] = xu[u] + res[:, P:2 * P]
                if not last:
                    lp[u] = res[:, 2 * P:]

        res = [_dot(m_rb[u], jnp.concatenate([bd(xa[u]), bd(xu[u])], axis=1)) for u in range(n)]
        r_hat = [r_t[u] + res[u][:, :P] for u in range(n)]
        y0 = [res[u][:, P:] + _dot(m_rk[u], bd_v[u]) for u in range(n)]
        zt = [jnp.concatenate([b[u] * dec[u], k[u] * dec[u]], axis=0).T.astype(BF16) for u in range(n)]
        pct = [jnp.broadcast_to(cum_last[u], (P, P)).T for u in range(n)]
        pcm = [jnp.exp(jnp.where(lane_lo_n, pct[u][:N], pct[u][N:])) for u in range(n)]

        y = []
        for u, (bi, p) in enumerate(grp):
            st = st_sc[bi, p]
            ws = _dot(jnp.concatenate([xa[u], r_hat[u]], axis=0).astype(BF16), bd(st))
            wc = ws[:C] + xu[u]
            y.append(ws[C:] + y0[u])
            full = _dot(zt[u], jnp.concatenate([wc, v[u]], axis=0).astype(BF16))
            st_sc[bi, p] = pcm[u] * st + jnp.where(lane_lo_n, full[:N], full[N:])

        for u, (bi, p) in enumerate(grp):
            sl = slice(p * P, (p + 1) * P)
            yc = y[u] - head_mean(y[u])
            yn = yc * lax.rsqrt(head_mean(yc * yc) + GN_EPS) * lg_ref[:, sl] + lb_ref[:, sl]
            o_ref[bi, :, sl] = ((yn + bonus_ref[bi, :, sl]) * g_ref[bi, :, sl]).astype(o_ref.dtype)


def _rwkv_chunk(r, k, v, kk, bvec, lw, bonus, g, lnx_g, lnx_b):
    b, s, w = r.shape
    C = RWKV_CHUNK
    spec = pl.BlockSpec((b, C, w), lambda j: (0, j, 0))
    pspec = pl.BlockSpec((1, w), lambda j: (0, 0))
    return pl.pallas_call(
        _rwkv_chunk_kernel,
        out_shape=jax.ShapeDtypeStruct((b, s, w), BF16),
        grid=(s // C,),
        in_specs=[spec] * 8 + [pspec, pspec],
        out_specs=spec,
        scratch_shapes=[pltpu.VMEM((b, w // (2 * RWKV_HEAD_DIM), RWKV_HEAD_DIM, 2 * RWKV_HEAD_DIM), F32)],
        compiler_params=_cparams(("arbitrary",)),
        name="rwkv_chunk",
    )(r, k, v, kk, bvec, lw, bonus, g, lnx_g.reshape(1, w), lnx_b.reshape(1, w))


def _mem_attn_kernel(q_ref, kv_ref, o_ref):
    W = MEM_HEADS * MEM_HEAD_DIM
    scale = MEM_HEAD_DIM ** -0.5
    for h in range(MEM_HEADS):
        sl = slice(h * MEM_HEAD_DIM, (h + 1) * MEM_HEAD_DIM)
        q = q_ref[0, :, sl]
        k = kv_ref[0, :, sl]
        v = kv_ref[0, :, W + h * MEM_HEAD_DIM:W + (h + 1) * MEM_HEAD_DIM]
        s = _dot_nt(q, k) * scale
        s = s - jnp.max(s, axis=-1, keepdims=True)
        p = jnp.exp(s)
        p = p / jnp.sum(p, axis=-1, keepdims=True)
        o_ref[0, :, sl] = _dot(p.astype(BF16), v).astype(o_ref.dtype)


def _mem_attn(q, kv, tq):
    b, s, w = q.shape
    m = kv.shape[1]
    return pl.pallas_call(
        _mem_attn_kernel,
        out_shape=jax.ShapeDtypeStruct((b, s, w), BF16),
        grid=(b, s // tq),
        in_specs=[pl.BlockSpec((1, tq, w), lambda i, j: (i, j, 0)),
                  pl.BlockSpec((1, m, 2 * w), lambda i, j: (i, 0, 0))],
        out_specs=pl.BlockSpec((1, tq, w), lambda i, j: (i, j, 0)),
        compiler_params=_cparams(("parallel", "parallel")),
        name="mem_attn",
    )(q, kv)


def _layernorm(z, g, b):
    mu = jnp.mean(z, axis=-1, keepdims=True)
    zc = z - mu
    var = jnp.mean(zc * zc, axis=-1, keepdims=True)
    return zc * lax.rsqrt(var + LN_EPS) * g + b


def _merge_kernel(x_ref, yd_ref, yr_ref, ym_ref, gt_ref, wd_ref, wr_ref, wm_ref, wo_ref,
                  g_ref, b_ref, x1_ref, x1b_ref, x1t_ref, *, alpha):
    d = x_ref.shape[-1]
    br_d = _dot(yd_ref[0], wd_ref[...])
    br_r = _dot(yr_ref[0], wr_ref[...])
    br_m = _dot(ym_ref[0], wm_ref[...])
    gt = gt_ref[0]
    merged = (gt[:, 0:d].astype(F32) * br_d + gt[:, d:2 * d].astype(F32) * br_r
              + gt[:, 2 * d:3 * d].astype(F32) * br_m)
    z = alpha * x_ref[0] + _dot(merged.astype(BF16), wo_ref[...])
    x1 = _layernorm(z, g_ref[...], b_ref[...])
    x1_ref[0] = x1
    x1b_ref[0] = x1.astype(BF16)
    x1t_ref[0] = x1.T.astype(BF16)


def _merge(x, y_dsa, y_rwkv, y_mem, gates, w_br_dsa, w_br_rwkv, w_br_mem, w_out, ln_g, ln_b, alpha, tm):
    b, s, d = x.shape
    full2 = lambda a: pl.BlockSpec(a.shape, lambda i, j: (0,) * a.ndim)
    wd = w_br_dsa.astype(BF16)
    wr = w_br_rwkv.astype(BF16)
    wm = w_br_mem.astype(BF16)
    wo = w_out.astype(BF16)
    g2 = ln_g.reshape(1, d)
    b2 = ln_b.reshape(1, d)
    tok = lambda w: pl.BlockSpec((1, tm, w), lambda i, j: (i, j, 0))
    return pl.pallas_call(
        functools.partial(_merge_kernel, alpha=alpha),
        out_shape=(jax.ShapeDtypeStruct((b, s, d), F32),
                   jax.ShapeDtypeStruct((b, s, d), BF16),
                   jax.ShapeDtypeStruct((b, d, s), BF16)),
        grid=(b, s // tm),
        in_specs=[tok(d), tok(y_dsa.shape[-1]), tok(y_rwkv.shape[-1]), tok(y_mem.shape[-1]), tok(3 * d),
                  full2(wd), full2(wr), full2(wm), full2(wo), full2(g2), full2(b2)],
        out_specs=(tok(d), tok(d), pl.BlockSpec((1, d, tm), lambda i, j: (i, 0, j))),
        compiler_params=_cparams(("parallel", "parallel")),
        name="merge",
    )(x, y_dsa, y_rwkv, y_mem, gates, wd, wr, wm, wo, g2, b2)


PEER_RANKS = PEER_TOPK + 1


def _peer_route_kernel(x_ref, wq_ref, keys_ref, thr_o, c1_o, s2_o, p2_o, a1_sc, a2_sc):
    tt = x_ref.shape[1]
    R = PEER_RANKS
    q = _dot(x_ref[0], wq_ref[...])
    keys0 = keys_ref[0]
    keys1 = keys_ref[1]

    def top_ranks(sub, sc):
        x = sub
        for rnk in range(R):
            m = jnp.max(x, axis=0, keepdims=True)
            sc[rnk:rnk + 1, :] = m
            x = jnp.where(x == m, NEG_INF, x)

    for h in range(PEER_HEADS):
        q1 = q[:, (2 * h) * PEER_HALF:(2 * h + 1) * PEER_HALF]
        q2 = q[:, (2 * h + 1) * PEER_HALF:(2 * h + 2) * PEER_HALF]
        s1 = _dot3(keys0, q1, nt=True)
        s2 = _dot3(keys1, q2, nt=True)
        top_ranks(s1, a1_sc)
        top_ranks(s2, a2_sc)
        pieces = []
        for i in range(R):
            nj = R // (i + 1)
            pieces.append(a1_sc[i:i + 1, :] + a2_sc[0:nj, :])
        best = pieces[0][0:1, :]
        zsum = jnp.zeros((1, tt), F32)
        v_prev = best
        v_cur = best
        work = pieces
        for rnk in range(R):
            m = work[0].max(axis=0, keepdims=True)
            for pc in work[1:]:
                m = jnp.maximum(m, pc.max(axis=0, keepdims=True))
            v_prev, v_cur = v_cur, m
            if rnk < PEER_TOPK:
                zsum = zsum + jnp.exp(m - best)
            work = [jnp.where(pc == m, NEG_INF, pc) for pc in work]
        thr = 0.5 * (v_prev + v_cur)
        m1 = a1_sc[0:1, :]
        m2 = a2_sc[0:1, :]
        thr_o[0, h] = thr - s1
        c1_o[0, h] = jnp.exp(s1 - m1) / zsum
        s2_o[0, h] = s2
        p2_o[0, h] = jnp.exp(s2 - m2)


def _peer_route(x1b, w_peer_q, peer_keys, tt):
    b, s, d = x1b.shape
    nk = PEER_N_KEYS
    wq = w_peer_q.astype(BF16)
    out = jax.ShapeDtypeStruct((b, PEER_HEADS, nk, s), F32)
    ospec = pl.BlockSpec((1, PEER_HEADS, nk, tt), lambda i, j: (i, 0, 0, j))
    return pl.pallas_call(
        _peer_route_kernel,
        out_shape=(out,) * 4,
        grid=(b, s // tt),
        in_specs=[pl.BlockSpec((1, tt, d), lambda i, j: (i, j, 0)),
                  pl.BlockSpec(wq.shape, lambda i, j: (0, 0)),
                  pl.BlockSpec(peer_keys.shape, lambda i, j: (0, 0, 0))],
        out_specs=(ospec,) * 4,
        scratch_shapes=[pltpu.VMEM((24, tt), F32), pltpu.VMEM((24, tt), F32)],
        compiler_params=_cparams(("parallel", "parallel")),
        name="peer_route",
    )(x1b, wq, peer_keys)


PEER_ET = 512


def _peer_ffn_kernel(x1_ref, x1t_ref, thr_ref, c1_ref, s2_ref, p2_ref, u_ref, vt_ref,
                     g_ref, b_ref, o_ref, acc_sc, *, alpha):
    e = pl.program_id(2)
    nk = PEER_N_KEYS
    tt = x1t_ref.shape[2]

    @pl.when(e == 0)
    def _():
        acc_sc[...] = jnp.zeros(acc_sc.shape, F32)

    act = _dot(u_ref[...], x1t_ref[0])
    act = 0.5 * act * (1.0 + lax.erf(act * (2.0 ** -0.5)))
    rows = PEER_ET // nk
    gate_rows = []
    for rr in range(rows):
        i1 = e * rows + rr
        gsum = jnp.zeros((nk, tt), F32)
        for h in range(PEER_HEADS):
            thr = thr_ref[0, h, pl.ds(i1, 1), :]
            c1 = c1_ref[0, h, pl.ds(i1, 1), :]
            sel = s2_ref[0, h] >= thr
            gsum = gsum + jnp.where(sel, p2_ref[0, h], 0.0) * c1
        gate_rows.append(gsum)
    gate = jnp.concatenate(gate_rows, axis=0)
    w = (gate * act).astype(BF16)
    acc_sc[...] += _dot(vt_ref[...], w)

    @pl.when(e == pl.num_programs(2) - 1)
    def _():
        z = alpha * x1_ref[0] + acc_sc[...].T
        o_ref[0] = _layernorm(z, g_ref[...], b_ref[...])


def _peer_ffn(x1, x1t, thr, c1, s2, p2, peer_u, peer_v, ln_g, ln_b, alpha, tt):
    b, s, d = x1.shape
    ne = peer_u.shape[0]
    nk = PEER_N_KEYS
    u = peer_u.astype(BF16)
    vt = peer_v.T.astype(BF16)
    rspec = pl.BlockSpec((1, PEER_HEADS, nk, tt), lambda i, j, e: (i, 0, 0, j))
    return pl.pallas_call(
        functools.partial(_peer_ffn_kernel, alpha=alpha),
        out_shape=jax.ShapeDtypeStruct((b, s, d), F32),
        grid=(b, s // tt, ne // PEER_ET),
        in_specs=[pl.BlockSpec((1, tt, d), lambda i, j, e: (i, j, 0)),
                  pl.BlockSpec((1, d, tt), lambda i, j, e: (i, 0, j)),
                  rspec, rspec, rspec, rspec,
                  pl.BlockSpec((PEER_ET, d), lambda i, j, e: (e, 0)),
                  pl.BlockSpec((d, PEER_ET), lambda i, j, e: (0, e)),
                  pl.BlockSpec((1, d), lambda i, j, e: (0, 0)),
                  pl.BlockSpec((1, d), lambda i, j, e: (0, 0))],
        out_specs=pl.BlockSpec((1, tt, d), lambda i, j, e: (i, j, 0)),
        scratch_shapes=[pltpu.VMEM((d, tt), F32)],
        compiler_params=_cparams(("parallel", "parallel", "arbitrary")),
        name="peer_ffn",
    )(x1, x1t, thr, c1, s2, p2, u, vt, ln_g.reshape(1, d), ln_b.reshape(1, d))


def _tile(n, pref):
    t = min(n, pref)
    while n % t:
        t //= 2
    return t


def _layer(x, mem, rel_bias, w_in, q_norm_g, kv_norm_g, w_uq, w_uk, w_uv, w_idx_q,
           rwkv_mu, rwkv_w0, rwkv_w2, rwkv_a0, rwkv_a2, rwkv_g2, rwkv_k_k, rwkv_k_a, rwkv_r_k,
           rwkv_lnx_g, rwkv_lnx_b, w_mem_kv, w_br_dsa, w_br_rwkv, w_br_mem, w_out, ln1_g, ln1_b,
           w_peer_q, peer_keys, peer_u, peer_v, ln2_g, ln2_b, alpha):
    b, s, d = x.shape
    t = b * s
    tm = _tile(s, 512)
    x2 = x.reshape(t, d)
    n_dsa = DSA_Q_RANK + DSA_KV_RANK + IDX_DIM + IDX_HEADS
    n_rwkv = 3 * RWKV_WIDTH + DECAY_RANK + AAA_RANK + GATE_RANK
    n_mem = MEM_HEADS * MEM_HEAD_DIM
    o1, o2, o3 = n_dsa, n_dsa + n_rwkv, n_dsa + n_rwkv + n_mem
    wb = w_in.astype(BF16)
    dcols = _linear(x2, wb[:, :o1], F32, tm, n_dsa, name="in_dsa")
    rcols = _linear(x2, wb[:, o1:o2], F32, tm, n_rwkv // 2, name="in_rwkv")
    mem_q = _linear(x2, wb[:, o2:o3], BF16, tm, n_mem, name="in_memq")
    gates = _linear(x2, wb[:, o3:], BF16, tm, d, act="sigmoid", name="in_gates")

    q_lat, q_idx, w_idx, k_idx, c_kv = _dsa_prep(dcols.reshape(b, s, n_dsa), q_norm_g, kv_norm_g,
                                                 w_uq, w_uk, w_idx_q, tm)
    y_dsa = _dsa_attn(rel_bias, q_idx, w_idx, q_lat, k_idx, c_kv, w_uv)

    prep = _rwkv_prep(rcols, s, rwkv_mu, rwkv_w0, rwkv_w2, rwkv_a0, rwkv_a2, rwkv_g2,
                      rwkv_k_k, rwkv_k_a, rwkv_r_k, tm)
    y_rwkv = _rwkv_chunk(*[a.reshape(b, s, RWKV_WIDTH) for a in prep], rwkv_lnx_g, rwkv_lnx_b)

    m = mem.shape[1]
    mkv = _linear(mem.reshape(b * m, d), w_mem_kv.astype(BF16), BF16, _tile(b * m, 512), 2 * n_mem,
                  name="mem_kv").reshape(b, m, 2 * n_mem)
    y_mem = _mem_attn(mem_q.reshape(b, s, n_mem), mkv, tm)

    x1, x1b, x1t = _merge(x, y_dsa, y_rwkv, y_mem, gates.reshape(b, s, 3 * d),
                          w_br_dsa, w_br_rwkv, w_br_mem, w_out, ln1_g, ln1_b, alpha, tm)

    tt = _tile(s, 512)
    thr, c1, s2, p2 = _peer_route(x1b, w_peer_q, peer_keys, tt)
    return _peer_ffn(x1, x1t, thr, c1, s2, p2, peer_u, peer_v, ln2_g, ln2_b, alpha, tt)


def kernel(x, mem, rel_bias, w_in, q_norm_g, kv_norm_g, w_uq, w_uk, w_uv, w_idx_q, rwkv_mu, rwkv_w0, rwkv_w2, rwkv_a0, rwkv_a2, rwkv_g2, rwkv_k_k, rwkv_k_a, rwkv_r_k, rwkv_lnx_g, rwkv_lnx_b, w_mem_kv, w_br_dsa, w_br_rwkv, w_br_mem, w_out, ln1_g, ln1_b, w_peer_q, peer_keys, peer_u, peer_v, ln2_g, ln2_b):
    depth = w_in.shape[0]
    alpha = (2 * depth) ** 0.25
    for l in range(depth):
        x = _layer(x, mem, rel_bias, w_in[l], q_norm_g[l], kv_norm_g[l], w_uq[l], w_uk[l], w_uv[l],
                   w_idx_q[l], rwkv_mu[l], rwkv_w0[l], rwkv_w2[l], rwkv_a0[l], rwkv_a2[l], rwkv_g2[l],
                   rwkv_k_k[l], rwkv_k_a[l], rwkv_r_k[l], rwkv_lnx_g[l], rwkv_lnx_b[l], w_mem_kv[l],
                   w_br_dsa[l], w_br_rwkv[l], w_br_mem[l], w_out[l], ln1_g[l], ln1_b[l],
                   w_peer_q[l], peer_keys[l], peer_u[l], peer_v[l], ln2_g[l], ln2_b[l], alpha)
    return x
```

```python
import functools
import math

import jax
import jax.numpy as jnp
from jax import lax
from jax.experimental import pallas as pl
from jax.experimental.pallas import tpu as pltpu

F32 = jnp.float32
BF16 = jnp.bfloat16
I32 = jnp.int32

DSA_HEADS = 8
DSA_HEAD_DIM = 64
DSA_Q_RANK = 256
DSA_KV_RANK = 128
IDX_HEADS = 8
IDX_DIM = 32
INDEX_TOPK_MAX = 256
NUM_BUCKETS = 32
MAX_DISTANCE = 128
RWKV_HEADS = 8
RWKV_HEAD_DIM = 64
RWKV_WIDTH = RWKV_HEADS * RWKV_HEAD_DIM
DECAY_RANK = 64
AAA_RANK = 64
GATE_RANK = 128
GN_EPS = 64e-5
MEM_HEADS = 4
MEM_HEAD_DIM = 128
PEER_HEADS = 8
PEER_N_KEYS = 128
PEER_HALF = 128
PEER_TOPK = 16
LN_EPS = 1e-5
RMS_EPS = 1e-6

LANES = 128
VMEM_LIMIT = 56 * 1024 * 1024

INT_MIN = -(2 ** 31)
NEG_INF = float("-inf")


def _cparams(sem, vmem=VMEM_LIMIT):
    return pltpu.CompilerParams(dimension_semantics=sem, vmem_limit_bytes=vmem)


def _dot(a, b):
    return jnp.dot(a, b, preferred_element_type=F32)


def _dot_nt(a, b):
    return lax.dot_general(a, b, (((1,), (1,)), ((), ())), preferred_element_type=F32)


def _split2(x):
    hi = x.astype(BF16)
    lo = (x - hi.astype(F32)).astype(BF16)
    return hi, lo


def _dot3(a, b, nt=False):
    d = _dot_nt if nt else _dot
    ah, al = _split2(a)
    bh, bl = _split2(b)
    return d(ah, bh) + (d(ah, bl) + d(al, bh))


def _linear_kernel(x_ref, w_ref, o_ref, *, act):
    y = _dot(x_ref[...].astype(BF16), w_ref[...])
    if act == "sigmoid":
        y = jax.nn.sigmoid(y)
    o_ref[...] = y.astype(o_ref.dtype)


def _linear(x, w, out_dtype, tm, tn, act=None, name="linear"):
    m, k = x.shape
    n = w.shape[1]
    return pl.pallas_call(
        functools.partial(_linear_kernel, act=act),
        out_shape=jax.ShapeDtypeStruct((m, n), out_dtype),
        grid=(m // tm, n // tn),
        in_specs=[pl.BlockSpec((tm, k), lambda i, j: (i, 0)),
                  pl.BlockSpec((k, tn), lambda i, j: (0, j))],
        out_specs=pl.BlockSpec((tm, tn), lambda i, j: (i, j)),
        compiler_params=_cparams(("parallel", "parallel")),
        name=name,
    )(x, w)


def _dsa_prep_kernel(dc_ref, qg_ref, kvg_ref, wuq_ref, wuk_ref, wiq_ref,
                     ql_ref, qi_ref, wi_ref, ki_ref, kv_ref):
    dc = dc_ref[0]
    c_q = dc[:, :DSA_Q_RANK]
    c_kv = dc[:, DSA_Q_RANK:DSA_Q_RANK + DSA_KV_RANK]
    off = DSA_Q_RANK + DSA_KV_RANK
    k_idx = dc[:, off:off + IDX_DIM]
    w_idx = dc[:, off + IDX_DIM:off + IDX_DIM + IDX_HEADS]
    c_q = c_q * lax.rsqrt(jnp.mean(c_q * c_q, axis=-1, keepdims=True) + RMS_EPS) * qg_ref[...]
    c_kv = c_kv * lax.rsqrt(jnp.mean(c_kv * c_kv, axis=-1, keepdims=True) + RMS_EPS) * kvg_ref[...]
    cqb = c_q.astype(BF16)
    scale = DSA_HEAD_DIM ** -0.5
    for h in range(DSA_HEADS):
        q_h = _dot(cqb, wuq_ref[h])
        ql_ref[0, h] = (_dot(q_h.astype(BF16), wuk_ref[h]) * scale).astype(BF16)
        qi_ref[0, h] = _dot(cqb, wiq_ref[h]).astype(BF16)
    wi_ref[0] = w_idx * (IDX_HEADS ** -0.5 * IDX_DIM ** -0.5)
    ki_ref[0] = k_idx.astype(BF16)
    kv_ref[0] = c_kv.astype(BF16)


def _dsa_prep(dcols, q_norm_g, kv_norm_g, w_uq, w_uk, w_idx_q, tm):
    b, s, wd = dcols.shape
    h = DSA_HEADS
    wuq_h = w_uq.reshape(DSA_Q_RANK, h, DSA_HEAD_DIM).transpose(1, 0, 2).astype(BF16)
    wuk_h = w_uk.transpose(1, 2, 0).astype(BF16)
    wiq_h = w_idx_q.reshape(DSA_Q_RANK, IDX_HEADS, IDX_DIM).transpose(1, 0, 2).astype(BF16)
    full = lambda shape: pl.BlockSpec(shape, lambda i, j: (0,) * len(shape))
    return pl.pallas_call(
        _dsa_prep_kernel,
        out_shape=(jax.ShapeDtypeStruct((b, h, s, DSA_KV_RANK), BF16),
                   jax.ShapeDtypeStruct((b, IDX_HEADS, s, IDX_DIM), BF16),
                   jax.ShapeDtypeStruct((b, s, IDX_HEADS), F32),
                   jax.ShapeDtypeStruct((b, s, IDX_DIM), BF16),
                   jax.ShapeDtypeStruct((b, s, DSA_KV_RANK), BF16)),
        grid=(b, s // tm),
        in_specs=[pl.BlockSpec((1, tm, wd), lambda i, j: (i, j, 0)),
                  full((1, DSA_Q_RANK)), full((1, DSA_KV_RANK)),
                  full(wuq_h.shape), full(wuk_h.shape), full(wiq_h.shape)],
        out_specs=(pl.BlockSpec((1, h, tm, DSA_KV_RANK), lambda i, j: (i, 0, j, 0)),
                   pl.BlockSpec((1, IDX_HEADS, tm, IDX_DIM), lambda i, j: (i, 0, j, 0)),
                   pl.BlockSpec((1, tm, IDX_HEADS), lambda i, j: (i, j, 0)),
                   pl.BlockSpec((1, tm, IDX_DIM), lambda i, j: (i, j, 0)),
                   pl.BlockSpec((1, tm, DSA_KV_RANK), lambda i, j: (i, j, 0))),
        compiler_params=_cparams(("parallel", "parallel")),
        name="dsa_prep",
    )(dcols, q_norm_g.reshape(1, -1), kv_norm_g.reshape(1, -1), wuq_h, wuk_h, wiq_h)


DSA_QB = 128
DSA_KB = 128
DSA_SC = 512


def _t5_bias_tile(rel_ref, h, off):
    row = lax.broadcasted_iota(I32, (DSA_QB, DSA_KB), 0)
    col = lax.broadcasted_iota(I32, (DSA_QB, DSA_KB), 1)
    n = jnp.maximum(row - col + off, 0)
    max_exact = NUM_BUCKETS // 2
    nf = jnp.maximum(n, max_exact).astype(F32)
    large = max_exact + jnp.floor(jnp.log(nf / max_exact) / math.log(MAX_DISTANCE / max_exact)
                                  * (NUM_BUCKETS - max_exact)).astype(I32)
    large = jnp.minimum(large, NUM_BUCKETS - 1)
    bucket = jnp.where(n < max_exact, n, large)
    last = rel_ref[NUM_BUCKETS - 1, h]
    out = jnp.zeros((DSA_QB, DSA_KB), F32)
    for bk in range(NUM_BUCKETS - 1):
        out = jnp.where(bucket == bk, rel_ref[bk, h] - last, out)
    return out


def _fold_lanes(x, op):
    out = x[..., :LANES]
    for c in range(1, x.shape[-1] // LANES):
        out = op(out, x[..., c * LANES:(c + 1) * LANES])
    return out


def _dsa_attn_kernel(rel_ref, qi_ref, wi_ref, ql_ref, ki_ref, kv_ref, wuv_ref, o_ref,
                     key_sc, bias_sc, mx_sc, ls_sc, acc_sc, *, topk, seq):
    H = DSA_HEADS
    QB, KB, SC = DSA_QB, DSA_KB, DSA_SC
    qb = pl.program_id(1)
    q0 = qb * QB

    @pl.when((pl.program_id(0) == 0) & (qb == 0))
    def _():
        for h in range(H):
            bias_sc[0, h] = _t5_bias_tile(rel_ref, h, 0)
            bias_sc[1, h] = _t5_bias_tile(rel_ref, h, KB)

    n_sc = (q0 + QB + SC - 1) // SC
    wi = wi_ref[0]
    HALF = SC // 2
    row_g = q0 + lax.broadcasted_iota(I32, (QB, HALF), 0)
    col_h = lax.broadcasted_iota(I32, (QB, HALF), 1)

    def score_body(j, carry):
        for c in range(2):
            k0 = pl.multiple_of(j * SC + c * HALF, HALF)
            ks = ki_ref[0, pl.ds(k0, HALF), :]
            acc = jnp.zeros((QB, HALF), F32)
            for h in range(IDX_HEADS):
                lg = _dot_nt(qi_ref[0, h], ks)
                acc = acc + wi[:, h:h + 1] * jnp.maximum(lg, 0.0)
            key_sc[:, pl.ds(k0, HALF)] = jnp.where(col_h + k0 <= row_g, acc, NEG_INF)
        return carry

    lax.fori_loop(0, n_sc, score_body, 0)

    col_l = lax.broadcasted_iota(I32, (QB, LANES), 1)

    def count(ind):
        def body(j, acc):
            for c in range(SC // LANES):
                k0 = pl.multiple_of(j * SC + c * LANES, LANES)
                acc = acc + ind(key_sc[:, pl.ds(k0, LANES)], k0)
            return acc
        acc = lax.fori_loop(0, n_sc, body, jnp.zeros((QB, LANES), I32))
        return jnp.sum(acc, axis=1, keepdims=True)

    def code_to_float(ru):
        c = ru ^ INT_MIN
        return pltpu.bitcast(c ^ ((c >> 31) & 0x7FFFFFFF), jnp.float32)

    def bis_body(i, ru):
        bit = lax.shift_left(jnp.int32(1), (31 - i).astype(I32))
        cand = jnp.broadcast_to(code_to_float(ru | bit), (QB, LANES))
        cnt = count(lambda blk, k0: jnp.where(blk >= cand, 1, 0))
        return jnp.where(cnt >= topk, ru | bit, ru)

    ru = lax.fori_loop(0, 32, bis_body, jnp.zeros((QB, 1), I32))
    few = q0 + lax.broadcasted_iota(I32, (QB, 1), 0) < topk
    tau = jnp.where(few, jnp.finfo(jnp.float32).min, code_to_float(ru))
    tau_b = jnp.broadcast_to(tau, (QB, LANES))
    cnt_ge = count(lambda blk, k0: jnp.where(blk >= tau_b, 1, 0))

    @pl.when(jnp.max(cnt_ge) > topk)
    def _():
        cnt_gt = count(lambda blk, k0: jnp.where(blk > tau_b, 1, 0))
        need = topk - cnt_gt

        def pos_body(_, lohi):
            lo, hi = lohi
            mid = (lo + hi) >> 1
            mid_b = jnp.broadcast_to(mid, (QB, LANES))
            f = count(lambda blk, k0: jnp.where(blk == tau_b, jnp.where(col_l + k0 <= mid_b, 1, 0), 0))
            ok = f >= need
            return jnp.where(ok, lo, mid + 1), jnp.where(ok, mid, hi)

        lo, _hi = lax.fori_loop(0, max(1, (seq - 1).bit_length()), pos_body,
                                (jnp.zeros((QB, 1), I32), jnp.full((QB, 1), seq - 1, I32)))
        jstar = jnp.broadcast_to(lo, (QB, LANES))

        def fix_body(j, carry):
            for c in range(SC // LANES):
                k0 = pl.multiple_of(j * SC + c * LANES, LANES)
                blk = key_sc[:, pl.ds(k0, LANES)]
                dropped = jnp.where(col_l + k0 > jstar, NEG_INF, blk)
                key_sc[:, pl.ds(k0, LANES)] = jnp.where(blk == tau_b, dropped, blk)
            return carry

        lax.fori_loop(0, n_sc, fix_body, 0)

    ql = ql_ref[0].reshape(H * QB, DSA_KV_RANK)
    near0 = jnp.maximum(qb - 1, 0) * KB
    n_far = (near0 + SC - 1) // SC
    tau_s = jnp.broadcast_to(tau, (QB, SC))
    tau_k = jnp.broadcast_to(tau, (QB, KB))
    col_s = lax.broadcasted_iota(I32, (QB, SC), 1)

    def far_logits(j):
        k0 = pl.multiple_of(j * SC, SC)
        kvb = kv_ref[0, pl.ds(k0, SC), :]
        blk = key_sc[:, pl.ds(k0, SC)]
        madd = jnp.where(blk >= tau_s, jnp.where(col_s + k0 < near0, 0.0, NEG_INF), NEG_INF)
        return _dot_nt(ql, kvb).reshape(H, QB, SC) + madd[None], kvb

    def near_logits(jb, bias):
        k0 = pl.multiple_of(jb * KB, KB)
        kvb = kv_ref[0, pl.ds(k0, KB), :]
        madd = jnp.where(key_sc[:, pl.ds(k0, KB)] >= tau_k, 0.0, NEG_INF)
        return _dot_nt(ql, kvb).reshape(H, QB, KB) + (bias + madd[None]), kvb

    mx_sc[...] = jnp.full(mx_sc.shape, NEG_INF, F32)

    def far_max(j, carry):
        s, _ = far_logits(j)
        mx_sc[...] = jnp.maximum(mx_sc[...], _fold_lanes(s, jnp.maximum))
        return carry

    lax.fori_loop(0, n_far, far_max, 0)

    @pl.when(qb >= 1)
    def _():
        s, _ = near_logits(qb - 1, bias_sc[1])
        mx_sc[...] = jnp.maximum(mx_sc[...], s)

    s, _ = near_logits(qb, bias_sc[0])
    mx_sc[...] = jnp.maximum(mx_sc[...], s)
    mx_sc[...] = jnp.broadcast_to(jnp.max(mx_sc[...], axis=-1, keepdims=True), mx_sc.shape)

    ls_sc[...] = jnp.zeros(ls_sc.shape, F32)
    acc_sc[...] = jnp.zeros(acc_sc.shape, F32)

    def far_acc(j, carry):
        s, kvb = far_logits(j)
        m_b = mx_sc[...]
        p = jnp.concatenate([jnp.exp(s[..., c * LANES:(c + 1) * LANES] - m_b)
                             for c in range(SC // LANES)], axis=-1)
        ls_sc[...] += _fold_lanes(p, jnp.add)
        acc_sc[...] += _dot(p.reshape(H * QB, SC).astype(BF16), kvb).reshape(H, QB, DSA_KV_RANK)
        return carry

    lax.fori_loop(0, n_far, far_acc, 0)

    def near_acc(jb, bias):
        s, kvb = near_logits(jb, bias)
        p = jnp.exp(s - mx_sc[...])
        ls_sc[...] += p
        acc_sc[...] += _dot(p.reshape(H * QB, KB).astype(BF16), kvb).reshape(H, QB, DSA_KV_RANK)

    @pl.when(qb >= 1)
    def _():
        near_acc(qb - 1, bias_sc[1])

    near_acc(qb, bias_sc[0])

    o_lat = acc_sc[...] / jnp.sum(ls_sc[...], axis=-1, keepdims=True)
    y = jnp.zeros((QB, DSA_HEADS * DSA_HEAD_DIM), F32)
    for h in range(H):
        y = y + _dot(o_lat[h].astype(BF16), wuv_ref[h])
    o_ref[0] = y.astype(o_ref.dtype)


def _dsa_attn(rel_bias, q_idx, w_idx, q_lat, k_idx, c_kv, w_uv):
    b, h, s, _ = q_lat.shape
    topk = min(INDEX_TOPK_MAX, s // 4)
    width = DSA_HEADS * DSA_HEAD_DIM
    eye = jnp.eye(DSA_HEADS, dtype=F32)
    wuv_e = jnp.einsum("chd,hg->hcgd", w_uv, eye).reshape(DSA_HEADS, DSA_KV_RANK, width).astype(BF16)
    QB = DSA_QB
    assert s % DSA_SC == 0
    return pl.pallas_call(
        functools.partial(_dsa_attn_kernel, topk=topk, seq=s),
        out_shape=jax.ShapeDtypeStruct((b, s, width), BF16),
        grid=(b, s // QB),
        in_specs=[pl.BlockSpec(memory_space=pltpu.SMEM),
                  pl.BlockSpec((1, IDX_HEADS, QB, IDX_DIM), lambda i, j: (i, 0, j, 0)),
                  pl.BlockSpec((1, QB, IDX_HEADS), lambda i, j: (i, j, 0)),
                  pl.BlockSpec((1, h, QB, DSA_KV_RANK), lambda i, j: (i, 0, j, 0)),
                  pl.BlockSpec((1, s, IDX_DIM), lambda i, j: (i, 0, 0)),
                  pl.BlockSpec((1, s, DSA_KV_RANK), lambda i, j: (i, 0, 0)),
                  pl.BlockSpec(wuv_e.shape, lambda i, j: (0, 0, 0))],
        out_specs=pl.BlockSpec((1, QB, width), lambda i, j: (i, j, 0)),
        scratch_shapes=[pltpu.VMEM((QB, s), F32),
                        pltpu.VMEM((2, h, QB, DSA_KB), F32),
                        pltpu.VMEM((h, QB, LANES), F32),
                        pltpu.VMEM((h, QB, LANES), F32),
                        pltpu.VMEM((h, QB, DSA_KV_RANK), F32)],
        compiler_params=_cparams(("arbitrary", "arbitrary")),
        name="dsa_attn",
    )(rel_bias, q_idx, w_idx, q_lat, k_idx, c_kv, wuv_e)


def _rwkv_prep_kernel(c_ref, p_ref, mu_ref, w0_ref, a0_ref, kk_ref, ka_ref, rk_ref,
                      wwa_ref, g2_ref, ones_ref,
                      r_o, k_o, v_o, kk_o, b_o, lw_o, bonus_o, g_o, *, tiles_per_seq):
    W = RWKV_WIDTH
    cols = c_ref[...]
    first = (pl.program_id(0) % tiles_per_seq) == 0
    prev_row = jnp.where(first, 0.0, p_ref[7:8, :])
    rolled = pltpu.roll(cols, 1, 0)
    row = lax.broadcasted_iota(I32, cols.shape, 0)
    prev = jnp.where(row == 0, prev_row, rolled)
    xs = cols + (prev - cols) * mu_ref[...]
    r = xs[:, 0:W]
    k = xs[:, W:2 * W]
    v = xs[:, 2 * W:3 * W]
    lora = xs[:, 3 * W:3 * W + DECAY_RANK + AAA_RANK]
    gl = xs[:, 3 * W + DECAY_RANK + AAA_RANK:]
    lane = lax.broadcasted_iota(I32, lora.shape, 1)
    lora = jnp.where(lane < DECAY_RANK, jnp.tanh(lora), lora)
    wa = _dot(lora.astype(BF16), wwa_ref[...])
    w = -jax.nn.softplus(-(w0_ref[...] + wa[:, :W])) - 0.5
    a = jax.nn.sigmoid(a0_ref[...] + wa[:, W:])
    ones_bd = ones_ref[...]

    def head_sum(t):
        hi, lo = _split2(t)
        return _dot(hi, ones_bd) + _dot(lo, ones_bd)

    kk = k * kk_ref[...]
    kk = kk / jnp.maximum(jnp.sqrt(head_sum(kk * kk)), 1e-12)
    k2 = k * (1.0 + (a - 1.0) * ka_ref[...])
    r_o[...] = r
    k_o[...] = k2
    v_o[...] = v
    kk_o[...] = kk
    b_o[...] = kk * a
    lw_o[...] = -jnp.exp(w)
    bonus_o[...] = head_sum(r * k2 * rk_ref[...]) * v
    g_o[...] = _dot(jax.nn.sigmoid(gl).astype(BF16), g2_ref[...])


def _rwkv_prep(cols, s, mu, w0, w2, a0, a2, g2, k_k, k_a, r_k, tm):
    t, wc = cols.shape
    W = RWKV_WIDTH
    wwa = jnp.zeros((DECAY_RANK + AAA_RANK, 2 * W), F32)
    wwa = wwa.at[:DECAY_RANK, :W].set(w2).at[DECAY_RANK:, W:].set(a2).astype(BF16)
    head_id = jnp.arange(W) // RWKV_HEAD_DIM
    ones_bd = (head_id[:, None] == head_id[None, :]).astype(BF16)
    row = lambda a: a.reshape(1, -1)
    vec = lambda n: pl.BlockSpec((1, n), lambda i: (0, 0))
    tok = jax.ShapeDtypeStruct((t, W), F32)
    tok_spec = pl.BlockSpec((tm, W), lambda i: (i, 0))
    return pl.pallas_call(
        functools.partial(_rwkv_prep_kernel, tiles_per_seq=s // tm),
        out_shape=(tok,) * 8,
        grid=(t // tm,),
        in_specs=[pl.BlockSpec((tm, wc), lambda i: (i, 0)),
                  pl.BlockSpec((8, wc), lambda i: (jnp.maximum(i * (tm // 8) - 1, 0), 0)),
                  vec(wc), vec(W), vec(W), vec(W), vec(W), vec(W),
                  pl.BlockSpec(wwa.shape, lambda i: (0, 0)),
                  pl.BlockSpec((GATE_RANK, W), lambda i: (0, 0)),
                  pl.BlockSpec((W, W), lambda i: (0, 0))],
        out_specs=(tok_spec,) * 8,
        compiler_params=_cparams(("parallel",)),
        name="rwkv_prep",
    )(cols, cols, row(mu), row(w0), row(a0), row(k_k), row(k_a), row(r_k.reshape(-1)),
      wwa, g2.astype(BF16), ones_bd)


RWKV_CHUNK = 64
RWKV_GROUP = 8


def _rwkv_chunk_kernel(r_ref, k_ref, v_ref, kk_ref, b_ref, lw_ref, bonus_ref, g_ref,
                       lg_ref, lb_ref, o_ref, st_sc):
    C = RWKV_CHUNK
    N = RWKV_HEAD_DIM
    P = 2 * N
    nb = r_ref.shape[0]
    npair = RWKV_WIDTH // P

    @pl.when(pl.program_id(0) == 0)
    def _():
        st_sc[...] = jnp.zeros(st_sc.shape, F32)

    row_c = lax.broadcasted_iota(I32, (C, P), 0)
    lane_c = lax.broadcasted_iota(I32, (C, P), 1)
    tcol = lane_c & (N - 1)
    strict = tcol < row_c
    incl = tcol <= row_c
    lane_lo_n = lax.broadcasted_iota(I32, (N, P), 1) < N
    r2 = lax.broadcasted_iota(I32, (P, P), 0)
    l2 = lax.broadcasted_iota(I32, (P, P), 1)
    bdmask = (r2 < N) == (l2 < N)
    ones_bd = jnp.where(bdmask, 1.0, 0.0).astype(BF16)
    ones2 = jnp.concatenate([ones_bd, ones_bd], axis=0)
    ti = lax.broadcasted_iota(I32, (C, C), 0)
    si = lax.broadcasted_iota(I32, (C, C), 1)
    tri = jnp.where(si <= ti, 1.0, 0.0).astype(BF16)
    tri3 = jnp.concatenate([tri, tri, tri], axis=1)

    def bd(x):
        return jnp.where(bdmask, jnp.concatenate([x, x], axis=0), 0.0).astype(BF16)

    def head_mean(t):
        hi, lo = _split2(t)
        return _dot(jnp.concatenate([hi, lo], axis=1), ones2) * (1.0 / N)

    units = [(bi, p) for bi in range(nb) for p in range(npair)]
    nlev = max(1, (C - 1).bit_length())
    for g0 in range(0, len(units), RWKV_GROUP):
        grp = units[g0:g0 + RWKV_GROUP]
        n = len(grp)
        ld = lambda ref: [ref[bi, :, p * P:(p + 1) * P] for bi, p in grp]
        r, k, v, kk, b, lw = ld(r_ref), ld(k_ref), ld(v_ref), ld(kk_ref), ld(b_ref), ld(lw_ref)

        cum = []
        for u in range(n):
            hi = lw[u].astype(BF16)
            r1 = lw[u] - hi.astype(F32)
            mid = r1.astype(BF16)
            lo = (r1 - mid.astype(F32)).astype(BF16)
            cum.append(_dot(tri3, jnp.concatenate([hi, mid, lo], axis=0)))
        cum_last = [c[C - 1:C, :] for c in cum]
        p_inv = [jnp.exp(-c) for c in cum]
        a_t = [-kk[u] * jnp.exp(cum[u] - lw[u]) for u in range(n)]
        r_t = [r[u] * jnp.exp(cum[u]) for u in range(n)]
        dec = [jnp.exp(cum_last[u] - cum[u]) for u in range(n)]
        ar = [jnp.concatenate([a_t[u], r_t[u]], axis=0).astype(BF16) for u in range(n)]
        sb = [_dot_nt(ar[u], bd(b[u] * p_inv[u])) for u in range(n)]
        sk = [_dot_nt(ar[u], bd(k[u] * p_inv[u])) for u in range(n)]
        bd_v = [bd(v[u]) for u in range(n)]
        lp = [jnp.where(strict, sb[u][:C], 0.0) for u in range(n)]
        l_ak = [jnp.where(strict, sk[u][:C], 0.0).astype(BF16) for u in range(n)]
        m_rb = [jnp.where(incl, sb[u][C:], 0.0).astype(BF16) for u in range(n)]
        m_rk = [jnp.where(incl, sk[u][C:], 0.0).astype(BF16) for u in range(n)]

        xa = list(a_t)
        xu = [_dot(l_ak[u], bd_v[u]) for u in range(n)]
        for lev in range(nlev):
            last = lev == nlev - 1
            for u in range(n):
                parts = [bd(xa[u]), bd(xu[u])] + ([] if last else [bd(lp[u])])
                res = _dot(lp[u].astype(BF16), jnp.concatenate(parts, axis=1))
                xa[u] = xa[u] + res[:, :P]
                xu[u] = xu[u] + res[:, P:2 * P]
                if not last:
                    lp[u] = res[:, 2 * P:]

        res = [_dot(m_rb[u], jnp.concatenate([bd(xa[u]), bd(xu[u])], axis=1)) for u in range(n)]
        r_hat = [r_t[u] + res[u][:, :P] for u in range(n)]
        y0 = [res[u][:, P:] + _dot(m_rk[u], bd_v[u]) for u in range(n)]
        zt = [jnp.concatenate([b[u] * dec[u], k[u] * dec[u]], axis=0).T.astype(BF16) for u in range(n)]
        pct = [jnp.broadcast_to(cum_last[u], (P, P)).T for u in range(n)]
        pcm = [jnp.exp(jnp.where(lane_lo_n, pct[u][:N], pct[u][N:])) for u in range(n)]

        y = []
        for u, (bi, p) in enumerate(grp):
            st = st_sc[bi, p]
            ws = _dot(jnp.concatenate([xa[u], r_hat[u]], axis=0).astype(BF16), bd(st))
            wc = ws[:C] + xu[u]
            y.append(ws[C:] + y0[u])
            full = _dot(zt[u], jnp.concatenate([wc, v[u]], axis=0).astype(BF16))
            st_sc[bi, p] = pcm[u] * st + jnp.where(lane_lo_n, full[:N], full[N:])

        for u, (bi, p) in enumerate(grp):
            sl = slice(p * P, (p + 1) * P)
            yc = y[u] - head_mean(y[u])
            yn = yc * lax.rsqrt(head_mean(yc * yc) + GN_EPS) * lg_ref[:, sl] + lb_ref[:, sl]
            o_ref[bi, :, sl] = ((yn + bonus_ref[bi, :, sl]) * g_ref[bi, :, sl]).astype(o_ref.dtype)


def _rwkv_chunk(r, k, v, kk, bvec, lw, bonus, g, lnx_g, lnx_b):
    b, s, w = r.shape
    C = RWKV_CHUNK
    spec = pl.BlockSpec((b, C, w), lambda j: (0, j, 0))
    pspec = pl.BlockSpec((1, w), lambda j: (0, 0))
    return pl.pallas_call(
        _rwkv_chunk_kernel,
        out_shape=jax.ShapeDtypeStruct((b, s, w), BF16),
        grid=(s // C,),
        in_specs=[spec] * 8 + [pspec, pspec],
        out_specs=spec,
        scratch_shapes=[pltpu.VMEM((b, w // (2 * RWKV_HEAD_DIM), RWKV_HEAD_DIM, 2 * RWKV_HEAD_DIM), F32)],
        compiler_params=_cparams(("arbitrary",)),
        name="rwkv_chunk",
    )(r, k, v, kk, bvec, lw, bonus, g, lnx_g.reshape(1, w), lnx_b.reshape(1, w))


def _mem_attn_kernel(q_ref, kv_ref, o_ref):
    W = MEM_HEADS * MEM_HEAD_DIM
    scale = MEM_HEAD_DIM ** -0.5
    for h in range(MEM_HEADS):
        sl = slice(h * MEM_HEAD_DIM, (h + 1) * MEM_HEAD_DIM)
        q = q_ref[0, :, sl]
        k = kv_ref[0, :, sl]
        v = kv_ref[0, :, W + h * MEM_HEAD_DIM:W + (h + 1) * MEM_HEAD_DIM]
        s = _dot_nt(q, k) * scale
        s = s - jnp.max(s, axis=-1, keepdims=True)
        p = jnp.exp(s)
        p = p / jnp.sum(p, axis=-1, keepdims=True)
        o_ref[0, :, sl] = _dot(p.astype(BF16), v).astype(o_ref.dtype)


def _mem_attn(q, kv, tq):
    b, s, w = q.shape
    m = kv.shape[1]
    return pl.pallas_call(
        _mem_attn_kernel,
        out_shape=jax.ShapeDtypeStruct((b, s, w), BF16),
        grid=(b, s // tq),
        in_specs=[pl.BlockSpec((1, tq, w), lambda i, j: (i, j, 0)),
                  pl.BlockSpec((1, m, 2 * w), lambda i, j: (i, 0, 0))],
        out_specs=pl.BlockSpec((1, tq, w), lambda i, j: (i, j, 0)),
        compiler_params=_cparams(("parallel", "parallel")),
        name="mem_attn",
    )(q, kv)


def _layernorm(z, g, b):
    mu = jnp.mean(z, axis=-1, keepdims=True)
    zc = z - mu
    var = jnp.mean(zc * zc, axis=-1, keepdims=True)
    return zc * lax.rsqrt(var + LN_EPS) * g + b


def _merge_kernel(x_ref, yd_ref, yr_ref, ym_ref, gt_ref, wd_ref, wr_ref, wm_ref, wo_ref,
                  g_ref, b_ref, x1_ref, x1b_ref, x1t_ref, *, alpha):
    d = x_ref.shape[-1]
    br_d = _dot(yd_ref[0], wd_ref[...])
    br_r = _dot(yr_ref[0], wr_ref[...])
    br_m = _dot(ym_ref[0], wm_ref[...])
    gt = gt_ref[0]
    merged = (gt[:, 0:d].astype(F32) * br_d + gt[:, d:2 * d].astype(F32) * br_r
              + gt[:, 2 * d:3 * d].astype(F32) * br_m)
    z = alpha * x_ref[0] + _dot(merged.astype(BF16), wo_ref[...])
    x1 = _layernorm(z, g_ref[...], b_ref[...])
    x1_ref[0] = x1
    x1b_ref[0] = x1.astype(BF16)
    x1t_ref[0] = x1.T.astype(BF16)


def _merge(x, y_dsa, y_rwkv, y_mem, gates, w_br_dsa, w_br_rwkv, w_br_mem, w_out, ln_g, ln_b, alpha, tm):
    b, s, d = x.shape
    full2 = lambda a: pl.BlockSpec(a.shape, lambda i, j: (0,) * a.ndim)
    wd = w_br_dsa.astype(BF16)
    wr = w_br_rwkv.astype(BF16)
    wm = w_br_mem.astype(BF16)
    wo = w_out.astype(BF16)
    g2 = ln_g.reshape(1, d)
    b2 = ln_b.reshape(1, d)
    tok = lambda w: pl.BlockSpec((1, tm, w), lambda i, j: (i, j, 0))
    return pl.pallas_call(
        functools.partial(_merge_kernel, alpha=alpha),
        out_shape=(jax.ShapeDtypeStruct((b, s, d), F32),
                   jax.ShapeDtypeStruct((b, s, d), BF16),
                   jax.ShapeDtypeStruct((b, d, s), BF16)),
        grid=(b, s // tm),
        in_specs=[tok(d), tok(y_dsa.shape[-1]), tok(y_rwkv.shape[-1]), tok(y_mem.shape[-1]), tok(3 * d),
                  full2(wd), full2(wr), full2(wm), full2(wo), full2(g2), full2(b2)],
        out_specs=(tok(d), tok(d), pl.BlockSpec((1, d, tm), lambda i, j: (i, 0, j))),
        compiler_params=_cparams(("parallel", "parallel")),
        name="merge",
    )(x, y_dsa, y_rwkv, y_mem, gates, wd, wr, wm, wo, g2, b2)


def _peer_route_kernel(x_ref, wq_ref, keys_ref, r1_o, c1_o, rk2_o, p2_o, a1_sc, a2_sc):
    tt = x_ref.shape[1]
    K = PEER_TOPK
    q = _dot(x_ref[0], wq_ref[...])
    keys0 = keys_ref[0]
    keys1 = keys_ref[1]

    def top_ranks(sub, sc):
        x = sub
        rank = jnp.full(sub.shape, float(K), F32)
        for rnk in range(K):
            m = jnp.max(x, axis=0, keepdims=True)
            sc[rnk:rnk + 1, :] = m
            hit = x == m
            rank = jnp.where(hit, float(rnk), rank)
            x = jnp.where(hit, NEG_INF, x)
        return rank

    for h in range(PEER_HEADS):
        q1 = q[:, (2 * h) * PEER_HALF:(2 * h + 1) * PEER_HALF]
        q2 = q[:, (2 * h + 1) * PEER_HALF:(2 * h + 2) * PEER_HALF]
        s1 = _dot3(keys0, q1, nt=True)
        s2 = _dot3(keys1, q2, nt=True)
        rank1 = top_ranks(s1, a1_sc)
        rank2 = top_ranks(s2, a2_sc)
        pieces = [a1_sc[i:i + 1, :] + a2_sc[0:K // (i + 1), :] for i in range(K)]
        best = pieces[0][0:1, :]
        zsum = jnp.zeros((1, tt), F32)
        v_k = best
        work = pieces
        for rnk in range(K):
            m = work[0].max(axis=0, keepdims=True)
            for pc in work[1:]:
                m = jnp.maximum(m, pc.max(axis=0, keepdims=True))
            v_k = m
            zsum = zsum + jnp.exp(m - best)
            work = [jnp.where(pc == m, NEG_INF, pc) for pc in work]
        r1 = jnp.zeros(s1.shape, F32)
        for i in range(K):
            cnt = jnp.sum(jnp.where(pieces[i] >= v_k, 1.0, 0.0), axis=0, keepdims=True)
            r1 = jnp.where(rank1 == float(i), cnt, r1)
        r1_o[0, h] = r1
        c1_o[0, h] = jnp.exp(s1 - a1_sc[0:1, :]) / zsum
        rk2_o[0, h] = rank2.astype(BF16)
        p2_o[0, h] = jnp.exp(s2 - a2_sc[0:1, :]).astype(BF16)


def _peer_route(x1b, w_peer_q, peer_keys, tt):
    b, s, d = x1b.shape
    nk = PEER_N_KEYS
    wq = w_peer_q.astype(BF16)
    o32 = jax.ShapeDtypeStruct((b, PEER_HEADS, nk, s), F32)
    o16 = jax.ShapeDtypeStruct((b, PEER_HEADS, nk, s), BF16)
    ospec = pl.BlockSpec((1, PEER_HEADS, nk, tt), lambda i, j: (i, 0, 0, j))
    return pl.pallas_call(
        _peer_route_kernel,
        out_shape=(o32, o32, o16, o16),
        grid=(b, s // tt),
        in_specs=[pl.BlockSpec((1, tt, d), lambda i, j: (i, j, 0)),
                  pl.BlockSpec(wq.shape, lambda i, j: (0, 0)),
                  pl.BlockSpec(peer_keys.shape, lambda i, j: (0, 0, 0))],
        out_specs=(ospec,) * 4,
        scratch_shapes=[pltpu.VMEM((PEER_TOPK, tt), F32), pltpu.VMEM((PEER_TOPK, tt), F32)],
        compiler_params=_cparams(("parallel", "parallel")),
        name="peer_route",
    )(x1b, wq, peer_keys)


PEER_ET = 1024


def _peer_ffn_kernel(x1_ref, x1t_ref, r1_ref, c1_ref, rk2_ref, p2_ref, u_ref, vt_ref,
                     g_ref, b_ref, o_ref, acc_sc, *, alpha):
    e = pl.program_id(2)
    nk = PEER_N_KEYS
    tt = x1t_ref.shape[2]

    @pl.when(e == 0)
    def _():
        acc_sc[...] = jnp.zeros(acc_sc.shape, F32)

    act = _dot(u_ref[...], x1t_ref[0])
    act = 0.5 * act * (1.0 + lax.erf(act * (2.0 ** -0.5)))
    gate_rows = []
    for rr in range(PEER_ET // nk):
        gsum = jnp.zeros((nk, tt), BF16)
        for h in range(PEER_HEADS):
            r1 = r1_ref[0, h, rr:rr + 1, :].astype(BF16)
            c1 = c1_ref[0, h, rr:rr + 1, :].astype(BF16)
            sel = rk2_ref[0, h] < r1
            gsum = gsum + jnp.where(sel, p2_ref[0, h], jnp.zeros((), BF16)) * c1
        gate_rows.append(gsum)
    gate = jnp.concatenate(gate_rows, axis=0)
    w = gate * act.astype(BF16)
    acc_sc[...] += _dot(vt_ref[...], w)

    @pl.when(e == pl.num_programs(2) - 1)
    def _():
        z = alpha * x1_ref[0] + acc_sc[...].T
        o_ref[0] = _layernorm(z, g_ref[...], b_ref[...])


def _peer_ffn(x1, x1t, r1, c1, rk2, p2, peer_u, peer_v, ln_g, ln_b, alpha, tt):
    b, s, d = x1.shape
    ne = peer_u.shape[0]
    nk = PEER_N_KEYS
    rows = PEER_ET // nk
    u = peer_u.astype(BF16)
    vt = peer_v.T.astype(BF16)
    row_spec = pl.BlockSpec((1, PEER_HEADS, rows, tt), lambda i, j, e: (i, 0, e, j))
    key_spec = pl.BlockSpec((1, PEER_HEADS, nk, tt), lambda i, j, e: (i, 0, 0, j))
    return pl.pallas_call(
        functools.partial(_peer_ffn_kernel, alpha=alpha),
        out_shape=jax.ShapeDtypeStruct((b, s, d), F32),
        grid=(b, s // tt, ne // PEER_ET),
        in_specs=[pl.BlockSpec((1, tt, d), lambda i, j, e: (i, j, 0)),
                  pl.BlockSpec((1, d, tt), lambda i, j, e: (i, 0, j)),
                  row_spec, row_spec, key_spec, key_spec,
                  pl.BlockSpec((PEER_ET, d), lambda i, j, e: (e, 0)),
                  pl.BlockSpec((d, PEER_ET), lambda i, j, e: (0, e)),
                  pl.BlockSpec((1, d), lambda i, j, e: (0, 0)),
                  pl.BlockSpec((1, d), lambda i, j, e: (0, 0))],
        out_specs=pl.BlockSpec((1, tt, d), lambda i, j, e: (i, j, 0)),
        scratch_shapes=[pltpu.VMEM((d, tt), F32)],
        compiler_params=_cparams(("parallel", "parallel", "arbitrary")),
        name="peer_ffn",
    )(x1, x1t, r1, c1, rk2, p2, u, vt, ln_g.reshape(1, d), ln_b.reshape(1, d))


def _tile(n, pref):
    t = min(n, pref)
    while n % t:
        t //= 2
    return t


def _layer(x, mem, rel_bias, w_in, q_norm_g, kv_norm_g, w_uq, w_uk, w_uv, w_idx_q,
           rwkv_mu, rwkv_w0, rwkv_w2, rwkv_a0, rwkv_a2, rwkv_g2, rwkv_k_k, rwkv_k_a, rwkv_r_k,
           rwkv_lnx_g, rwkv_lnx_b, w_mem_kv, w_br_dsa, w_br_rwkv, w_br_mem, w_out, ln1_g, ln1_b,
           w_peer_q, peer_keys, peer_u, peer_v, ln2_g, ln2_b, alpha):
    b, s, d = x.shape
    t = b * s
    tm = _tile(s, 512)
    x2 = x.reshape(t, d)
    n_dsa = DSA_Q_RANK + DSA_KV_RANK + IDX_DIM + IDX_HEADS
    n_rwkv = 3 * RWKV_WIDTH + DECAY_RANK + AAA_RANK + GATE_RANK
    n_mem = MEM_HEADS * MEM_HEAD_DIM
    o1, o2, o3 = n_dsa, n_dsa + n_rwkv, n_dsa + n_rwkv + n_mem
    wb = w_in.astype(BF16)
    dcols = _linear(x2, wb[:, :o1], F32, tm, n_dsa, name="in_dsa")
    rcols = _linear(x2, wb[:, o1:o2], F32, tm, n_rwkv // 2, name="in_rwkv")
    mem_q = _linear(x2, wb[:, o2:o3], BF16, tm, n_mem, name="in_memq")
    gates = _linear(x2, wb[:, o3:], BF16, tm, d, act="sigmoid", name="in_gates")

    q_lat, q_idx, w_idx, k_idx, c_kv = _dsa_prep(dcols.reshape(b, s, n_dsa), q_norm_g, kv_norm_g,
                                                 w_uq, w_uk, w_idx_q, tm)
    y_dsa = _dsa_attn(rel_bias, q_idx, w_idx, q_lat, k_idx, c_kv, w_uv)

    prep = _rwkv_prep(rcols, s, rwkv_mu, rwkv_w0, rwkv_w2, rwkv_a0, rwkv_a2, rwkv_g2,
                      rwkv_k_k, rwkv_k_a, rwkv_r_k, tm)
    y_rwkv = _rwkv_chunk(*[a.reshape(b, s, RWKV_WIDTH) for a in prep], rwkv_lnx_g, rwkv_lnx_b)

    m = mem.shape[1]
    mkv = _linear(mem.reshape(b * m, d), w_mem_kv.astype(BF16), BF16, _tile(b * m, 512), 2 * n_mem,
                  name="mem_kv").reshape(b, m, 2 * n_mem)
    y_mem = _mem_attn(mem_q.reshape(b, s, n_mem), mkv, tm)

    x1, x1b, x1t = _merge(x, y_dsa, y_rwkv, y_mem, gates.reshape(b, s, 3 * d),
                          w_br_dsa, w_br_rwkv, w_br_mem, w_out, ln1_g, ln1_b, alpha, tm)

    r1, c1, rk2, p2 = _peer_route(x1b, w_peer_q, peer_keys, _tile(s, 512))
    return _peer_ffn(x1, x1t, r1, c1, rk2, p2, peer_u, peer_v, ln2_g, ln2_b, alpha, _tile(s, 1024))


def kernel(x, mem, rel_bias, w_in, q_norm_g, kv_norm_g, w_uq, w_uk, w_uv, w_idx_q, rwkv_mu, rwkv_w0, rwkv_w2, rwkv_a0, rwkv_a2, rwkv_g2, rwkv_k_k, rwkv_k_a, rwkv_r_k, rwkv_lnx_g, rwkv_lnx_b, w_mem_kv, w_br_dsa, w_br_rwkv, w_br_mem, w_out, ln1_g, ln1_b, w_peer_q, peer_keys, peer_u, peer_v, ln2_g, ln2_b):
    depth = w_in.shape[0]
    alpha = (2 * depth) ** 0.25
    for l in range(depth):
        x = _layer(x, mem, rel_bias, w_in[l], q_norm_g[l], kv_norm_g[l], w_uq[l], w_uk[l], w_uv[l],
                   w_idx_q[l], rwkv_mu[l], rwkv_w0[l], rwkv_w2[l], rwkv_a0[l], rwkv_a2[l], rwkv_g2[l],
                   rwkv_k_k[l], rwkv_k_a[l], rwkv_r_k[l], rwkv_lnx_g[l], rwkv_lnx_b[l], w_mem_kv[l],
                   w_br_dsa[l], w_br_rwkv[l], w_br_mem[l], w_out[l], ln1_g[l], ln1_b[l],
                   w_peer_q[l], peer_keys[l], peer_u[l], peer_v[l], ln2_g[l], ln2_b[l], alpha)
    return x
```

```python
import functools
import math

import jax
import jax.numpy as jnp
from jax import lax
from jax.experimental import pallas as pl
from jax.experimental.pallas import tpu as pltpu

F32 = jnp.float32
BF16 = jnp.bfloat16
I32 = jnp.int32

DSA_HEADS = 8
DSA_HEAD_DIM = 64
DSA_Q_RANK = 256
DSA_KV_RANK = 128
IDX_HEADS = 8
IDX_DIM = 32
INDEX_TOPK_MAX = 256
NUM_BUCKETS = 32
MAX_DISTANCE = 128
RWKV_HEADS = 8
RWKV_HEAD_DIM = 64
RWKV_WIDTH = RWKV_HEADS * RWKV_HEAD_DIM
DECAY_RANK = 64
AAA_RANK = 64
GATE_RANK = 128
GN_EPS = 64e-5
MEM_HEADS = 4
MEM_HEAD_DIM = 128
PEER_HEADS = 8
PEER_N_KEYS = 128
PEER_HALF = 128
PEER_TOPK = 16
LN_EPS = 1e-5
RMS_EPS = 1e-6

LANES = 128
VMEM_LIMIT = 56 * 1024 * 1024

INT_MIN = -(2 ** 31)
NEG_INF = float("-inf")


def _cparams(sem, vmem=VMEM_LIMIT):
    return pltpu.CompilerParams(dimension_semantics=sem, vmem_limit_bytes=vmem)


def _dot(a, b):
    return jnp.dot(a, b, preferred_element_type=F32)


def _dot_nt(a, b):
    return lax.dot_general(a, b, (((1,), (1,)), ((), ())), preferred_element_type=F32)


def _split2(x):
    hi = x.astype(BF16)
    lo = (x - hi.astype(F32)).astype(BF16)
    return hi, lo


def _dot3(a, b, nt=False):
    d = _dot_nt if nt else _dot
    ah, al = _split2(a)
    bh, bl = _split2(b)
    return d(ah, bh) + (d(ah, bl) + d(al, bh))


def _linear_kernel(x_ref, w_ref, o_ref, *, act):
    y = _dot(x_ref[...].astype(BF16), w_ref[...])
    if act == "sigmoid":
        y = jax.nn.sigmoid(y)
    o_ref[...] = y.astype(o_ref.dtype)


def _linear(x, w, out_dtype, tm, tn, act=None, name="linear"):
    m, k = x.shape
    n = w.shape[1]
    return pl.pallas_call(
        functools.partial(_linear_kernel, act=act),
        out_shape=jax.ShapeDtypeStruct((m, n), out_dtype),
        grid=(m // tm, n // tn),
        in_specs=[pl.BlockSpec((tm, k), lambda i, j: (i, 0)),
                  pl.BlockSpec((k, tn), lambda i, j: (0, j))],
        out_specs=pl.BlockSpec((tm, tn), lambda i, j: (i, j)),
        compiler_params=_cparams(("parallel", "parallel")),
        name=name,
    )(x, w)


def _dsa_prep_kernel(dc_ref, qg_ref, kvg_ref, wuq_ref, wuk_ref, wiq_ref,
                     ql_ref, qi_ref, wi_ref, ki_ref, kv_ref):
    dc = dc_ref[0]
    c_q = dc[:, :DSA_Q_RANK]
    c_kv = dc[:, DSA_Q_RANK:DSA_Q_RANK + DSA_KV_RANK]
    off = DSA_Q_RANK + DSA_KV_RANK
    k_idx = dc[:, off:off + IDX_DIM]
    w_idx = dc[:, off + IDX_DIM:off + IDX_DIM + IDX_HEADS]
    c_q = c_q * lax.rsqrt(jnp.mean(c_q * c_q, axis=-1, keepdims=True) + RMS_EPS) * qg_ref[...]
    c_kv = c_kv * lax.rsqrt(jnp.mean(c_kv * c_kv, axis=-1, keepdims=True) + RMS_EPS) * kvg_ref[...]
    cqb = c_q.astype(BF16)
    scale = DSA_HEAD_DIM ** -0.5
    for h in range(DSA_HEADS):
        q_h = _dot(cqb, wuq_ref[h])
        ql_ref[0, h] = (_dot(q_h.astype(BF16), wuk_ref[h]) * scale).astype(BF16)
        qi_ref[0, h] = _dot(cqb, wiq_ref[h]).astype(BF16)
    wi_ref[0] = w_idx * (IDX_HEADS ** -0.5 * IDX_DIM ** -0.5)
    ki_ref[0] = k_idx.astype(BF16)
    kv_ref[0] = c_kv.astype(BF16)


def _dsa_prep(dcols, q_norm_g, kv_norm_g, w_uq, w_uk, w_idx_q, tm):
    b, s, wd = dcols.shape
    h = DSA_HEADS
    wuq_h = w_uq.reshape(DSA_Q_RANK, h, DSA_HEAD_DIM).transpose(1, 0, 2).astype(BF16)
    wuk_h = w_uk.transpose(1, 2, 0).astype(BF16)
    wiq_h = w_idx_q.reshape(DSA_Q_RANK, IDX_HEADS, IDX_DIM).transpose(1, 0, 2).astype(BF16)
    full = lambda shape: pl.BlockSpec(shape, lambda i, j: (0,) * len(shape))
    return pl.pallas_call(
        _dsa_prep_kernel,
        out_shape=(jax.ShapeDtypeStruct((b, h, s, DSA_KV_RANK), BF16),
                   jax.ShapeDtypeStruct((b, IDX_HEADS, s, IDX_DIM), BF16),
                   jax.ShapeDtypeStruct((b, s, IDX_HEADS), F32),
                   jax.ShapeDtypeStruct((b, s, IDX_DIM), BF16),
                   jax.ShapeDtypeStruct((b, s, DSA_KV_RANK), BF16)),
        grid=(b, s // tm),
        in_specs=[pl.BlockSpec((1, tm, wd), lambda i, j: (i, j, 0)),
                  full((1, DSA_Q_RANK)), full((1, DSA_KV_RANK)),
                  full(wuq_h.shape), full(wuk_h.shape), full(wiq_h.shape)],
        out_specs=(pl.BlockSpec((1, h, tm, DSA_KV_RANK), lambda i, j: (i, 0, j, 0)),
                   pl.BlockSpec((1, IDX_HEADS, tm, IDX_DIM), lambda i, j: (i, 0, j, 0)),
                   pl.BlockSpec((1, tm, IDX_HEADS), lambda i, j: (i, j, 0)),
                   pl.BlockSpec((1, tm, IDX_DIM), lambda i, j: (i, j, 0)),
                   pl.BlockSpec((1, tm, DSA_KV_RANK), lambda i, j: (i, j, 0))),
        compiler_params=_cparams(("parallel", "parallel")),
        name="dsa_prep",
    )(dcols, q_norm_g.reshape(1, -1), kv_norm_g.reshape(1, -1), wuq_h, wuk_h, wiq_h)


DSA_QB = 128
DSA_KB = 128
DSA_SC = 512


def _t5_bias_tile(rel_ref, h, off):
    row = lax.broadcasted_iota(I32, (DSA_QB, DSA_KB), 0)
    col = lax.broadcasted_iota(I32, (DSA_QB, DSA_KB), 1)
    n = jnp.maximum(row - col + off, 0)
    max_exact = NUM_BUCKETS // 2
    nf = jnp.maximum(n, max_exact).astype(F32)
    large = max_exact + jnp.floor(jnp.log(nf / max_exact) / math.log(MAX_DISTANCE / max_exact)
                                  * (NUM_BUCKETS - max_exact)).astype(I32)
    large = jnp.minimum(large, NUM_BUCKETS - 1)
    bucket = jnp.where(n < max_exact, n, large)
    last = rel_ref[NUM_BUCKETS - 1, h]
    out = jnp.zeros((DSA_QB, DSA_KB), F32)
    for bk in range(NUM_BUCKETS - 1):
        out = jnp.where(bucket == bk, rel_ref[bk, h] - last, out)
    return out


def _fold_lanes(x, op):
    out = x[..., :LANES]
    for c in range(1, x.shape[-1] // LANES):
        out = op(out, x[..., c * LANES:(c + 1) * LANES])
    return out


def _dsa_attn_kernel(rel_ref, qi_ref, wi_ref, ql_ref, ki_ref, kv_ref, wuv_ref, o_ref,
                     key_sc, bias_sc, mx_sc, ls_sc, acc_sc, *, topk, seq):
    H = DSA_HEADS
    QB, KB, SC = DSA_QB, DSA_KB, DSA_SC
    qb = pl.program_id(1)
    q0 = qb * QB

    @pl.when((pl.program_id(0) == 0) & (qb == 0))
    def _():
        for h in range(H):
            bias_sc[0, h] = _t5_bias_tile(rel_ref, h, 0)
            bias_sc[1, h] = _t5_bias_tile(rel_ref, h, KB)

    n_sc = (q0 + QB + SC - 1) // SC
    wi = wi_ref[0]
    HALF = SC // 2
    row_g = q0 + lax.broadcasted_iota(I32, (QB, HALF), 0)
    col_h = lax.broadcasted_iota(I32, (QB, HALF), 1)

    def score_body(j, carry):
        for c in range(2):
            k0 = pl.multiple_of(j * SC + c * HALF, HALF)
            ks = ki_ref[0, pl.ds(k0, HALF), :]
            acc = jnp.zeros((QB, HALF), F32)
            for h in range(IDX_HEADS):
                lg = _dot_nt(qi_ref[0, h], ks)
                acc = acc + wi[:, h:h + 1] * jnp.maximum(lg, 0.0)
            key_sc[:, pl.ds(k0, HALF)] = jnp.where(col_h + k0 <= row_g, acc, NEG_INF)
        return carry

    lax.fori_loop(0, n_sc, score_body, 0)

    col_l = lax.broadcasted_iota(I32, (QB, LANES), 1)

    def count(ind):
        def body(j, acc):
            for c in range(SC // LANES):
                k0 = pl.multiple_of(j * SC + c * LANES, LANES)
                acc = acc + ind(key_sc[:, pl.ds(k0, LANES)], k0)
            return acc
        acc = lax.fori_loop(0, n_sc, body, jnp.zeros((QB, LANES), I32))
        return jnp.sum(acc, axis=1, keepdims=True)

    def code_to_float(ru):
        c = ru ^ INT_MIN
        return pltpu.bitcast(c ^ ((c >> 31) & 0x7FFFFFFF), jnp.float32)

    HB = QB // 2

    def descent(n):
        def count_half(r0, ru, bit):
            cand = code_to_float(ru | bit)
            acc = jnp.zeros((HB, LANES), I32)
            for c in range(n * SC // LANES):
                acc = acc + jnp.where(key_sc[r0:r0 + HB, c * LANES:(c + 1) * LANES] >= cand, 1, 0)
            return acc

        def update(ru, acc, bit):
            cnt = jnp.sum(acc, axis=1, keepdims=True)
            return jnp.where(cnt >= topk, ru | bit, ru)

        def run():
            def bis_body(i, carry):
                ru_a, ru_b, acc_b = carry
                sh = lax.convert_element_type(31 - i, I32)
                bit = lax.shift_left(jnp.int32(1), sh)
                bit_b = jnp.where(sh == 31, 0, lax.shift_left(jnp.int32(2), jnp.minimum(sh, 30)))
                ru_b = update(ru_b, acc_b, bit_b)
                ru_a = update(ru_a, count_half(0, ru_a, bit), bit)
                return ru_a, ru_b, count_half(HB, ru_b, bit)

            zero = jnp.minimum(lax.broadcasted_iota(I32, (HB, LANES), 0), 0)
            ru_a, ru_b, acc_b = lax.fori_loop(0, 32, bis_body, (zero, zero, zero))
            return jnp.concatenate([ru_a, update(ru_b, acc_b, 1)], axis=0)
        return run

    skip = lambda: jnp.zeros((QB, LANES), I32)
    ru = lax.switch(jnp.where(q0 + QB <= topk, 0, n_sc),
                    [skip] + [descent(n) for n in range(1, seq // SC + 1)])[:, :1]
    few = q0 + lax.broadcasted_iota(I32, (QB, 1), 0) < topk
    tau = jnp.where(few, jnp.finfo(jnp.float32).min, code_to_float(ru))
    tau_b = jnp.broadcast_to(tau, (QB, LANES))
    cnt_ge = count(lambda blk, k0: jnp.where(blk >= tau_b, 1, 0))

    @pl.when(jnp.max(cnt_ge) > topk)
    def _():
        cnt_gt = count(lambda blk, k0: jnp.where(blk > tau_b, 1, 0))
        need = topk - cnt_gt

        def pos_body(_, lohi):
            lo, hi = lohi
            mid = (lo + hi) >> 1
            mid_b = jnp.broadcast_to(mid, (QB, LANES))
            f = count(lambda blk, k0: jnp.where(blk == tau_b, jnp.where(col_l + k0 <= mid_b, 1, 0), 0))
            ok = f >= need
            return jnp.where(ok, lo, mid + 1), jnp.where(ok, mid, hi)

        lo, _hi = lax.fori_loop(0, max(1, (seq - 1).bit_length()), pos_body,
                                (jnp.zeros((QB, 1), I32), jnp.full((QB, 1), seq - 1, I32)))
        jstar = jnp.broadcast_to(lo, (QB, LANES))

        def fix_body(j, carry):
            for c in range(SC // LANES):
                k0 = pl.multiple_of(j * SC + c * LANES, LANES)
                blk = key_sc[:, pl.ds(k0, LANES)]
                dropped = jnp.where(col_l + k0 > jstar, NEG_INF, blk)
                key_sc[:, pl.ds(k0, LANES)] = jnp.where(blk == tau_b, dropped, blk)
            return carry

        lax.fori_loop(0, n_sc, fix_body, 0)

    ql = ql_ref[0].reshape(H * QB, DSA_KV_RANK)
    near0 = jnp.maximum(qb - 1, 0) * KB
    n_far = (near0 + SC - 1) // SC
    tau_s = jnp.broadcast_to(tau, (QB, SC))
    tau_k = jnp.broadcast_to(tau, (QB, KB))
    col_s = lax.broadcasted_iota(I32, (QB, SC), 1)

    def far_logits(j):
        k0 = pl.multiple_of(j * SC, SC)
        kvb = kv_ref[0, pl.ds(k0, SC), :]
        blk = key_sc[:, pl.ds(k0, SC)]
        madd = jnp.where(blk >= tau_s, jnp.where(col_s + k0 < near0, 0.0, NEG_INF), NEG_INF)
        return _dot_nt(ql, kvb).reshape(H, QB, SC) + madd[None], kvb

    def near_logits(jb, bias):
        k0 = pl.multiple_of(jb * KB, KB)
        kvb = kv_ref[0, pl.ds(k0, KB), :]
        madd = jnp.where(key_sc[:, pl.ds(k0, KB)] >= tau_k, 0.0, NEG_INF)
        return _dot_nt(ql, kvb).reshape(H, QB, KB) + (bias + madd[None]), kvb

    mx_sc[...] = jnp.full(mx_sc.shape, NEG_INF, F32)

    def far_max(j, carry):
        s, _ = far_logits(j)
        mx_sc[...] = jnp.maximum(mx_sc[...], _fold_lanes(s, jnp.maximum))
        return carry

    lax.fori_loop(0, n_far, far_max, 0)

    @pl.when(qb >= 1)
    def _():
        s, _ = near_logits(qb - 1, bias_sc[1])
        mx_sc[...] = jnp.maximum(mx_sc[...], s)

    s, _ = near_logits(qb, bias_sc[0])
    mx_sc[...] = jnp.maximum(mx_sc[...], s)
    mx_sc[...] = jnp.broadcast_to(jnp.max(mx_sc[...], axis=-1, keepdims=True), mx_sc.shape)

    ls_sc[...] = jnp.zeros(ls_sc.shape, F32)
    acc_sc[...] = jnp.zeros(acc_sc.shape, F32)

    def far_acc(j, carry):
        s, kvb = far_logits(j)
        m_b = mx_sc[...]
        p = jnp.concatenate([jnp.exp(s[..., c * LANES:(c + 1) * LANES] - m_b)
                             for c in range(SC // LANES)], axis=-1)
        ls_sc[...] += _fold_lanes(p, jnp.add)
        acc_sc[...] += _dot(p.reshape(H * QB, SC).astype(BF16), kvb).reshape(H, QB, DSA_KV_RANK)
        return carry

    lax.fori_loop(0, n_far, far_acc, 0)

    def near_acc(jb, bias):
        s, kvb = near_logits(jb, bias)
        p = jnp.exp(s - mx_sc[...])
        ls_sc[...] += p
        acc_sc[...] += _dot(p.reshape(H * QB, KB).astype(BF16), kvb).reshape(H, QB, DSA_KV_RANK)

    @pl.when(qb >= 1)
    def _():
        near_acc(qb - 1, bias_sc[1])

    near_acc(qb, bias_sc[0])

    o_lat = acc_sc[...] / jnp.sum(ls_sc[...], axis=-1, keepdims=True)
    y = jnp.zeros((QB, DSA_HEADS * DSA_HEAD_DIM), F32)
    for h in range(H):
        y = y + _dot(o_lat[h].astype(BF16), wuv_ref[h])
    o_ref[0] = y.astype(o_ref.dtype)


def _dsa_attn(rel_bias, q_idx, w_idx, q_lat, k_idx, c_kv, w_uv):
    b, h, s, _ = q_lat.shape
    topk = min(INDEX_TOPK_MAX, s // 4)
    width = DSA_HEADS * DSA_HEAD_DIM
    eye = jnp.eye(DSA_HEADS, dtype=F32)
    wuv_e = jnp.einsum("chd,hg->hcgd", w_uv, eye).reshape(DSA_HEADS, DSA_KV_RANK, width).astype(BF16)
    QB = DSA_QB
    assert s % DSA_SC == 0
    return pl.pallas_call(
        functools.partial(_dsa_attn_kernel, topk=topk, seq=s),
        out_shape=jax.ShapeDtypeStruct((b, s, width), BF16),
        grid=(b, s // QB),
        in_specs=[pl.BlockSpec(memory_space=pltpu.SMEM),
                  pl.BlockSpec((1, IDX_HEADS, QB, IDX_DIM), lambda i, j: (i, 0, j, 0)),
                  pl.BlockSpec((1, QB, IDX_HEADS), lambda i, j: (i, j, 0)),
                  pl.BlockSpec((1, h, QB, DSA_KV_RANK), lambda i, j: (i, 0, j, 0)),
                  pl.BlockSpec((1, s, IDX_DIM), lambda i, j: (i, 0, 0)),
                  pl.BlockSpec((1, s, DSA_KV_RANK), lambda i, j: (i, 0, 0)),
                  pl.BlockSpec(wuv_e.shape, lambda i, j: (0, 0, 0))],
        out_specs=pl.BlockSpec((1, QB, width), lambda i, j: (i, j, 0)),
        scratch_shapes=[pltpu.VMEM((QB, s), F32),
                        pltpu.VMEM((2, h, QB, DSA_KB), F32),
                        pltpu.VMEM((h, QB, LANES), F32),
                        pltpu.VMEM((h, QB, LANES), F32),
                        pltpu.VMEM((h, QB, DSA_KV_RANK), F32)],
        compiler_params=_cparams(("arbitrary", "arbitrary")),
        name="dsa_attn",
    )(rel_bias, q_idx, w_idx, q_lat, k_idx, c_kv, wuv_e)


def _rwkv_prep_kernel(c_ref, p_ref, mu_ref, w0_ref, a0_ref, kk_ref, ka_ref, rk_ref,
                      wwa_ref, g2_ref, ones_ref,
                      r_o, k_o, v_o, kk_o, b_o, lw_o, bonus_o, g_o, *, tiles_per_seq):
    W = RWKV_WIDTH
    cols = c_ref[...]
    first = (pl.program_id(0) % tiles_per_seq) == 0
    prev_row = jnp.where(first, 0.0, p_ref[7:8, :])
    rolled = pltpu.roll(cols, 1, 0)
    row = lax.broadcasted_iota(I32, cols.shape, 0)
    prev = jnp.where(row == 0, prev_row, rolled)
    xs = cols + (prev - cols) * mu_ref[...]
    r = xs[:, 0:W]
    k = xs[:, W:2 * W]
    v = xs[:, 2 * W:3 * W]
    lora = xs[:, 3 * W:3 * W + DECAY_RANK + AAA_RANK]
    gl = xs[:, 3 * W + DECAY_RANK + AAA_RANK:]
    lane = lax.broadcasted_iota(I32, lora.shape, 1)
    lora = jnp.where(lane < DECAY_RANK, jnp.tanh(lora), lora)
    wa = _dot(lora.astype(BF16), wwa_ref[...])
    w = -jax.nn.softplus(-(w0_ref[...] + wa[:, :W])) - 0.5
    a = jax.nn.sigmoid(a0_ref[...] + wa[:, W:])
    ones_bd = ones_ref[...]

    def head_sum(t):
        hi, lo = _split2(t)
        return _dot(hi, ones_bd) + _dot(lo, ones_bd)

    kk = k * kk_ref[...]
    kk = kk / jnp.maximum(jnp.sqrt(head_sum(kk * kk)), 1e-12)
    k2 = k * (1.0 + (a - 1.0) * ka_ref[...])
    r_o[...] = r
    k_o[...] = k2
    v_o[...] = v
    kk_o[...] = kk
    b_o[...] = kk * a
    lw_o[...] = -jnp.exp(w)
    bonus_o[...] = head_sum(r * k2 * rk_ref[...]) * v
    g_o[...] = _dot(jax.nn.sigmoid(gl).astype(BF16), g2_ref[...])


def _rwkv_prep(cols, s, mu, w0, w2, a0, a2, g2, k_k, k_a, r_k, tm):
    t, wc = cols.shape
    W = RWKV_WIDTH
    wwa = jnp.zeros((DECAY_RANK + AAA_RANK, 2 * W), F32)
    wwa = wwa.at[:DECAY_RANK, :W].set(w2).at[DECAY_RANK:, W:].set(a2).astype(BF16)
    head_id = jnp.arange(W) // RWKV_HEAD_DIM
    ones_bd = (head_id[:, None] == head_id[None, :]).astype(BF16)
    row = lambda a: a.reshape(1, -1)
    vec = lambda n: pl.BlockSpec((1, n), lambda i: (0, 0))
    tok = jax.ShapeDtypeStruct((t, W), F32)
    tok_spec = pl.BlockSpec((tm, W), lambda i: (i, 0))
    return pl.pallas_call(
        functools.partial(_rwkv_prep_kernel, tiles_per_seq=s // tm),
        out_shape=(tok,) * 8,
        grid=(t // tm,),
        in_specs=[pl.BlockSpec((tm, wc), lambda i: (i, 0)),
                  pl.BlockSpec((8, wc), lambda i: (jnp.maximum(i * (tm // 8) - 1, 0), 0)),
                  vec(wc), vec(W), vec(W), vec(W), vec(W), vec(W),
                  pl.BlockSpec(wwa.shape, lambda i: (0, 0)),
                  pl.BlockSpec((GATE_RANK, W), lambda i: (0, 0)),
                  pl.BlockSpec((W, W), lambda i: (0, 0))],
        out_specs=(tok_spec,) * 8,
        compiler_params=_cparams(("parallel",)),
        name="rwkv_prep",
    )(cols, cols, row(mu), row(w0), row(a0), row(k_k), row(k_a), row(r_k.reshape(-1)),
      wwa, g2.astype(BF16), ones_bd)


RWKV_CHUNK = 64
RWKV_GROUP = 8


def _rwkv_chunk_kernel(r_ref, k_ref, v_ref, kk_ref, b_ref, lw_ref, bonus_ref, g_ref,
                       lg_ref, lb_ref, o_ref, st_sc):
    C = RWKV_CHUNK
    N = RWKV_HEAD_DIM
    P = 2 * N
    nb = r_ref.shape[0]
    npair = RWKV_WIDTH // P

    @pl.when(pl.program_id(0) == 0)
    def _():
        st_sc[...] = jnp.zeros(st_sc.shape, F32)

    row_c = lax.broadcasted_iota(I32, (C, P), 0)
    lane_c = lax.broadcasted_iota(I32, (C, P), 1)
    tcol = lane_c & (N - 1)
    strict = tcol < row_c
    incl = tcol <= row_c
    lane_lo_n = lax.broadcasted_iota(I32, (N, P), 1) < N
    r2 = lax.broadcasted_iota(I32, (P, P), 0)
    l2 = lax.broadcasted_iota(I32, (P, P), 1)
    bdmask = (r2 < N) == (l2 < N)
    ones_bd = jnp.where(bdmask, 1.0, 0.0).astype(BF16)
    ones2 = jnp.concatenate([ones_bd, ones_bd], axis=0)
    ti = lax.broadcasted_iota(I32, (C, C), 0)
    si = lax.broadcasted_iota(I32, (C, C), 1)
    tri = jnp.where(si <= ti, 1.0, 0.0).astype(BF16)
    tri3 = jnp.concatenate([tri, tri, tri], axis=1)

    def bd(x):
        return jnp.where(bdmask, jnp.concatenate([x, x], axis=0), 0.0).astype(BF16)

    def head_mean(t):
        hi, lo = _split2(t)
        return _dot(jnp.concatenate([hi, lo], axis=1), ones2) * (1.0 / N)

    units = [(bi, p) for bi in range(nb) for p in range(npair)]
    nlev = max(1, (C - 1).bit_length())
    for g0 in range(0, len(units), RWKV_GROUP):
        grp = units[g0:g0 + RWKV_GROUP]
        n = len(grp)
        ld = lambda ref: [ref[bi, :, p * P:(p + 1) * P] for bi, p in grp]
        r, k, v, kk, b, lw = ld(r_ref), ld(k_ref), ld(v_ref), ld(kk_ref), ld(b_ref), ld(lw_ref)

        cum = []
        for u in range(n):
            hi = lw[u].astype(BF16)
            r1 = lw[u] - hi.astype(F32)
            mid = r1.astype(BF16)
            lo = (r1 - mid.astype(F32)).astype(BF16)
            cum.append(_dot(tri3, jnp.concatenate([hi, mid, lo], axis=0)))
        cum_last = [c[C - 1:C, :] for c in cum]
        p_inv = [jnp.exp(-c) for c in cum]
        a_t = [-kk[u] * jnp.exp(cum[u] - lw[u]) for u in range(n)]
        r_t = [r[u] * jnp.exp(cum[u]) for u in range(n)]
        dec = [jnp.exp(cum_last[u] - cum[u]) for u in range(n)]
        ar = [jnp.concatenate([a_t[u], r_t[u]], axis=0).astype(BF16) for u in range(n)]
        sb = [_dot_nt(ar[u], bd(b[u] * p_inv[u])) for u in range(n)]
        sk = [_dot_nt(ar[u], bd(k[u] * p_inv[u])) for u in range(n)]
        bd_v = [bd(v[u]) for u in range(n)]
        lp = [jnp.where(strict, sb[u][:C], 0.0) for u in range(n)]
        l_ak = [jnp.where(strict, sk[u][:C], 0.0).astype(BF16) for u in range(n)]
        m_rb = [jnp.where(incl, sb[u][C:], 0.0).astype(BF16) for u in range(n)]
        m_rk = [jnp.where(incl, sk[u][C:], 0.0).astype(BF16) for u in range(n)]

        xa = list(a_t)
        xu = [_dot(l_ak[u], bd_v[u]) for u in range(n)]
        for lev in range(nlev):
            last = lev == nlev - 1
            for u in range(n):
                parts = [bd(xa[u]), bd(xu[u])] + ([] if last else [bd(lp[u])])
                res = _dot(lp[u].astype(BF16), jnp.concatenate(parts, axis=1))
                xa[u] = xa[u] + res[:, :P]
                xu[u] = xu[u] + res[:, P:2 * P]
                if not last:
                    lp[u] = res[:, 2 * P:]

        res = [_dot(m_rb[u], jnp.concatenate([bd(xa[u]), bd(xu[u])], axis=1)) for u in range(n)]
        r_hat = [r_t[u] + res[u][:, :P] for u in range(n)]
        y0 = [res[u][:, P:] + _dot(m_rk[u], bd_v[u]) for u in range(n)]
        zt = [jnp.concatenate([b[u] * dec[u], k[u] * dec[u]], axis=0).T.astype(BF16) for u in range(n)]
        pct = [jnp.broadcast_to(cum_last[u], (P, P)).T for u in range(n)]
        pcm = [jnp.exp(jnp.where(lane_lo_n, pct[u][:N], pct[u][N:])) for u in range(n)]

        y = []
        for u, (bi, p) in enumerate(grp):
            st = st_sc[bi, p]
            ws = _dot(jnp.concatenate([xa[u], r_hat[u]], axis=0).astype(BF16), bd(st))
            wc = ws[:C] + xu[u]
            y.append(ws[C:] + y0[u])
            full = _dot(zt[u], jnp.concatenate([wc, v[u]], axis=0).astype(BF16))
            st_sc[bi, p] = pcm[u] * st + jnp.where(lane_lo_n, full[:N], full[N:])

        for u, (bi, p) in enumerate(grp):
            sl = slice(p * P, (p + 1) * P)
            yc = y[u] - head_mean(y[u])
            yn = yc * lax.rsqrt(head_mean(yc * yc) + GN_EPS) * lg_ref[:, sl] + lb_ref[:, sl]
            o_ref[bi, :, sl] = ((yn + bonus_ref[bi, :, sl]) * g_ref[bi, :, sl]).astype(o_ref.dtype)


def _rwkv_chunk(r, k, v, kk, bvec, lw, bonus, g, lnx_g, lnx_b):
    b, s, w = r.shape
    C = RWKV_CHUNK
    spec = pl.BlockSpec((b, C, w), lambda j: (0, j, 0))
    pspec = pl.BlockSpec((1, w), lambda j: (0, 0))
    return pl.pallas_call(
        _rwkv_chunk_kernel,
        out_shape=jax.ShapeDtypeStruct((b, s, w), BF16),
        grid=(s // C,),
        in_specs=[spec] * 8 + [pspec, pspec],
        out_specs=spec,
        scratch_shapes=[pltpu.VMEM((b, w // (2 * RWKV_HEAD_DIM), RWKV_HEAD_DIM, 2 * RWKV_HEAD_DIM), F32)],
        compiler_params=_cparams(("arbitrary",)),
        name="rwkv_chunk",
    )(r, k, v, kk, bvec, lw, bonus, g, lnx_g.reshape(1, w), lnx_b.reshape(1, w))


def _mem_attn_kernel(q_ref, kv_ref, o_ref):
    W = MEM_HEADS * MEM_HEAD_DIM
    scale = MEM_HEAD_DIM ** -0.5
    for h in range(MEM_HEADS):
        sl = slice(h * MEM_HEAD_DIM, (h + 1) * MEM_HEAD_DIM)
        q = q_ref[0, :, sl]
        k = kv_ref[0, :, sl]
        v = kv_ref[0, :, W + h * MEM_HEAD_DIM:W + (h + 1) * MEM_HEAD_DIM]
        s = _dot_nt(q, k) * scale
        s = s - jnp.max(s, axis=-1, keepdims=True)
        p = jnp.exp(s)
        p = p / jnp.sum(p, axis=-1, keepdims=True)
        o_ref[0, :, sl] = _dot(p.astype(BF16), v).astype(o_ref.dtype)


def _mem_attn(q, kv, tq):
    b, s, w = q.shape
    m = kv.shape[1]
    return pl.pallas_call(
        _mem_attn_kernel,
        out_shape=jax.ShapeDtypeStruct((b, s, w), BF16),
        grid=(b, s // tq),
        in_specs=[pl.BlockSpec((1, tq, w), lambda i, j: (i, j, 0)),
                  pl.BlockSpec((1, m, 2 * w), lambda i, j: (i, 0, 0))],
        out_specs=pl.BlockSpec((1, tq, w), lambda i, j: (i, j, 0)),
        compiler_params=_cparams(("parallel", "parallel")),
        name="mem_attn",
    )(q, kv)


def _layernorm(z, g, b):
    mu = jnp.mean(z, axis=-1, keepdims=True)
    zc = z - mu
    var = jnp.mean(zc * zc, axis=-1, keepdims=True)
    return zc * lax.rsqrt(var + LN_EPS) * g + b


def _merge_kernel(x_ref, yd_ref, yr_ref, ym_ref, gt_ref, wd_ref, wr_ref, wm_ref, wo_ref,
                  g_ref, b_ref, x1_ref, x1b_ref, x1t_ref, *, alpha):
    d = x_ref.shape[-1]
    br_d = _dot(yd_ref[0], wd_ref[...])
    br_r = _dot(yr_ref[0], wr_ref[...])
    br_m = _dot(ym_ref[0], wm_ref[...])
    gt = gt_ref[0]
    merged = (gt[:, 0:d].astype(F32) * br_d + gt[:, d:2 * d].astype(F32) * br_r
              + gt[:, 2 * d:3 * d].astype(F32) * br_m)
    z = alpha * x_ref[0] + _dot(merged.astype(BF16), wo_ref[...])
    x1 = _layernorm(z, g_ref[...], b_ref[...])
    x1_ref[0] = x1
    x1b_ref[0] = x1.astype(BF16)
    x1t_ref[0] = x1.T.astype(BF16)


def _merge(x, y_dsa, y_rwkv, y_mem, gates, w_br_dsa, w_br_rwkv, w_br_mem, w_out, ln_g, ln_b, alpha, tm):
    b, s, d = x.shape
    full2 = lambda a: pl.BlockSpec(a.shape, lambda i, j: (0,) * a.ndim)
    wd = w_br_dsa.astype(BF16)
    wr = w_br_rwkv.astype(BF16)
    wm = w_br_mem.astype(BF16)
    wo = w_out.astype(BF16)
    g2 = ln_g.reshape(1, d)
    b2 = ln_b.reshape(1, d)
    tok = lambda w: pl.BlockSpec((1, tm, w), lambda i, j: (i, j, 0))
    return pl.pallas_call(
        functools.partial(_merge_kernel, alpha=alpha),
        out_shape=(jax.ShapeDtypeStruct((b, s, d), F32),
                   jax.ShapeDtypeStruct((b, s, d), BF16),
                   jax.ShapeDtypeStruct((b, d, s), BF16)),
        grid=(b, s // tm),
        in_specs=[tok(d), tok(y_dsa.shape[-1]), tok(y_rwkv.shape[-1]), tok(y_mem.shape[-1]), tok(3 * d),
                  full2(wd), full2(wr), full2(wm), full2(wo), full2(g2), full2(b2)],
        out_specs=(tok(d), tok(d), pl.BlockSpec((1, d, tm), lambda i, j: (i, 0, j))),
        compiler_params=_cparams(("parallel", "parallel")),
        name="merge",
    )(x, y_dsa, y_rwkv, y_mem, gates, wd, wr, wm, wo, g2, b2)


def _sort_network(n):
    pairs = []

    def merge(lo, cnt, r):
        step = r * 2
        if step < cnt:
            merge(lo, cnt, step)
            merge(lo + r, cnt, step)
            for i in range(lo + r, lo + cnt - r, step):
                pairs.append((i, i + r))
        else:
            pairs.append((lo, lo + r))

    def sort(lo, cnt):
        if cnt > 1:
            half = cnt // 2
            sort(lo, half)
            sort(lo + half, half)
            merge(lo, cnt, 1)

    sort(0, n)
    return pairs


SUBLANES = 8
PEER_SLABS = PEER_N_KEYS // SUBLANES
PEER_SORT_PAIRS = _sort_network(PEER_SLABS)
PEER_CAND = [(i, j) for i in range(PEER_TOPK) for j in range(PEER_TOPK // (i + 1))]
PEER_CAND += [None] * (-len(PEER_CAND) % SUBLANES)


def _top_sorted(sub, k):
    slabs = [sub[SUBLANES * s:SUBLANES * (s + 1), :] for s in range(PEER_SLABS)]
    for i, j in PEER_SORT_PAIRS:
        slabs[i], slabs[j] = jnp.maximum(slabs[i], slabs[j]), jnp.minimum(slabs[i], slabs[j])
    tops = []
    for r in range(k):
        m = jnp.max(slabs[0], axis=0, keepdims=True)
        tops.append(m)
        if r < k - 1:
            hit = slabs[0] == m
            for i in range(k - 1 - r):
                slabs[i] = jnp.where(hit, slabs[i + 1], slabs[i])
    return tops


def _prefix_count(pred, rows):
    full = pred(rows[15])
    c1 = pred(rows[7])
    c2 = pred(jnp.where(c1, rows[11], rows[3]))
    c3 = pred(jnp.where(c1, jnp.where(c2, rows[13], rows[9]), jnp.where(c2, rows[5], rows[1])))
    c4 = pred(jnp.where(c1,
                        jnp.where(c2, jnp.where(c3, rows[14], rows[12]), jnp.where(c3, rows[10], rows[8])),
                        jnp.where(c2, jnp.where(c3, rows[6], rows[4]), jnp.where(c3, rows[2], rows[0]))))
    cnt = (jnp.where(c1, 8.0, 0.0) + jnp.where(c2, 4.0, 0.0)) + (jnp.where(c3, 2.0, 0.0) + jnp.where(c4, 1.0, 0.0))
    return jnp.where(full, 16.0, cnt)


def _peer_route_kernel(x_ref, wq_ref, keys_ref, r1_o, c1_o, rk2_o, p2_o):
    tt = x_ref.shape[1]
    K = PEER_TOPK
    q = _dot(x_ref[0], wq_ref[...])
    keys0 = keys_ref[0]
    keys1 = keys_ref[1]
    sub_id = lax.broadcasted_iota(I32, (SUBLANES, tt), 0)

    for h in range(PEER_HEADS):
        q1 = q[:, (2 * h) * PEER_HALF:(2 * h + 1) * PEER_HALF]
        q2 = q[:, (2 * h + 1) * PEER_HALF:(2 * h + 2) * PEER_HALF]
        s1 = _dot3(keys0, q1, nt=True)
        s2 = _dot3(keys1, q2, nt=True)
        a1 = _top_sorted(s1, K)
        a2 = _top_sorted(s2, K)
        best = a1[0] + a2[0]
        cand = []
        for g in range(len(PEER_CAND) // SUBLANES):
            slab = jnp.full((SUBLANES, tt), NEG_INF, F32)
            for s, ij in enumerate(PEER_CAND[SUBLANES * g:SUBLANES * (g + 1)]):
                if ij is not None:
                    slab = jnp.where(sub_id == s, a1[ij[0]] + a2[ij[1]], slab)
            cand.append(slab)
        zsum = jnp.zeros((1, tt), F32)
        v_k = best
        for rnk in range(K):
            m = cand[0]
            for slab in cand[1:]:
                m = jnp.maximum(m, slab)
            v_k = jnp.max(m, axis=0, keepdims=True)
            zsum = zsum + jnp.exp(v_k - best)
            if rnk < K - 1:
                cand = [jnp.where(slab == v_k, NEG_INF, slab) for slab in cand]
        r1_o[0, h] = _prefix_count(lambda t: s1 + t >= v_k, a2)
        c1_o[0, h] = jnp.exp(s1 - a1[0]) / zsum
        rk2_o[0, h] = _prefix_count(lambda t: t > s2, a2).astype(BF16)
        p2_o[0, h] = jnp.exp(s2 - a2[0]).astype(BF16)


def _peer_route(x1b, w_peer_q, peer_keys, tt):
    b, s, d = x1b.shape
    nk = PEER_N_KEYS
    wq = w_peer_q.astype(BF16)
    o32 = jax.ShapeDtypeStruct((b, PEER_HEADS, nk, s), F32)
    o16 = jax.ShapeDtypeStruct((b, PEER_HEADS, nk, s), BF16)
    ospec = pl.BlockSpec((1, PEER_HEADS, nk, tt), lambda i, j: (i, 0, 0, j))
    return pl.pallas_call(
        _peer_route_kernel,
        out_shape=(o32, o32, o16, o16),
        grid=(b, s // tt),
        in_specs=[pl.BlockSpec((1, tt, d), lambda i, j: (i, j, 0)),
                  pl.BlockSpec(wq.shape, lambda i, j: (0, 0)),
                  pl.BlockSpec(peer_keys.shape, lambda i, j: (0, 0, 0))],
        out_specs=(ospec,) * 4,
        compiler_params=_cparams(("parallel", "parallel")),
        name="peer_route",
    )(x1b, wq, peer_keys)


PEER_ET = 1024


def _peer_ffn_kernel(x1_ref, x1t_ref, r1_ref, c1_ref, rk2_ref, p2_ref, u_ref, vt_ref,
                     g_ref, b_ref, o_ref, acc_sc, *, alpha):
    e = pl.program_id(2)
    nk = PEER_N_KEYS
    tt = x1t_ref.shape[2]

    @pl.when(e == 0)
    def _():
        acc_sc[...] = jnp.zeros(acc_sc.shape, F32)

    act = _dot(u_ref[...], x1t_ref[0])
    act = 0.5 * act * (1.0 + lax.erf(act * (2.0 ** -0.5)))
    gate_rows = []
    for rr in range(PEER_ET // nk):
        gsum = jnp.zeros((nk, tt), BF16)
        for h in range(PEER_HEADS):
            r1 = r1_ref[0, h, rr:rr + 1, :].astype(BF16)
            c1 = c1_ref[0, h, rr:rr + 1, :].astype(BF16)
            sel = rk2_ref[0, h] < r1
            gsum = gsum + jnp.where(sel, p2_ref[0, h], jnp.zeros((), BF16)) * c1
        gate_rows.append(gsum)
    gate = jnp.concatenate(gate_rows, axis=0)
    w = gate * act.astype(BF16)
    acc_sc[...] += _dot(vt_ref[...], w)

    @pl.when(e == pl.num_programs(2) - 1)
    def _():
        z = alpha * x1_ref[0] + acc_sc[...].T
        o_ref[0] = _layernorm(z, g_ref[...], b_ref[...])


def _peer_ffn(x1, x1t, r1, c1, rk2, p2, peer_u, peer_v, ln_g, ln_b, alpha, tt):
    b, s, d = x1.shape
    ne = peer_u.shape[0]
    nk = PEER_N_KEYS
    rows = PEER_ET // nk
    u = peer_u.astype(BF16)
    vt = peer_v.T.astype(BF16)
    row_spec = pl.BlockSpec((1, PEER_HEADS, rows, tt), lambda i, j, e: (i, 0, e, j))
    key_spec = pl.BlockSpec((1, PEER_HEADS, nk, tt), lambda i, j, e: (i, 0, 0, j))
    return pl.pallas_call(
        functools.partial(_peer_ffn_kernel, alpha=alpha),
        out_shape=jax.ShapeDtypeStruct((b, s, d), F32),
        grid=(b, s // tt, ne // PEER_ET),
        in_specs=[pl.BlockSpec((1, tt, d), lambda i, j, e: (i, j, 0)),
                  pl.BlockSpec((1, d, tt), lambda i, j, e: (i, 0, j)),
                  row_spec, row_spec, key_spec, key_spec,
                  pl.BlockSpec((PEER_ET, d), lambda i, j, e: (e, 0)),
                  pl.BlockSpec((d, PEER_ET), lambda i, j, e: (0, e)),
                  pl.BlockSpec((1, d), lambda i, j, e: (0, 0)),
                  pl.BlockSpec((1, d), lambda i, j, e: (0, 0))],
        out_specs=pl.BlockSpec((1, tt, d), lambda i, j, e: (i, j, 0)),
        scratch_shapes=[pltpu.VMEM((d, tt), F32)],
        compiler_params=_cparams(("parallel", "parallel", "arbitrary")),
        name="peer_ffn",
    )(x1, x1t, r1, c1, rk2, p2, u, vt, ln_g.reshape(1, d), ln_b.reshape(1, d))


def _tile(n, pref):
    t = min(n, pref)
    while n % t:
        t //= 2
    return t


def _layer(x, mem, rel_bias, w_in, q_norm_g, kv_norm_g, w_uq, w_uk, w_uv, w_idx_q,
           rwkv_mu, rwkv_w0, rwkv_w2, rwkv_a0, rwkv_a2, rwkv_g2, rwkv_k_k, rwkv_k_a, rwkv_r_k,
           rwkv_lnx_g, rwkv_lnx_b, w_mem_kv, w_br_dsa, w_br_rwkv, w_br_mem, w_out, ln1_g, ln1_b,
           w_peer_q, peer_keys, peer_u, peer_v, ln2_g, ln2_b, alpha):
    b, s, d = x.shape
    t = b * s
    tm = _tile(s, 512)
    x2 = x.reshape(t, d).astype(BF16)
    n_dsa = DSA_Q_RANK + DSA_KV_RANK + IDX_DIM + IDX_HEADS
    n_rwkv = 3 * RWKV_WIDTH + DECAY_RANK + AAA_RANK + GATE_RANK
    n_mem = MEM_HEADS * MEM_HEAD_DIM
    o1, o2, o3 = n_dsa, n_dsa + n_rwkv, n_dsa + n_rwkv + n_mem
    wb = w_in.astype(BF16)
    tl = _tile(t, 2048)
    dcols = _linear(x2, wb[:, :o1], F32, tl, n_dsa, name="in_dsa")
    rcols = _linear(x2, wb[:, o1:o2], F32, tl, n_rwkv // 2, name="in_rwkv")
    mem_q = _linear(x2, wb[:, o2:o3], BF16, tl, n_mem, name="in_memq")
    gates = _linear(x2, wb[:, o3:], BF16, tl, d, act="sigmoid", name="in_gates")

    q_lat, q_idx, w_idx, k_idx, c_kv = _dsa_prep(dcols.reshape(b, s, n_dsa), q_norm_g, kv_norm_g,
                                                 w_uq, w_uk, w_idx_q, tm)
    y_dsa = _dsa_attn(rel_bias, q_idx, w_idx, q_lat, k_idx, c_kv, w_uv)

    prep = _rwkv_prep(rcols, s, rwkv_mu, rwkv_w0, rwkv_w2, rwkv_a0, rwkv_a2, rwkv_g2,
                      rwkv_k_k, rwkv_k_a, rwkv_r_k, tm)
    y_rwkv = _rwkv_chunk(*[a.reshape(b, s, RWKV_WIDTH) for a in prep], rwkv_lnx_g, rwkv_lnx_b)

    m = mem.shape[1]
    mkv = _linear(mem.reshape(b * m, d), w_mem_kv.astype(BF16), BF16, _tile(b * m, 512), 2 * n_mem,
                  name="mem_kv").reshape(b, m, 2 * n_mem)
    y_mem = _mem_attn(mem_q.reshape(b, s, n_mem), mkv, tm)

    x1, x1b, x1t = _merge(x, y_dsa, y_rwkv, y_mem, gates.reshape(b, s, 3 * d),
                          w_br_dsa, w_br_rwkv, w_br_mem, w_out, ln1_g, ln1_b, alpha, tm)

    r1, c1, rk2, p2 = _peer_route(x1b, w_peer_q, peer_keys, _tile(s, 512))
    return _peer_ffn(x1, x1t, r1, c1, rk2, p2, peer_u, peer_v, ln2_g, ln2_b, alpha, _tile(s, 1024))


def kernel(x, mem, rel_bias, w_in, q_norm_g, kv_norm_g, w_uq, w_uk, w_uv, w_idx_q, rwkv_mu, rwkv_w0, rwkv_w2, rwkv_a0, rwkv_a2, rwkv_g2, rwkv_k_k, rwkv_k_a, rwkv_r_k, rwkv_lnx_g, rwkv_lnx_b, w_mem_kv, w_br_dsa, w_br_rwkv, w_br_mem, w_out, ln1_g, ln1_b, w_peer_q, peer_keys, peer_u, peer_v, ln2_g, ln2_b):
    depth = w_in.shape[0]
    alpha = (2 * depth) ** 0.25
    for l in range(depth):
        x = _layer(x, mem, rel_bias, w_in[l], q_norm_g[l], kv_norm_g[l], w_uq[l], w_uk[l], w_uv[l],
                   w_idx_q[l], rwkv_mu[l], rwkv_w0[l], rwkv_w2[l], rwkv_a0[l], rwkv_a2[l], rwkv_g2[l],
                   rwkv_k_k[l], rwkv_k_a[l], rwkv_r_k[l], rwkv_lnx_g[l], rwkv_lnx_b[l], w_mem_kv[l],
                   w_br_dsa[l], w_br_rwkv[l], w_br_mem[l], w_out[l], ln1_g[l], ln1_b[l],
                   w_peer_q[l], peer_keys[l], peer_u[l], peer_v[l], ln2_g[l], ln2_b[l], alpha)
    return x
```

```python
import functools
import math

import jax
import jax.numpy as jnp
from jax import lax
from jax.experimental import pallas as pl
from jax.experimental.pallas import tpu as pltpu

F32 = jnp.float32
BF16 = jnp.bfloat16
I32 = jnp.int32

DSA_HEADS = 8
DSA_HEAD_DIM = 64
DSA_Q_RANK = 256
DSA_KV_RANK = 128
IDX_HEADS = 8
IDX_DIM = 32
INDEX_TOPK_MAX = 256
NUM_BUCKETS = 32
MAX_DISTANCE = 128
RWKV_HEADS = 8
RWKV_HEAD_DIM = 64
RWKV_WIDTH = RWKV_HEADS * RWKV_HEAD_DIM
DECAY_RANK = 64
AAA_RANK = 64
GATE_RANK = 128
GN_EPS = 64e-5
MEM_HEADS = 4
MEM_HEAD_DIM = 128
PEER_HEADS = 8
PEER_N_KEYS = 128
PEER_HALF = 128
PEER_TOPK = 16
LN_EPS = 1e-5
RMS_EPS = 1e-6

LANES = 128
VMEM_LIMIT = 56 * 1024 * 1024

INT_MIN = -(2 ** 31)
NEG_INF = float("-inf")


def _cparams(sem, vmem=VMEM_LIMIT):
    return pltpu.CompilerParams(dimension_semantics=sem, vmem_limit_bytes=vmem)


def _dot(a, b):
    return jnp.dot(a, b, preferred_element_type=F32)


def _dot_nt(a, b):
    return lax.dot_general(a, b, (((1,), (1,)), ((), ())), preferred_element_type=F32)


def _split2(x):
    hi = x.astype(BF16)
    lo = (x - hi.astype(F32)).astype(BF16)
    return hi, lo


def _dot3(a, b, nt=False):
    d = _dot_nt if nt else _dot
    ah, al = _split2(a)
    bh, bl = _split2(b)
    return d(ah, bh) + (d(ah, bl) + d(al, bh))


def _linear_kernel(x_ref, w_ref, o_ref, *, act):
    y = _dot(x_ref[...].astype(BF16), w_ref[...])
    if act == "sigmoid":
        y = jax.nn.sigmoid(y)
    o_ref[...] = y.astype(o_ref.dtype)


def _linear(x, w, out_dtype, tm, tn, act=None, name="linear"):
    m, k = x.shape
    n = w.shape[1]
    return pl.pallas_call(
        functools.partial(_linear_kernel, act=act),
        out_shape=jax.ShapeDtypeStruct((m, n), out_dtype),
        grid=(m // tm, n // tn),
        in_specs=[pl.BlockSpec((tm, k), lambda i, j: (i, 0)),
                  pl.BlockSpec((k, tn), lambda i, j: (0, j))],
        out_specs=pl.BlockSpec((tm, tn), lambda i, j: (i, j)),
        compiler_params=_cparams(("parallel", "parallel")),
        name=name,
    )(x, w)


def _dsa_prep_kernel(dc_ref, qg_ref, kvg_ref, wuq_ref, wuk_ref, wiq_ref,
                     ql_ref, qi_ref, wi_ref, ki_ref, kv_ref):
    dc = dc_ref[0]
    c_q = dc[:, :DSA_Q_RANK]
    c_kv = dc[:, DSA_Q_RANK:DSA_Q_RANK + DSA_KV_RANK]
    off = DSA_Q_RANK + DSA_KV_RANK
    k_idx = dc[:, off:off + IDX_DIM]
    w_idx = dc[:, off + IDX_DIM:off + IDX_DIM + IDX_HEADS]
    c_q = c_q * lax.rsqrt(jnp.mean(c_q * c_q, axis=-1, keepdims=True) + RMS_EPS) * qg_ref[...]
    c_kv = c_kv * lax.rsqrt(jnp.mean(c_kv * c_kv, axis=-1, keepdims=True) + RMS_EPS) * kvg_ref[...]
    cqb = c_q.astype(BF16)
    scale = DSA_HEAD_DIM ** -0.5
    for h in range(DSA_HEADS):
        q_h = _dot(cqb, wuq_ref[h])
        ql_ref[0, h] = (_dot(q_h.astype(BF16), wuk_ref[h]) * scale).astype(BF16)
        qi_ref[0, h] = _dot(cqb, wiq_ref[h]).astype(BF16)
    wi_ref[0] = w_idx * (IDX_HEADS ** -0.5 * IDX_DIM ** -0.5)
    ki_ref[0] = k_idx.astype(BF16)
    kv_ref[0] = c_kv.astype(BF16)


def _dsa_prep(dcols, q_norm_g, kv_norm_g, w_uq, w_uk, w_idx_q, tm):
    b, s, wd = dcols.shape
    h = DSA_HEADS
    wuq_h = w_uq.reshape(DSA_Q_RANK, h, DSA_HEAD_DIM).transpose(1, 0, 2).astype(BF16)
    wuk_h = w_uk.transpose(1, 2, 0).astype(BF16)
    wiq_h = w_idx_q.reshape(DSA_Q_RANK, IDX_HEADS, IDX_DIM).transpose(1, 0, 2).astype(BF16)
    full = lambda shape: pl.BlockSpec(shape, lambda i, j: (0,) * len(shape))
    return pl.pallas_call(
        _dsa_prep_kernel,
        out_shape=(jax.ShapeDtypeStruct((b, h, s, DSA_KV_RANK), BF16),
                   jax.ShapeDtypeStruct((b, IDX_HEADS, s, IDX_DIM), BF16),
                   jax.ShapeDtypeStruct((b, s, IDX_HEADS), F32),
                   jax.ShapeDtypeStruct((b, s, IDX_DIM), BF16),
                   jax.ShapeDtypeStruct((b, s, DSA_KV_RANK), BF16)),
        grid=(b, s // tm),
        in_specs=[pl.BlockSpec((1, tm, wd), lambda i, j: (i, j, 0)),
                  full((1, DSA_Q_RANK)), full((1, DSA_KV_RANK)),
                  full(wuq_h.shape), full(wuk_h.shape), full(wiq_h.shape)],
        out_specs=(pl.BlockSpec((1, h, tm, DSA_KV_RANK), lambda i, j: (i, 0, j, 0)),
                   pl.BlockSpec((1, IDX_HEADS, tm, IDX_DIM), lambda i, j: (i, 0, j, 0)),
                   pl.BlockSpec((1, tm, IDX_HEADS), lambda i, j: (i, j, 0)),
                   pl.BlockSpec((1, tm, IDX_DIM), lambda i, j: (i, j, 0)),
                   pl.BlockSpec((1, tm, DSA_KV_RANK), lambda i, j: (i, j, 0))),
        compiler_params=_cparams(("parallel", "parallel")),
        name="dsa_prep",
    )(dcols, q_norm_g.reshape(1, -1), kv_norm_g.reshape(1, -1), wuq_h, wuk_h, wiq_h)


DSA_QB = 128
DSA_KB = 128
DSA_SC = 512


def _t5_bias_tile(rel_ref, h, off):
    row = lax.broadcasted_iota(I32, (DSA_QB, DSA_KB), 0)
    col = lax.broadcasted_iota(I32, (DSA_QB, DSA_KB), 1)
    n = jnp.maximum(row - col + off, 0)
    max_exact = NUM_BUCKETS // 2
    nf = jnp.maximum(n, max_exact).astype(F32)
    large = max_exact + jnp.floor(jnp.log(nf / max_exact) / math.log(MAX_DISTANCE / max_exact)
                                  * (NUM_BUCKETS - max_exact)).astype(I32)
    large = jnp.minimum(large, NUM_BUCKETS - 1)
    bucket = jnp.where(n < max_exact, n, large)
    last = rel_ref[NUM_BUCKETS - 1, h]
    out = jnp.zeros((DSA_QB, DSA_KB), F32)
    for bk in range(NUM_BUCKETS - 1):
        out = jnp.where(bucket == bk, rel_ref[bk, h] - last, out)
    return out


def _fold_lanes(x, op):
    out = x[..., :LANES]
    for c in range(1, x.shape[-1] // LANES):
        out = op(out, x[..., c * LANES:(c + 1) * LANES])
    return out


def _dsa_attn_kernel(rel_ref, qi_ref, wi_ref, ql_ref, ki_ref, kv_ref, wuv_ref, o_ref,
                     key_sc, bias_sc, mx_sc, ref_sc, ls_sc, acc_sc, *, topk, seq):
    H = DSA_HEADS
    QB, KB, SC = DSA_QB, DSA_KB, DSA_SC
    qb = pl.program_id(1)
    q0 = qb * QB

    @pl.when((pl.program_id(0) == 0) & (qb == 0))
    def _():
        for h in range(H):
            bias_sc[0, h] = _t5_bias_tile(rel_ref, h, 0)
            bias_sc[1, h] = _t5_bias_tile(rel_ref, h, KB)

    n_sc = (q0 + QB + SC - 1) // SC
    wi = wi_ref[0]
    HALF = SC // 2
    row_g = q0 + lax.broadcasted_iota(I32, (QB, HALF), 0)
    col_h = lax.broadcasted_iota(I32, (QB, HALF), 1)

    def score_body(j, carry):
        for c in range(2):
            k0 = pl.multiple_of(j * SC + c * HALF, HALF)
            ks = ki_ref[0, pl.ds(k0, HALF), :]
            acc = jnp.zeros((QB, HALF), F32)
            for h in range(IDX_HEADS):
                lg = _dot_nt(qi_ref[0, h], ks)
                acc = acc + wi[:, h:h + 1] * jnp.maximum(lg, 0.0)
            key_sc[:, pl.ds(k0, HALF)] = jnp.where(col_h + k0 <= row_g, acc, NEG_INF)
        return carry

    lax.fori_loop(0, n_sc, score_body, 0)

    col_l = lax.broadcasted_iota(I32, (QB, LANES), 1)

    def count(ind):
        def body(j, acc):
            for c in range(SC // LANES):
                k0 = pl.multiple_of(j * SC + c * LANES, LANES)
                acc = acc + ind(key_sc[:, pl.ds(k0, LANES)], k0)
            return acc
        acc = lax.fori_loop(0, n_sc, body, jnp.zeros((QB, LANES), I32))
        return jnp.sum(acc, axis=1, keepdims=True)

    def code_to_float(ru):
        c = ru ^ INT_MIN
        return pltpu.bitcast(c ^ ((c >> 31) & 0x7FFFFFFF), jnp.float32)

    HB = QB // 2

    def descent(n):
        def count_half(r0, ru, bit):
            cand = code_to_float(ru | bit)
            acc = jnp.zeros((HB, LANES), I32)
            for c in range(n * SC // LANES):
                acc = acc + jnp.where(key_sc[r0:r0 + HB, c * LANES:(c + 1) * LANES] >= cand, 1, 0)
            return acc

        def update(ru, acc, bit):
            cnt = jnp.sum(acc, axis=1, keepdims=True)
            return jnp.where(cnt >= topk, ru | bit, ru)

        def run():
            def bis_body(i, carry):
                ru_a, ru_b, acc_b = carry
                sh = lax.convert_element_type(31 - i, I32)
                bit = lax.shift_left(jnp.int32(1), sh)
                bit_b = jnp.where(sh == 31, 0, lax.shift_left(jnp.int32(2), jnp.minimum(sh, 30)))
                ru_b = update(ru_b, acc_b, bit_b)
                ru_a = update(ru_a, count_half(0, ru_a, bit), bit)
                return ru_a, ru_b, count_half(HB, ru_b, bit)

            zero = jnp.minimum(lax.broadcasted_iota(I32, (HB, LANES), 0), 0)
            ru_a, ru_b, acc_b = lax.fori_loop(0, 32, bis_body, (zero, zero, zero))
            return jnp.concatenate([ru_a, update(ru_b, acc_b, 1)], axis=0)
        return run

    skip = lambda: jnp.zeros((QB, LANES), I32)
    ru = lax.switch(jnp.where(q0 + QB <= topk, 0, n_sc),
                    [skip] + [descent(n) for n in range(1, seq // SC + 1)])[:, :1]
    few = q0 + lax.broadcasted_iota(I32, (QB, 1), 0) < topk
    tau = jnp.where(few, jnp.finfo(jnp.float32).min, code_to_float(ru))
    tau_b = jnp.broadcast_to(tau, (QB, LANES))
    cnt_ge = count(lambda blk, k0: jnp.where(blk >= tau_b, 1, 0))

    @pl.when(jnp.max(cnt_ge) > topk)
    def _():
        cnt_gt = count(lambda blk, k0: jnp.where(blk > tau_b, 1, 0))
        need = topk - cnt_gt

        def pos_body(_, lohi):
            lo, hi = lohi
            mid = (lo + hi) >> 1
            mid_b = jnp.broadcast_to(mid, (QB, LANES))
            f = count(lambda blk, k0: jnp.where(blk == tau_b, jnp.where(col_l + k0 <= mid_b, 1, 0), 0))
            ok = f >= need
            return jnp.where(ok, lo, mid + 1), jnp.where(ok, mid, hi)

        lo, _hi = lax.fori_loop(0, max(1, (seq - 1).bit_length()), pos_body,
                                (jnp.zeros((QB, 1), I32), jnp.full((QB, 1), seq - 1, I32)))
        jstar = jnp.broadcast_to(lo, (QB, LANES))

        def fix_body(j, carry):
            for c in range(SC // LANES):
                k0 = pl.multiple_of(j * SC + c * LANES, LANES)
                blk = key_sc[:, pl.ds(k0, LANES)]
                dropped = jnp.where(col_l + k0 > jstar, NEG_INF, blk)
                key_sc[:, pl.ds(k0, LANES)] = jnp.where(blk == tau_b, dropped, blk)
            return carry

        lax.fori_loop(0, n_sc, fix_body, 0)

    ql = ql_ref[0].reshape(H * QB, DSA_KV_RANK)
    near0 = jnp.maximum(qb - 1, 0) * KB
    n_far = (near0 + SC - 1) // SC
    tau_s = jnp.broadcast_to(tau, (QB, SC))
    tau_k = jnp.broadcast_to(tau, (QB, KB))
    col_s = lax.broadcasted_iota(I32, (QB, SC), 1)

    def far_logits(j):
        k0 = pl.multiple_of(j * SC, SC)
        kvb = kv_ref[0, pl.ds(k0, SC), :]
        blk = key_sc[:, pl.ds(k0, SC)]
        madd = jnp.where(blk >= tau_s, jnp.where(col_s + k0 < near0, 0.0, NEG_INF), NEG_INF)
        return _dot_nt(ql, kvb).reshape(H, QB, SC) + madd[None], kvb

    def near_logits(jb, bias):
        k0 = pl.multiple_of(jb * KB, KB)
        kvb = kv_ref[0, pl.ds(k0, KB), :]
        madd = jnp.where(key_sc[:, pl.ds(k0, KB)] >= tau_k, 0.0, NEG_INF)
        return _dot_nt(ql, kvb).reshape(H, QB, KB) + (bias + madd[None]), kvb

    SHIFT_SLACK = 60.0
    row_k = lax.broadcasted_iota(I32, (QB, KB), 0)
    col_k = lax.broadcasted_iota(I32, (QB, KB), 1)
    causal_add = jnp.where(col_k <= row_k, 0.0, NEG_INF)

    def near_raw(jb, bias):
        k0 = pl.multiple_of(jb * KB, KB)
        return _dot_nt(ql, kv_ref[0, pl.ds(k0, KB), :]).reshape(H, QB, KB) + bias

    ref_sc[...] = near_raw(qb, bias_sc[0]) + causal_add[None]

    @pl.when(qb >= 1)
    def _():
        ref_sc[...] = jnp.maximum(ref_sc[...], near_raw(qb - 1, bias_sc[1]))

    ref_sc[...] = jnp.broadcast_to(jnp.max(ref_sc[...], axis=-1, keepdims=True), ref_sc.shape)

    def run_pass(track_max):
        ls_sc[...] = jnp.zeros(ls_sc.shape, F32)
        acc_sc[...] = jnp.zeros(acc_sc.shape, F32)
        if track_max:
            mx_sc[...] = jnp.full(mx_sc.shape, NEG_INF, F32)

        def far_acc(j, carry):
            s, kvb = far_logits(j)
            if track_max:
                mx_sc[...] = jnp.maximum(mx_sc[...], _fold_lanes(s, jnp.maximum))
            m_b = ref_sc[...]
            p = jnp.concatenate([jnp.exp(s[..., c * LANES:(c + 1) * LANES] - m_b)
                                 for c in range(SC // LANES)], axis=-1)
            ls_sc[...] += _fold_lanes(p, jnp.add)
            acc_sc[...] += _dot(p.reshape(H * QB, SC).astype(BF16), kvb).reshape(H, QB, DSA_KV_RANK)
            return carry

        lax.fori_loop(0, n_far, far_acc, 0)

        def near_acc(jb, bias):
            s, kvb = near_logits(jb, bias)
            if track_max:
                mx_sc[...] = jnp.maximum(mx_sc[...], s)
            p = jnp.exp(s - ref_sc[...])
            ls_sc[...] += p
            acc_sc[...] += _dot(p.reshape(H * QB, KB).astype(BF16), kvb).reshape(H, QB, DSA_KV_RANK)

        @pl.when(qb >= 1)
        def _():
            near_acc(qb - 1, bias_sc[1])

        near_acc(qb, bias_sc[0])

    run_pass(True)
    mx = jnp.max(mx_sc[...], axis=-1, keepdims=True)

    @pl.when(jnp.max(jnp.abs(mx - ref_sc[:, :, 0:1])) > SHIFT_SLACK)
    def _():
        ref_sc[...] = jnp.broadcast_to(mx, ref_sc.shape)
        run_pass(False)

    o_lat = acc_sc[...] / jnp.sum(ls_sc[...], axis=-1, keepdims=True)
    y = jnp.zeros((QB, DSA_HEADS * DSA_HEAD_DIM), F32)
    for h in range(H):
        y = y + _dot(o_lat[h].astype(BF16), wuv_ref[h])
    o_ref[0] = y.astype(o_ref.dtype)


def _dsa_attn(rel_bias, q_idx, w_idx, q_lat, k_idx, c_kv, w_uv):
    b, h, s, _ = q_lat.shape
    topk = min(INDEX_TOPK_MAX, s // 4)
    width = DSA_HEADS * DSA_HEAD_DIM
    eye = jnp.eye(DSA_HEADS, dtype=F32)
    wuv_e = jnp.einsum("chd,hg->hcgd", w_uv, eye).reshape(DSA_HEADS, DSA_KV_RANK, width).astype(BF16)
    QB = DSA_QB
    assert s % DSA_SC == 0
    return pl.pallas_call(
        functools.partial(_dsa_attn_kernel, topk=topk, seq=s),
        out_shape=jax.ShapeDtypeStruct((b, s, width), BF16),
        grid=(b, s // QB),
        in_specs=[pl.BlockSpec(memory_space=pltpu.SMEM),
                  pl.BlockSpec((1, IDX_HEADS, QB, IDX_DIM), lambda i, j: (i, 0, j, 0)),
                  pl.BlockSpec((1, QB, IDX_HEADS), lambda i, j: (i, j, 0)),
                  pl.BlockSpec((1, h, QB, DSA_KV_RANK), lambda i, j: (i, 0, j, 0)),
                  pl.BlockSpec((1, s, IDX_DIM), lambda i, j: (i, 0, 0)),
                  pl.BlockSpec((1, s, DSA_KV_RANK), lambda i, j: (i, 0, 0)),
                  pl.BlockSpec(wuv_e.shape, lambda i, j: (0, 0, 0))],
        out_specs=pl.BlockSpec((1, QB, width), lambda i, j: (i, j, 0)),
        scratch_shapes=[pltpu.VMEM((QB, s), F32),
                        pltpu.VMEM((2, h, QB, DSA_KB), F32),
                        pltpu.VMEM((h, QB, LANES), F32),
                        pltpu.VMEM((h, QB, LANES), F32),
                        pltpu.VMEM((h, QB, LANES), F32),
                        pltpu.VMEM((h, QB, DSA_KV_RANK), F32)],
        compiler_params=_cparams(("arbitrary", "arbitrary")),
        name="dsa_attn",
    )(rel_bias, q_idx, w_idx, q_lat, k_idx, c_kv, wuv_e)


def _rwkv_prep_kernel(c_ref, p_ref, mu_ref, w0_ref, a0_ref, kk_ref, ka_ref, rk_ref,
                      wwa_ref, g2_ref, ones_ref,
                      r_o, k_o, v_o, kk_o, b_o, lw_o, bonus_o, g_o, *, tiles_per_seq):
    W = RWKV_WIDTH
    cols = c_ref[...]
    first = (pl.program_id(0) % tiles_per_seq) == 0
    prev_row = jnp.where(first, 0.0, p_ref[7:8, :])
    rolled = pltpu.roll(cols, 1, 0)
    row = lax.broadcasted_iota(I32, cols.shape, 0)
    prev = jnp.where(row == 0, prev_row, rolled)
    xs = cols + (prev - cols) * mu_ref[...]
    r = xs[:, 0:W]
    k = xs[:, W:2 * W]
    v = xs[:, 2 * W:3 * W]
    lora = xs[:, 3 * W:3 * W + DECAY_RANK + AAA_RANK]
    gl = xs[:, 3 * W + DECAY_RANK + AAA_RANK:]
    lane = lax.broadcasted_iota(I32, lora.shape, 1)
    lora = jnp.where(lane < DECAY_RANK, jnp.tanh(lora), lora)
    wa = _dot(lora.astype(BF16), wwa_ref[...])
    w = -jax.nn.softplus(-(w0_ref[...] + wa[:, :W])) - 0.5
    a = jax.nn.sigmoid(a0_ref[...] + wa[:, W:])
    ones_bd = ones_ref[...]

    def head_sum(t):
        hi, lo = _split2(t)
        return _dot(hi, ones_bd) + _dot(lo, ones_bd)

    kk = k * kk_ref[...]
    kk = kk / jnp.maximum(jnp.sqrt(head_sum(kk * kk)), 1e-12)
    k2 = k * (1.0 + (a - 1.0) * ka_ref[...])
    r_o[...] = r
    k_o[...] = k2
    v_o[...] = v
    kk_o[...] = kk
    b_o[...] = kk * a
    lw_o[...] = -jnp.exp(w)
    bonus_o[...] = head_sum(r * k2 * rk_ref[...]) * v
    g_o[...] = _dot(jax.nn.sigmoid(gl).astype(BF16), g2_ref[...])


def _rwkv_prep(cols, s, mu, w0, w2, a0, a2, g2, k_k, k_a, r_k, tm):
    t, wc = cols.shape
    W = RWKV_WIDTH
    wwa = jnp.zeros((DECAY_RANK + AAA_RANK, 2 * W), F32)
    wwa = wwa.at[:DECAY_RANK, :W].set(w2).at[DECAY_RANK:, W:].set(a2).astype(BF16)
    head_id = jnp.arange(W) // RWKV_HEAD_DIM
    ones_bd = (head_id[:, None] == head_id[None, :]).astype(BF16)
    row = lambda a: a.reshape(1, -1)
    vec = lambda n: pl.BlockSpec((1, n), lambda i: (0, 0))
    tok = jax.ShapeDtypeStruct((t, W), F32)
    tok_spec = pl.BlockSpec((tm, W), lambda i: (i, 0))
    return pl.pallas_call(
        functools.partial(_rwkv_prep_kernel, tiles_per_seq=s // tm),
        out_shape=(tok,) * 8,
        grid=(t // tm,),
        in_specs=[pl.BlockSpec((tm, wc), lambda i: (i, 0)),
                  pl.BlockSpec((8, wc), lambda i: (jnp.maximum(i * (tm // 8) - 1, 0), 0)),
                  vec(wc), vec(W), vec(W), vec(W), vec(W), vec(W),
                  pl.BlockSpec(wwa.shape, lambda i: (0, 0)),
                  pl.BlockSpec((GATE_RANK, W), lambda i: (0, 0)),
                  pl.BlockSpec((W, W), lambda i: (0, 0))],
        out_specs=(tok_spec,) * 8,
        compiler_params=_cparams(("parallel",)),
        name="rwkv_prep",
    )(cols, cols, row(mu), row(w0), row(a0), row(k_k), row(k_a), row(r_k.reshape(-1)),
      wwa, g2.astype(BF16), ones_bd)


RWKV_CHUNK = 64
RWKV_GROUP = 8


def _rwkv_chunk_kernel(r_ref, k_ref, v_ref, kk_ref, b_ref, lw_ref, bonus_ref, g_ref,
                       lg_ref, lb_ref, o_ref, st_sc):
    C = RWKV_CHUNK
    N = RWKV_HEAD_DIM
    P = 2 * N
    nb = r_ref.shape[0]
    npair = RWKV_WIDTH // P

    @pl.when(pl.program_id(0) == 0)
    def _():
        st_sc[...] = jnp.zeros(st_sc.shape, F32)

    row_c = lax.broadcasted_iota(I32, (C, P), 0)
    lane_c = lax.broadcasted_iota(I32, (C, P), 1)
    tcol = lane_c & (N - 1)
    strict = tcol < row_c
    incl = tcol <= row_c
    lane_lo_n = lax.broadcasted_iota(I32, (N, P), 1) < N
    r2 = lax.broadcasted_iota(I32, (P, P), 0)
    l2 = lax.broadcasted_iota(I32, (P, P), 1)
    bdmask = (r2 < N) == (l2 < N)
    ones_bd = jnp.where(bdmask, 1.0, 0.0).astype(BF16)
    ones2 = jnp.concatenate([ones_bd, ones_bd], axis=0)
    ti = lax.broadcasted_iota(I32, (C, C), 0)
    si = lax.broadcasted_iota(I32, (C, C), 1)
    tri = jnp.where(si <= ti, 1.0, 0.0).astype(BF16)
    tri3 = jnp.concatenate([tri, tri, tri], axis=1)

    def bd(x):
        return jnp.where(bdmask, jnp.concatenate([x, x], axis=0), 0.0).astype(BF16)

    def head_mean(t):
        hi, lo = _split2(t)
        return _dot(jnp.concatenate([hi, lo], axis=1), ones2) * (1.0 / N)

    units = [(bi, p) for bi in range(nb) for p in range(npair)]
    nlev = max(1, (C - 1).bit_length())
    for g0 in range(0, len(units), RWKV_GROUP):
        grp = units[g0:g0 + RWKV_GROUP]
        n = len(grp)
        ld = lambda ref: [ref[bi, :, p * P:(p + 1) * P] for bi, p in grp]
        r, k, v, kk, b, lw = ld(r_ref), ld(k_ref), ld(v_ref), ld(kk_ref), ld(b_ref), ld(lw_ref)

        cum = []
        for u in range(n):
            hi = lw[u].astype(BF16)
            r1 = lw[u] - hi.astype(F32)
            mid = r1.astype(BF16)
            lo = (r1 - mid.astype(F32)).astype(BF16)
            cum.append(_dot(tri3, jnp.concatenate([hi, mid, lo], axis=0)))
        cum_last = [c[C - 1:C, :] for c in cum]
        p_inv = [jnp.exp(-c) for c in cum]
        a_t = [-kk[u] * jnp.exp(cum[u] - lw[u]) for u in range(n)]
        r_t = [r[u] * jnp.exp(cum[u]) for u in range(n)]
        dec = [jnp.exp(cum_last[u] - cum[u]) for u in range(n)]
        ar = [jnp.concatenate([a_t[u], r_t[u]], axis=0).astype(BF16) for u in range(n)]
        sb = [_dot_nt(ar[u], bd(b[u] * p_inv[u])) for u in range(n)]
        sk = [_dot_nt(ar[u], bd(k[u] * p_inv[u])) for u in range(n)]
        bd_v = [bd(v[u]) for u in range(n)]
        lp = [jnp.where(strict, sb[u][:C], 0.0) for u in range(n)]
        l_ak = [jnp.where(strict, sk[u][:C], 0.0).astype(BF16) for u in range(n)]
        m_rb = [jnp.where(incl, sb[u][C:], 0.0).astype(BF16) for u in range(n)]
        m_rk = [jnp.where(incl, sk[u][C:], 0.0).astype(BF16) for u in range(n)]

        xa = list(a_t)
        xu = [_dot(l_ak[u], bd_v[u]) for u in range(n)]
        for lev in range(nlev):
            last = lev == nlev - 1
            for u in range(n):
                parts = [bd(xa[u]), bd(xu[u])] + ([] if last else [bd(lp[u])])
                res = _dot(lp[u].astype(BF16), jnp.concatenate(parts, axis=1))
                xa[u] = xa[u] + res[:, :P]
                xu[u] = xu[u] + res[:, P:2 * P]
                if not last:
                    lp[u] = res[:, 2 * P:]

        res = [_dot(m_rb[u], jnp.concatenate([bd(xa[u]), bd(xu[u])], axis=1)) for u in range(n)]
        r_hat = [r_t[u] + res[u][:, :P] for u in range(n)]
        y0 = [res[u][:, P:] + _dot(m_rk[u], bd_v[u]) for u in range(n)]
        zt = [jnp.concatenate([b[u] * dec[u], k[u] * dec[u]], axis=0).T.astype(BF16) for u in range(n)]
        pct = [jnp.broadcast_to(cum_last[u], (P, P)).T for u in range(n)]
        pcm = [jnp.exp(jnp.where(lane_lo_n, pct[u][:N], pct[u][N:])) for u in range(n)]

        y = []
        for u, (bi, p) in enumerate(grp):
            st = st_sc[bi, p]
            ws = _dot(jnp.concatenate([xa[u], r_hat[u]], axis=0).astype(BF16), bd(st))
            wc = ws[:C] + xu[u]
            y.append(ws[C:] + y0[u])
            full = _dot(zt[u], jnp.concatenate([wc, v[u]], axis=0).astype(BF16))
            st_sc[bi, p] = pcm[u] * st + jnp.where(lane_lo_n, full[:N], full[N:])

        for u, (bi, p) in enumerate(grp):
            sl = slice(p * P, (p + 1) * P)
            yc = y[u] - head_mean(y[u])
            yn = yc * lax.rsqrt(head_mean(yc * yc) + GN_EPS) * lg_ref[:, sl] + lb_ref[:, sl]
            o_ref[bi, :, sl] = ((yn + bonus_ref[bi, :, sl]) * g_ref[bi, :, sl]).astype(o_ref.dtype)


def _rwkv_chunk(r, k, v, kk, bvec, lw, bonus, g, lnx_g, lnx_b):
    b, s, w = r.shape
    C = RWKV_CHUNK
    spec = pl.BlockSpec((b, C, w), lambda j: (0, j, 0))
    pspec = pl.BlockSpec((1, w), lambda j: (0, 0))
    return pl.pallas_call(
        _rwkv_chunk_kernel,
        out_shape=jax.ShapeDtypeStruct((b, s, w), BF16),
        grid=(s // C,),
        in_specs=[spec] * 8 + [pspec, pspec],
        out_specs=spec,
        scratch_shapes=[pltpu.VMEM((b, w // (2 * RWKV_HEAD_DIM), RWKV_HEAD_DIM, 2 * RWKV_HEAD_DIM), F32)],
        compiler_params=_cparams(("arbitrary",)),
        name="rwkv_chunk",
    )(r, k, v, kk, bvec, lw, bonus, g, lnx_g.reshape(1, w), lnx_b.reshape(1, w))


def _mem_attn_kernel(q_ref, kv_ref, o_ref):
    W = MEM_HEADS * MEM_HEAD_DIM
    scale = MEM_HEAD_DIM ** -0.5
    for h in range(MEM_HEADS):
        sl = slice(h * MEM_HEAD_DIM, (h + 1) * MEM_HEAD_DIM)
        q = q_ref[0, :, sl]
        k = kv_ref[0, :, sl]
        v = kv_ref[0, :, W + h * MEM_HEAD_DIM:W + (h + 1) * MEM_HEAD_DIM]
        s = _dot_nt(q, k) * scale
        s = s - jnp.max(s, axis=-1, keepdims=True)
        p = jnp.exp(s)
        p = p / jnp.sum(p, axis=-1, keepdims=True)
        o_ref[0, :, sl] = _dot(p.astype(BF16), v).astype(o_ref.dtype)


def _mem_attn(q, kv, tq):
    b, s, w = q.shape
    m = kv.shape[1]
    return pl.pallas_call(
        _mem_attn_kernel,
        out_shape=jax.ShapeDtypeStruct((b, s, w), BF16),
        grid=(b, s // tq),
        in_specs=[pl.BlockSpec((1, tq, w), lambda i, j: (i, j, 0)),
                  pl.BlockSpec((1, m, 2 * w), lambda i, j: (i, 0, 0))],
        out_specs=pl.BlockSpec((1, tq, w), lambda i, j: (i, j, 0)),
        compiler_params=_cparams(("parallel", "parallel")),
        name="mem_attn",
    )(q, kv)


def _layernorm(z, g, b):
    mu = jnp.mean(z, axis=-1, keepdims=True)
    zc = z - mu
    var = jnp.mean(zc * zc, axis=-1, keepdims=True)
    return zc * lax.rsqrt(var + LN_EPS) * g + b


def _merge_kernel(x_ref, yd_ref, yr_ref, ym_ref, gt_ref, wd_ref, wr_ref, wm_ref, wo_ref,
                  g_ref, b_ref, x1_ref, x1b_ref, x1t_ref, *, alpha):
    d = x_ref.shape[-1]
    br_d = _dot(yd_ref[0], wd_ref[...])
    br_r = _dot(yr_ref[0], wr_ref[...])
    br_m = _dot(ym_ref[0], wm_ref[...])
    gt = gt_ref[0]
    merged = (gt[:, 0:d].astype(F32) * br_d + gt[:, d:2 * d].astype(F32) * br_r
              + gt[:, 2 * d:3 * d].astype(F32) * br_m)
    z = alpha * x_ref[0] + _dot(merged.astype(BF16), wo_ref[...])
    x1 = _layernorm(z, g_ref[...], b_ref[...])
    x1_ref[0] = x1
    x1b_ref[0] = x1.astype(BF16)
    x1t_ref[0] = x1.T.astype(BF16)


def _merge(x, y_dsa, y_rwkv, y_mem, gates, w_br_dsa, w_br_rwkv, w_br_mem, w_out, ln_g, ln_b, alpha, tm):
    b, s, d = x.shape
    full2 = lambda a: pl.BlockSpec(a.shape, lambda i, j: (0,) * a.ndim)
    wd = w_br_dsa.astype(BF16)
    wr = w_br_rwkv.astype(BF16)
    wm = w_br_mem.astype(BF16)
    wo = w_out.astype(BF16)
    g2 = ln_g.reshape(1, d)
    b2 = ln_b.reshape(1, d)
    tok = lambda w: pl.BlockSpec((1, tm, w), lambda i, j: (i, j, 0))
    return pl.pallas_call(
        functools.partial(_merge_kernel, alpha=alpha),
        out_shape=(jax.ShapeDtypeStruct((b, s, d), F32),
                   jax.ShapeDtypeStruct((b, s, d), BF16),
                   jax.ShapeDtypeStruct((b, d, s), BF16)),
        grid=(b, s // tm),
        in_specs=[tok(d), tok(y_dsa.shape[-1]), tok(y_rwkv.shape[-1]), tok(y_mem.shape[-1]), tok(3 * d),
                  full2(wd), full2(wr), full2(wm), full2(wo), full2(g2), full2(b2)],
        out_specs=(tok(d), tok(d), pl.BlockSpec((1, d, tm), lambda i, j: (i, 0, j))),
        compiler_params=_cparams(("parallel", "parallel")),
        name="merge",
    )(x, y_dsa, y_rwkv, y_mem, gates, wd, wr, wm, wo, g2, b2)


def _sort_network(n):
    pairs = []

    def merge(lo, cnt, r):
        step = r * 2
        if step < cnt:
            merge(lo, cnt, step)
            merge(lo + r, cnt, step)
            for i in range(lo + r, lo + cnt - r, step):
                pairs.append((i, i + r))
        else:
            pairs.append((lo, lo + r))

    def sort(lo, cnt):
        if cnt > 1:
            half = cnt // 2
            sort(lo, half)
            sort(lo + half, half)
            merge(lo, cnt, 1)

    sort(0, n)
    return pairs


SUBLANES = 8
PEER_SLABS = PEER_N_KEYS // SUBLANES
PEER_SORT_PAIRS = _sort_network(PEER_SLABS)
PEER_CAND = [(i, j) for i in range(PEER_TOPK) for j in range(PEER_TOPK // (i + 1))]
PEER_CAND += [None] * (-len(PEER_CAND) % SUBLANES)


def _top_sorted(sub, k):
    slabs = [sub[SUBLANES * s:SUBLANES * (s + 1), :] for s in range(PEER_SLABS)]
    for i, j in PEER_SORT_PAIRS:
        slabs[i], slabs[j] = jnp.maximum(slabs[i], slabs[j]), jnp.minimum(slabs[i], slabs[j])
    tops = []
    for r in range(k):
        m = jnp.max(slabs[0], axis=0, keepdims=True)
        tops.append(m)
        if r < k - 1:
            hit = slabs[0] == m
            for i in range(k - 1 - r):
                slabs[i] = jnp.where(hit, slabs[i + 1], slabs[i])
    return tops


def _prefix_count(pred, rows):
    full = pred(rows[15])
    c1 = pred(rows[7])
    c2 = pred(jnp.where(c1, rows[11], rows[3]))
    c3 = pred(jnp.where(c1, jnp.where(c2, rows[13], rows[9]), jnp.where(c2, rows[5], rows[1])))
    c4 = pred(jnp.where(c1,
                        jnp.where(c2, jnp.where(c3, rows[14], rows[12]), jnp.where(c3, rows[10], rows[8])),
                        jnp.where(c2, jnp.where(c3, rows[6], rows[4]), jnp.where(c3, rows[2], rows[0]))))
    cnt = (jnp.where(c1, 8.0, 0.0) + jnp.where(c2, 4.0, 0.0)) + (jnp.where(c3, 2.0, 0.0) + jnp.where(c4, 1.0, 0.0))
    return jnp.where(full, 16.0, cnt)


def _peer_route_kernel(x_ref, wq_ref, keys_ref, r1_o, c1_o, rk2_o, p2_o):
    tt = x_ref.shape[1]
    K = PEER_TOPK
    q = _dot(x_ref[0], wq_ref[...])
    keys0 = keys_ref[0]
    keys1 = keys_ref[1]
    sub_id = lax.broadcasted_iota(I32, (SUBLANES, tt), 0)

    for h in range(PEER_HEADS):
        q1 = q[:, (2 * h) * PEER_HALF:(2 * h + 1) * PEER_HALF]
        q2 = q[:, (2 * h + 1) * PEER_HALF:(2 * h + 2) * PEER_HALF]
        s1 = _dot3(keys0, q1, nt=True)
        s2 = _dot3(keys1, q2, nt=True)
        a1 = _top_sorted(s1, K)
        a2 = _top_sorted(s2, K)
        best = a1[0] + a2[0]
        cand = []
        for g in range(len(PEER_CAND) // SUBLANES):
            slab = jnp.full((SUBLANES, tt), NEG_INF, F32)
            for s, ij in enumerate(PEER_CAND[SUBLANES * g:SUBLANES * (g + 1)]):
                if ij is not None:
                    slab = jnp.where(sub_id == s, a1[ij[0]] + a2[ij[1]], slab)
            cand.append(slab)
        zsum = jnp.zeros((1, tt), F32)
        v_k = best
        for rnk in range(K):
            m = cand[0]
            for slab in cand[1:]:
                m = jnp.maximum(m, slab)
            v_k = jnp.max(m, axis=0, keepdims=True)
            zsum = zsum + jnp.exp(v_k - best)
            if rnk < K - 1:
                cand = [jnp.where(slab == v_k, NEG_INF, slab) for slab in cand]
        r1_o[0, h] = _prefix_count(lambda t: s1 + t >= v_k, a2)
        c1_o[0, h] = jnp.exp(s1 - a1[0]) / zsum
        rk2_o[0, h] = _prefix_count(lambda t: t > s2, a2).astype(BF16)
        p2_o[0, h] = jnp.exp(s2 - a2[0]).astype(BF16)


def _peer_route(x1b, w_peer_q, peer_keys, tt):
    b, s, d = x1b.shape
    nk = PEER_N_KEYS
    wq = w_peer_q.astype(BF16)
    o32 = jax.ShapeDtypeStruct((b, PEER_HEADS, nk, s), F32)
    o16 = jax.ShapeDtypeStruct((b, PEER_HEADS, nk, s), BF16)
    ospec = pl.BlockSpec((1, PEER_HEADS, nk, tt), lambda i, j: (i, 0, 0, j))
    return pl.pallas_call(
        _peer_route_kernel,
        out_shape=(o32, o32, o16, o16),
        grid=(b, s // tt),
        in_specs=[pl.BlockSpec((1, tt, d), lambda i, j: (i, j, 0)),
                  pl.BlockSpec(wq.shape, lambda i, j: (0, 0)),
                  pl.BlockSpec(peer_keys.shape, lambda i, j: (0, 0, 0))],
        out_specs=(ospec,) * 4,
        compiler_params=_cparams(("parallel", "parallel")),
        name="peer_route",
    )(x1b, wq, peer_keys)


PEER_ET = 1024


def _peer_ffn_kernel(x1_ref, x1t_ref, r1_ref, c1_ref, rk2_ref, p2_ref, u_ref, vt_ref,
                     g_ref, b_ref, o_ref, acc_sc, w_sc, rk_sc, p2_sc, *, alpha):
    e = pl.program_id(2)
    nk = PEER_N_KEYS
    tt = x1t_ref.shape[2]

    @pl.when(e == 0)
    def _():
        acc_sc[...] = jnp.zeros(acc_sc.shape, F32)
        rk_sc[...] = rk2_ref[0]
        p2_sc[...] = p2_ref[0]

    act = _dot(u_ref[...], x1t_ref[0])
    w_sc[...] = (0.5 * act * (1.0 + lax.erf(act * (2.0 ** -0.5)))).astype(BF16)
    PK = 16
    blk = (nk // PK, PK, LANES)
    for c in range(tt // LANES):
        cs = slice(c * LANES, (c + 1) * LANES)
        for rr in range(PEER_ET // nk):
            gsum = jnp.zeros(blk, BF16)
            for h in range(PEER_HEADS):
                r1 = jnp.broadcast_to(r1_ref[0, h, rr:rr + 1, cs], (PK, LANES)).astype(BF16)
                c1 = jnp.broadcast_to(c1_ref[0, h, rr:rr + 1, cs], (PK, LANES)).astype(BF16)
                sel = rk_sc[h, :, cs].reshape(blk) < r1[None]
                gsum = gsum + jnp.where(sel, p2_sc[h, :, cs].reshape(blk), jnp.zeros((), BF16)) * c1[None]
            rs = slice(rr * nk, (rr + 1) * nk)
            w_sc[rs, cs] = gsum.reshape(nk, LANES) * w_sc[rs, cs]
    acc_sc[...] += _dot(vt_ref[...], w_sc[...])

    @pl.when(e == pl.num_programs(2) - 1)
    def _():
        z = alpha * x1_ref[0] + acc_sc[...].T
        o_ref[0] = _layernorm(z, g_ref[...], b_ref[...])


def _peer_ffn(x1, x1t, r1, c1, rk2, p2, peer_u, peer_v, ln_g, ln_b, alpha, tt):
    b, s, d = x1.shape
    ne = peer_u.shape[0]
    nk = PEER_N_KEYS
    rows = PEER_ET // nk
    u = peer_u.astype(BF16)
    vt = peer_v.T.astype(BF16)
    row_spec = pl.BlockSpec((1, PEER_HEADS, rows, tt), lambda i, j, e: (i, 0, e, j))
    key_spec = pl.BlockSpec((1, PEER_HEADS, nk, tt), lambda i, j, e: (i, 0, 0, j))
    return pl.pallas_call(
        functools.partial(_peer_ffn_kernel, alpha=alpha),
        out_shape=jax.ShapeDtypeStruct((b, s, d), F32),
        grid=(b, s // tt, ne // PEER_ET),
        in_specs=[pl.BlockSpec((1, tt, d), lambda i, j, e: (i, j, 0)),
                  pl.BlockSpec((1, d, tt), lambda i, j, e: (i, 0, j)),
                  row_spec, row_spec, key_spec, key_spec,
                  pl.BlockSpec((PEER_ET, d), lambda i, j, e: (e, 0)),
                  pl.BlockSpec((d, PEER_ET), lambda i, j, e: (0, e)),
                  pl.BlockSpec((1, d), lambda i, j, e: (0, 0)),
                  pl.BlockSpec((1, d), lambda i, j, e: (0, 0))],
        out_specs=pl.BlockSpec((1, tt, d), lambda i, j, e: (i, j, 0)),
        scratch_shapes=[pltpu.VMEM((d, tt), F32), pltpu.VMEM((PEER_ET, tt), BF16),
                        pltpu.VMEM((PEER_HEADS, nk, tt), BF16), pltpu.VMEM((PEER_HEADS, nk, tt), BF16)],
        compiler_params=_cparams(("parallel", "parallel", "arbitrary")),
        name="peer_ffn",
    )(x1, x1t, r1, c1, rk2, p2, u, vt, ln_g.reshape(1, d), ln_b.reshape(1, d))


def _tile(n, pref):
    t = min(n, pref)
    while n % t:
        t //= 2
    return t


def _layer(x, mem, rel_bias, w_in, q_norm_g, kv_norm_g, w_uq, w_uk, w_uv, w_idx_q,
           rwkv_mu, rwkv_w0, rwkv_w2, rwkv_a0, rwkv_a2, rwkv_g2, rwkv_k_k, rwkv_k_a, rwkv_r_k,
           rwkv_lnx_g, rwkv_lnx_b, w_mem_kv, w_br_dsa, w_br_rwkv, w_br_mem, w_out, ln1_g, ln1_b,
           w_peer_q, peer_keys, peer_u, peer_v, ln2_g, ln2_b, alpha):
    b, s, d = x.shape
    t = b * s
    tm = _tile(s, 512)
    x2 = x.reshape(t, d).astype(BF16)
    n_dsa = DSA_Q_RANK + DSA_KV_RANK + IDX_DIM + IDX_HEADS
    n_rwkv = 3 * RWKV_WIDTH + DECAY_RANK + AAA_RANK + GATE_RANK
    n_mem = MEM_HEADS * MEM_HEAD_DIM
    o1, o2, o3 = n_dsa, n_dsa + n_rwkv, n_dsa + n_rwkv + n_mem
    wb = w_in.astype(BF16)
    tl = _tile(t, 2048)
    dcols = _linear(x2, wb[:, :o1], F32, tl, n_dsa, name="in_dsa")
    rcols = _linear(x2, wb[:, o1:o2], F32, tl, n_rwkv // 2, name="in_rwkv")
    mem_q = _linear(x2, wb[:, o2:o3], BF16, tl, n_mem, name="in_memq")
    gates = _linear(x2, wb[:, o3:], BF16, tl, d, act="sigmoid", name="in_gates")

    q_lat, q_idx, w_idx, k_idx, c_kv = _dsa_prep(dcols.reshape(b, s, n_dsa), q_norm_g, kv_norm_g,
                                                 w_uq, w_uk, w_idx_q, tm)
    y_dsa = _dsa_attn(rel_bias, q_idx, w_idx, q_lat, k_idx, c_kv, w_uv)

    prep = _rwkv_prep(rcols, s, rwkv_mu, rwkv_w0, rwkv_w2, rwkv_a0, rwkv_a2, rwkv_g2,
                      rwkv_k_k, rwkv_k_a, rwkv_r_k, tm)
    y_rwkv = _rwkv_chunk(*[a.reshape(b, s, RWKV_WIDTH) for a in prep], rwkv_lnx_g, rwkv_lnx_b)

    m = mem.shape[1]
    mkv = _linear(mem.reshape(b * m, d), w_mem_kv.astype(BF16), BF16, _tile(b * m, 512), 2 * n_mem,
                  name="mem_kv").reshape(b, m, 2 * n_mem)
    y_mem = _mem_attn(mem_q.reshape(b, s, n_mem), mkv, tm)

    x1, x1b, x1t = _merge(x, y_dsa, y_rwkv, y_mem, gates.reshape(b, s, 3 * d),
                          w_br_dsa, w_br_rwkv, w_br_mem, w_out, ln1_g, ln1_b, alpha, tm)

    r1, c1, rk2, p2 = _peer_route(x1b, w_peer_q, peer_keys, _tile(s, 512))
    return _peer_ffn(x1, x1t, r1, c1, rk2, p2, peer_u, peer_v, ln2_g, ln2_b, alpha, _tile(s, 1024))


def kernel(x, mem, rel_bias, w_in, q_norm_g, kv_norm_g, w_uq, w_uk, w_uv, w_idx_q, rwkv_mu, rwkv_w0, rwkv_w2, rwkv_a0, rwkv_a2, rwkv_g2, rwkv_k_k, rwkv_k_a, rwkv_r_k, rwkv_lnx_g, rwkv_lnx_b, w_mem_kv, w_br_dsa, w_br_rwkv, w_br_mem, w_out, ln1_g, ln1_b, w_peer_q, peer_keys, peer_u, peer_v, ln2_g, ln2_b):
    depth = w_in.shape[0]
    alpha = (2 * depth) ** 0.25
    for l in range(depth):
        x = _layer(x, mem, rel_bias, w_in[l], q_norm_g[l], kv_norm_g[l], w_uq[l], w_uk[l], w_uv[l],
                   w_idx_q[l], rwkv_mu[l], rwkv_w0[l], rwkv_w2[l], rwkv_a0[l], rwkv_a2[l], rwkv_g2[l],
                   rwkv_k_k[l], rwkv_k_a[l], rwkv_r_k[l], rwkv_lnx_g[l], rwkv_lnx_b[l], w_mem_kv[l],
                   w_br_dsa[l], w_br_rwkv[l], w_br_mem[l], w_out[l], ln1_g[l], ln1_b[l],
                   w_peer_q[l], peer_keys[l], peer_u[l], peer_v[l], ln2_g[l], ln2_b[l], alpha)
    return x
```

```python
import functools
import math

import jax
import jax.numpy as jnp
from jax import lax
from jax.experimental import pallas as pl
from jax.experimental.pallas import tpu as pltpu

F32 = jnp.float32
BF16 = jnp.bfloat16
I32 = jnp.int32

DSA_HEADS = 8
DSA_HEAD_DIM = 64
DSA_Q_RANK = 256
DSA_KV_RANK = 128
IDX_HEADS = 8
IDX_DIM = 32
INDEX_TOPK_MAX = 256
NUM_BUCKETS = 32
MAX_DISTANCE = 128
RWKV_HEADS = 8
RWKV_HEAD_DIM = 64
RWKV_WIDTH = RWKV_HEADS * RWKV_HEAD_DIM
DECAY_RANK = 64
AAA_RANK = 64
GATE_RANK = 128
GN_EPS = 64e-5
MEM_HEADS = 4
MEM_HEAD_DIM = 128
PEER_HEADS = 8
PEER_N_KEYS = 128
PEER_HALF = 128
PEER_TOPK = 16
LN_EPS = 1e-5
RMS_EPS = 1e-6

LANES = 128
VMEM_LIMIT = 56 * 1024 * 1024

INT_MIN = -(2 ** 31)
NEG_INF = float("-inf")


def _cparams(sem, vmem=VMEM_LIMIT):
    return pltpu.CompilerParams(dimension_semantics=sem, vmem_limit_bytes=vmem)


def _dot(a, b):
    return jnp.dot(a, b, preferred_element_type=F32)


def _dot_nt(a, b):
    return lax.dot_general(a, b, (((1,), (1,)), ((), ())), preferred_element_type=F32)


def _split2(x):
    hi = x.astype(BF16)
    lo = (x - hi.astype(F32)).astype(BF16)
    return hi, lo


def _dot3(a, b, nt=False):
    d = _dot_nt if nt else _dot
    ah, al = _split2(a)
    bh, bl = _split2(b)
    return d(ah, bh) + (d(ah, bl) + d(al, bh))


def _linear_kernel(x_ref, w_ref, o_ref, *, act):
    y = _dot(x_ref[...].astype(BF16), w_ref[...])
    if act == "sigmoid":
        y = jax.nn.sigmoid(y)
    o_ref[...] = y.astype(o_ref.dtype)


def _linear(x, w, out_dtype, tm, tn, act=None, name="linear"):
    m, k = x.shape
    n = w.shape[1]
    return pl.pallas_call(
        functools.partial(_linear_kernel, act=act),
        out_shape=jax.ShapeDtypeStruct((m, n), out_dtype),
        grid=(m // tm, n // tn),
        in_specs=[pl.BlockSpec((tm, k), lambda i, j: (i, 0)),
                  pl.BlockSpec((k, tn), lambda i, j: (0, j))],
        out_specs=pl.BlockSpec((tm, tn), lambda i, j: (i, j)),
        compiler_params=_cparams(("parallel", "parallel")),
        name=name,
    )(x, w)


def _dsa_prep_kernel(dc_ref, qg_ref, kvg_ref, wuq_ref, wuk_ref, wiq_ref,
                     ql_ref, qi_ref, wi_ref, ki_ref, kv_ref):
    dc = dc_ref[0]
    c_q = dc[:, :DSA_Q_RANK]
    c_kv = dc[:, DSA_Q_RANK:DSA_Q_RANK + DSA_KV_RANK]
    off = DSA_Q_RANK + DSA_KV_RANK
    k_idx = dc[:, off:off + IDX_DIM]
    w_idx = dc[:, off + IDX_DIM:off + IDX_DIM + IDX_HEADS]
    c_q = c_q * lax.rsqrt(jnp.mean(c_q * c_q, axis=-1, keepdims=True) + RMS_EPS) * qg_ref[...]
    c_kv = c_kv * lax.rsqrt(jnp.mean(c_kv * c_kv, axis=-1, keepdims=True) + RMS_EPS) * kvg_ref[...]
    cqb = c_q.astype(BF16)
    scale = DSA_HEAD_DIM ** -0.5
    for h in range(DSA_HEADS):
        q_h = _dot(cqb, wuq_ref[h])
        ql_ref[0, h] = (_dot(q_h.astype(BF16), wuk_ref[h]) * scale).astype(BF16)
        qi_ref[0, h] = _dot(cqb, wiq_ref[h]).astype(BF16)
    wi_ref[0] = w_idx * (IDX_HEADS ** -0.5 * IDX_DIM ** -0.5)
    ki_ref[0] = k_idx.astype(BF16)
    kv_ref[0] = c_kv.astype(BF16)


def _dsa_prep(dcols, q_norm_g, kv_norm_g, w_uq, w_uk, w_idx_q, tm):
    b, s, wd = dcols.shape
    h = DSA_HEADS
    wuq_h = w_uq.reshape(DSA_Q_RANK, h, DSA_HEAD_DIM).transpose(1, 0, 2).astype(BF16)
    wuk_h = w_uk.transpose(1, 2, 0).astype(BF16)
    wiq_h = w_idx_q.reshape(DSA_Q_RANK, IDX_HEADS, IDX_DIM).transpose(1, 0, 2).astype(BF16)
    full = lambda shape: pl.BlockSpec(shape, lambda i, j: (0,) * len(shape))
    return pl.pallas_call(
        _dsa_prep_kernel,
        out_shape=(jax.ShapeDtypeStruct((b, h, s, DSA_KV_RANK), BF16),
                   jax.ShapeDtypeStruct((b, IDX_HEADS, s, IDX_DIM), BF16),
                   jax.ShapeDtypeStruct((b, s, IDX_HEADS), F32),
                   jax.ShapeDtypeStruct((b, s, IDX_DIM), BF16),
                   jax.ShapeDtypeStruct((b, s, DSA_KV_RANK), BF16)),
        grid=(b, s // tm),
        in_specs=[pl.BlockSpec((1, tm, wd), lambda i, j: (i, j, 0)),
                  full((1, DSA_Q_RANK)), full((1, DSA_KV_RANK)),
                  full(wuq_h.shape), full(wuk_h.shape), full(wiq_h.shape)],
        out_specs=(pl.BlockSpec((1, h, tm, DSA_KV_RANK), lambda i, j: (i, 0, j, 0)),
                   pl.BlockSpec((1, IDX_HEADS, tm, IDX_DIM), lambda i, j: (i, 0, j, 0)),
                   pl.BlockSpec((1, tm, IDX_HEADS), lambda i, j: (i, j, 0)),
                   pl.BlockSpec((1, tm, IDX_DIM), lambda i, j: (i, j, 0)),
                   pl.BlockSpec((1, tm, DSA_KV_RANK), lambda i, j: (i, j, 0))),
        compiler_params=_cparams(("parallel", "parallel")),
        name="dsa_prep",
    )(dcols, q_norm_g.reshape(1, -1), kv_norm_g.reshape(1, -1), wuq_h, wuk_h, wiq_h)


DSA_QB = 128
DSA_KB = 128
DSA_SC = 512


def _t5_bias_tile(rel_ref, h, off):
    row = lax.broadcasted_iota(I32, (DSA_QB, DSA_KB), 0)
    col = lax.broadcasted_iota(I32, (DSA_QB, DSA_KB), 1)
    n = jnp.maximum(row - col + off, 0)
    max_exact = NUM_BUCKETS // 2
    nf = jnp.maximum(n, max_exact).astype(F32)
    large = max_exact + jnp.floor(jnp.log(nf / max_exact) / math.log(MAX_DISTANCE / max_exact)
                                  * (NUM_BUCKETS - max_exact)).astype(I32)
    large = jnp.minimum(large, NUM_BUCKETS - 1)
    bucket = jnp.where(n < max_exact, n, large)
    last = rel_ref[NUM_BUCKETS - 1, h]
    out = jnp.zeros((DSA_QB, DSA_KB), F32)
    for bk in range(NUM_BUCKETS - 1):
        out = jnp.where(bucket == bk, rel_ref[bk, h] - last, out)
    return out


def _fold_lanes(x, op):
    out = x[..., :LANES]
    for c in range(1, x.shape[-1] // LANES):
        out = op(out, x[..., c * LANES:(c + 1) * LANES])
    return out


def _dsa_attn_kernel(rel_ref, qi_ref, wi_ref, ql_ref, ki_ref, kv_ref, wuv_ref, o_ref,
                     key_sc, bias_sc, mx_sc, ref_sc, ls_sc, acc_sc, *, topk, seq):
    H = DSA_HEADS
    QB, KB, SC = DSA_QB, DSA_KB, DSA_SC
    qb = pl.program_id(1)
    q0 = qb * QB

    @pl.when((pl.program_id(0) == 0) & (qb == 0))
    def _():
        for h in range(H):
            bias_sc[0, h] = _t5_bias_tile(rel_ref, h, 0)
            bias_sc[1, h] = _t5_bias_tile(rel_ref, h, KB)

    n_sc = (q0 + QB + SC - 1) // SC
    wi = wi_ref[0]
    HALF = SC // 2
    row_g = q0 + lax.broadcasted_iota(I32, (QB, HALF), 0)
    col_h = lax.broadcasted_iota(I32, (QB, HALF), 1)

    def score_body(j, carry):
        for c in range(2):
            k0 = pl.multiple_of(j * SC + c * HALF, HALF)
            ks = ki_ref[0, pl.ds(k0, HALF), :]
            acc = jnp.zeros((QB, HALF), F32)
            for h in range(IDX_HEADS):
                lg = _dot_nt(qi_ref[0, h], ks)
                acc = acc + wi[:, h:h + 1] * jnp.maximum(lg, 0.0)
            key_sc[:, pl.ds(k0, HALF)] = jnp.where(col_h + k0 <= row_g, acc, NEG_INF)
        return carry

    lax.fori_loop(0, n_sc, score_body, 0)

    col_l = lax.broadcasted_iota(I32, (QB, LANES), 1)

    def count(ind):
        def body(j, acc):
            for c in range(SC // LANES):
                k0 = pl.multiple_of(j * SC + c * LANES, LANES)
                acc = acc + ind(key_sc[:, pl.ds(k0, LANES)], k0)
            return acc
        acc = lax.fori_loop(0, n_sc, body, jnp.zeros((QB, LANES), I32))
        return jnp.sum(acc, axis=1, keepdims=True)

    def code_to_float(ru):
        c = ru ^ INT_MIN
        return pltpu.bitcast(c ^ ((c >> 31) & 0x7FFFFFFF), jnp.float32)

    HB = QB // 2

    def descent(n):
        def count_half(r0, ru, bit):
            cand = code_to_float(ru | bit)
            acc = jnp.zeros((HB, LANES), I32)
            for c in range(n * SC // LANES):
                acc = acc + jnp.where(key_sc[r0:r0 + HB, c * LANES:(c + 1) * LANES] >= cand, 1, 0)
            return acc

        def update(ru, acc, bit):
            cnt = jnp.sum(acc, axis=1, keepdims=True)
            return jnp.where(cnt >= topk, ru | bit, ru)

        def run():
            def bis_body(i, carry):
                ru_a, ru_b, acc_b = carry
                sh = lax.convert_element_type(31 - i, I32)
                bit = lax.shift_left(jnp.int32(1), sh)
                bit_b = jnp.where(sh == 31, 0, lax.shift_left(jnp.int32(2), jnp.minimum(sh, 30)))
                ru_b = update(ru_b, acc_b, bit_b)
                ru_a = update(ru_a, count_half(0, ru_a, bit), bit)
                return ru_a, ru_b, count_half(HB, ru_b, bit)

            zero = jnp.minimum(lax.broadcasted_iota(I32, (HB, LANES), 0), 0)
            ru_a, ru_b, acc_b = lax.fori_loop(0, 32, bis_body, (zero, zero, zero))
            return jnp.concatenate([ru_a, update(ru_b, acc_b, 1)], axis=0)
        return run

    skip = lambda: jnp.zeros((QB, LANES), I32)
    ru = lax.switch(jnp.where(q0 + QB <= topk, 0, n_sc),
                    [skip] + [descent(n) for n in range(1, seq // SC + 1)])[:, :1]
    few = q0 + lax.broadcasted_iota(I32, (QB, 1), 0) < topk
    tau = jnp.where(few, jnp.finfo(jnp.float32).min, code_to_float(ru))
    tau_b = jnp.broadcast_to(tau, (QB, LANES))
    cnt_ge = count(lambda blk, k0: jnp.where(blk >= tau_b, 1, 0))

    @pl.when(jnp.max(cnt_ge) > topk)
    def _():
        cnt_gt = count(lambda blk, k0: jnp.where(blk > tau_b, 1, 0))
        need = topk - cnt_gt

        def pos_body(_, lohi):
            lo, hi = lohi
            mid = (lo + hi) >> 1
            mid_b = jnp.broadcast_to(mid, (QB, LANES))
            f = count(lambda blk, k0: jnp.where(blk == tau_b, jnp.where(col_l + k0 <= mid_b, 1, 0), 0))
            ok = f >= need
            return jnp.where(ok, lo, mid + 1), jnp.where(ok, mid, hi)

        lo, _hi = lax.fori_loop(0, max(1, (seq - 1).bit_length()), pos_body,
                                (jnp.zeros((QB, 1), I32), jnp.full((QB, 1), seq - 1, I32)))
        jstar = jnp.broadcast_to(lo, (QB, LANES))

        def fix_body(j, carry):
            for c in range(SC // LANES):
                k0 = pl.multiple_of(j * SC + c * LANES, LANES)
                blk = key_sc[:, pl.ds(k0, LANES)]
                dropped = jnp.where(col_l + k0 > jstar, NEG_INF, blk)
                key_sc[:, pl.ds(k0, LANES)] = jnp.where(blk == tau_b, dropped, blk)
            return carry

        lax.fori_loop(0, n_sc, fix_body, 0)

    ql = ql_ref[0].reshape(H * QB, DSA_KV_RANK)
    near0 = jnp.maximum(qb - 1, 0) * KB
    n_far = (near0 + SC - 1) // SC
    tau_s = jnp.broadcast_to(tau, (QB, SC))
    tau_k = jnp.broadcast_to(tau, (QB, KB))
    col_s = lax.broadcasted_iota(I32, (QB, SC), 1)

    def far_logits(j):
        k0 = pl.multiple_of(j * SC, SC)
        kvb = kv_ref[0, pl.ds(k0, SC), :]
        blk = key_sc[:, pl.ds(k0, SC)]
        madd = jnp.where(blk >= tau_s, jnp.where(col_s + k0 < near0, 0.0, NEG_INF), NEG_INF)
        return _dot_nt(ql, kvb).reshape(H, QB, SC) + madd[None], kvb

    def near_logits(jb, bias):
        k0 = pl.multiple_of(jb * KB, KB)
        kvb = kv_ref[0, pl.ds(k0, KB), :]
        madd = jnp.where(key_sc[:, pl.ds(k0, KB)] >= tau_k, 0.0, NEG_INF)
        return _dot_nt(ql, kvb).reshape(H, QB, KB) + (bias + madd[None]), kvb

    SHIFT_SLACK = 60.0
    row_k = lax.broadcasted_iota(I32, (QB, KB), 0)
    col_k = lax.broadcasted_iota(I32, (QB, KB), 1)
    causal_add = jnp.where(col_k <= row_k, 0.0, NEG_INF)

    def near_raw(jb, bias):
        k0 = pl.multiple_of(jb * KB, KB)
        return _dot_nt(ql, kv_ref[0, pl.ds(k0, KB), :]).reshape(H, QB, KB) + bias

    ref_sc[...] = near_raw(qb, bias_sc[0]) + causal_add[None]

    @pl.when(qb >= 1)
    def _():
        ref_sc[...] = jnp.maximum(ref_sc[...], near_raw(qb - 1, bias_sc[1]))

    ref_sc[...] = jnp.broadcast_to(jnp.max(ref_sc[...], axis=-1, keepdims=True), ref_sc.shape)

    def run_pass(track_max):
        ls_sc[...] = jnp.zeros(ls_sc.shape, F32)
        acc_sc[...] = jnp.zeros(acc_sc.shape, F32)
        if track_max:
            mx_sc[...] = jnp.full(mx_sc.shape, NEG_INF, F32)

        def far_acc(j, carry):
            s, kvb = far_logits(j)
            if track_max:
                mx_sc[...] = jnp.maximum(mx_sc[...], _fold_lanes(s, jnp.maximum))
            m_b = ref_sc[...]
            p = jnp.concatenate([jnp.exp(s[..., c * LANES:(c + 1) * LANES] - m_b)
                                 for c in range(SC // LANES)], axis=-1)
            ls_sc[...] += _fold_lanes(p, jnp.add)
            acc_sc[...] += _dot(p.reshape(H * QB, SC).astype(BF16), kvb).reshape(H, QB, DSA_KV_RANK)
            return carry

        lax.fori_loop(0, n_far, far_acc, 0)

        def near_acc(jb, bias):
            s, kvb = near_logits(jb, bias)
            if track_max:
                mx_sc[...] = jnp.maximum(mx_sc[...], s)
            p = jnp.exp(s - ref_sc[...])
            ls_sc[...] += p
            acc_sc[...] += _dot(p.reshape(H * QB, KB).astype(BF16), kvb).reshape(H, QB, DSA_KV_RANK)

        @pl.when(qb >= 1)
        def _():
            near_acc(qb - 1, bias_sc[1])

        near_acc(qb, bias_sc[0])

    run_pass(True)
    mx = jnp.max(mx_sc[...], axis=-1, keepdims=True)

    @pl.when(jnp.max(jnp.abs(mx - ref_sc[:, :, 0:1])) > SHIFT_SLACK)
    def _():
        ref_sc[...] = jnp.broadcast_to(mx, ref_sc.shape)
        run_pass(False)

    o_lat = acc_sc[...] / jnp.sum(ls_sc[...], axis=-1, keepdims=True)
    y = jnp.zeros((QB, DSA_HEADS * DSA_HEAD_DIM), F32)
    for h in range(H):
        y = y + _dot(o_lat[h].astype(BF16), wuv_ref[h])
    o_ref[0] = y.astype(o_ref.dtype)


def _dsa_attn(rel_bias, q_idx, w_idx, q_lat, k_idx, c_kv, w_uv):
    b, h, s, _ = q_lat.shape
    topk = min(INDEX_TOPK_MAX, s // 4)
    width = DSA_HEADS * DSA_HEAD_DIM
    eye = jnp.eye(DSA_HEADS, dtype=F32)
    wuv_e = jnp.einsum("chd,hg->hcgd", w_uv, eye).reshape(DSA_HEADS, DSA_KV_RANK, width).astype(BF16)
    QB = DSA_QB
    assert s % DSA_SC == 0
    return pl.pallas_call(
        functools.partial(_dsa_attn_kernel, topk=topk, seq=s),
        out_shape=jax.ShapeDtypeStruct((b, s, width), BF16),
        grid=(b, s // QB),
        in_specs=[pl.BlockSpec(memory_space=pltpu.SMEM),
                  pl.BlockSpec((1, IDX_HEADS, QB, IDX_DIM), lambda i, j: (i, 0, j, 0)),
                  pl.BlockSpec((1, QB, IDX_HEADS), lambda i, j: (i, j, 0)),
                  pl.BlockSpec((1, h, QB, DSA_KV_RANK), lambda i, j: (i, 0, j, 0)),
                  pl.BlockSpec((1, s, IDX_DIM), lambda i, j: (i, 0, 0)),
                  pl.BlockSpec((1, s, DSA_KV_RANK), lambda i, j: (i, 0, 0)),
                  pl.BlockSpec(wuv_e.shape, lambda i, j: (0, 0, 0))],
        out_specs=pl.BlockSpec((1, QB, width), lambda i, j: (i, j, 0)),
        scratch_shapes=[pltpu.VMEM((QB, s), F32),
                        pltpu.VMEM((2, h, QB, DSA_KB), F32),
                        pltpu.VMEM((h, QB, LANES), F32),
                        pltpu.VMEM((h, QB, LANES), F32),
                        pltpu.VMEM((h, QB, LANES), F32),
                        pltpu.VMEM((h, QB, DSA_KV_RANK), F32)],
        compiler_params=_cparams(("arbitrary", "arbitrary")),
        name="dsa_attn",
    )(rel_bias, q_idx, w_idx, q_lat, k_idx, c_kv, wuv_e)


def _rwkv_prep_kernel(c_ref, p_ref, mu_ref, w0_ref, a0_ref, kk_ref, ka_ref, rk_ref,
                      wwa_ref, g2_ref, ones_ref,
                      r_o, k_o, v_o, kk_o, b_o, lw_o, bonus_o, g_o, *, tiles_per_seq):
    W = RWKV_WIDTH
    cols = c_ref[...]
    first = (pl.program_id(0) % tiles_per_seq) == 0
    prev_row = jnp.where(first, 0.0, p_ref[7:8, :])
    rolled = pltpu.roll(cols, 1, 0)
    row = lax.broadcasted_iota(I32, cols.shape, 0)
    prev = jnp.where(row == 0, prev_row, rolled)
    xs = cols + (prev - cols) * mu_ref[...]
    r = xs[:, 0:W]
    k = xs[:, W:2 * W]
    v = xs[:, 2 * W:3 * W]
    lora = xs[:, 3 * W:3 * W + DECAY_RANK + AAA_RANK]
    gl = xs[:, 3 * W + DECAY_RANK + AAA_RANK:]
    lane = lax.broadcasted_iota(I32, lora.shape, 1)
    lora = jnp.where(lane < DECAY_RANK, jnp.tanh(lora), lora)
    wa = _dot(lora.astype(BF16), wwa_ref[...])
    w = -jax.nn.softplus(-(w0_ref[...] + wa[:, :W])) - 0.5
    a = jax.nn.sigmoid(a0_ref[...] + wa[:, W:])
    ones_bd = ones_ref[...]

    def head_sum(t):
        hi, lo = _split2(t)
        return _dot(hi, ones_bd) + _dot(lo, ones_bd)

    kk = k * kk_ref[...]
    kk = kk / jnp.maximum(jnp.sqrt(head_sum(kk * kk)), 1e-12)
    k2 = k * (1.0 + (a - 1.0) * ka_ref[...])
    r_o[...] = r
    k_o[...] = k2
    v_o[...] = v
    kk_o[...] = kk
    b_o[...] = kk * a
    lw_o[...] = -jnp.exp(w)
    bonus_o[...] = head_sum(r * k2 * rk_ref[...]) * v
    g_o[...] = _dot(jax.nn.sigmoid(gl).astype(BF16), g2_ref[...])


def _rwkv_prep(cols, s, mu, w0, w2, a0, a2, g2, k_k, k_a, r_k, tm):
    t, wc = cols.shape
    W = RWKV_WIDTH
    wwa = jnp.zeros((DECAY_RANK + AAA_RANK, 2 * W), F32)
    wwa = wwa.at[:DECAY_RANK, :W].set(w2).at[DECAY_RANK:, W:].set(a2).astype(BF16)
    head_id = jnp.arange(W) // RWKV_HEAD_DIM
    ones_bd = (head_id[:, None] == head_id[None, :]).astype(BF16)
    row = lambda a: a.reshape(1, -1)
    vec = lambda n: pl.BlockSpec((1, n), lambda i: (0, 0))
    tok = jax.ShapeDtypeStruct((t, W), F32)
    tok_spec = pl.BlockSpec((tm, W), lambda i: (i, 0))
    return pl.pallas_call(
        functools.partial(_rwkv_prep_kernel, tiles_per_seq=s // tm),
        out_shape=(tok,) * 8,
        grid=(t // tm,),
        in_specs=[pl.BlockSpec((tm, wc), lambda i: (i, 0)),
                  pl.BlockSpec((8, wc), lambda i: (jnp.maximum(i * (tm // 8) - 1, 0), 0)),
                  vec(wc), vec(W), vec(W), vec(W), vec(W), vec(W),
                  pl.BlockSpec(wwa.shape, lambda i: (0, 0)),
                  pl.BlockSpec((GATE_RANK, W), lambda i: (0, 0)),
                  pl.BlockSpec((W, W), lambda i: (0, 0))],
        out_specs=(tok_spec,) * 8,
        compiler_params=_cparams(("parallel",)),
        name="rwkv_prep",
    )(cols, cols, row(mu), row(w0), row(a0), row(k_k), row(k_a), row(r_k.reshape(-1)),
      wwa, g2.astype(BF16), ones_bd)


RWKV_CHUNK = 64
RWKV_GROUP = 8


def _rwkv_chunk_kernel(r_ref, k_ref, v_ref, kk_ref, b_ref, lw_ref, bonus_ref, g_ref,
                       lg_ref, lb_ref, o_ref, st_sc):
    C = RWKV_CHUNK
    N = RWKV_HEAD_DIM
    P = 2 * N
    nb = r_ref.shape[0]
    npair = RWKV_WIDTH // P

    @pl.when(pl.program_id(0) == 0)
    def _():
        st_sc[...] = jnp.zeros(st_sc.shape, F32)

    row_c = lax.broadcasted_iota(I32, (C, P), 0)
    lane_c = lax.broadcasted_iota(I32, (C, P), 1)
    tcol = lane_c & (N - 1)
    strict = tcol < row_c
    incl = tcol <= row_c
    lane_lo_n = lax.broadcasted_iota(I32, (N, P), 1) < N
    r2 = lax.broadcasted_iota(I32, (P, P), 0)
    l2 = lax.broadcasted_iota(I32, (P, P), 1)
    bdmask = (r2 < N) == (l2 < N)
    ones_bd = jnp.where(bdmask, 1.0, 0.0).astype(BF16)
    ones2 = jnp.concatenate([ones_bd, ones_bd], axis=0)
    ti = lax.broadcasted_iota(I32, (C, C), 0)
    si = lax.broadcasted_iota(I32, (C, C), 1)
    tri = jnp.where(si <= ti, 1.0, 0.0).astype(BF16)
    tri3 = jnp.concatenate([tri, tri, tri], axis=1)

    def bd(x):
        return jnp.where(bdmask, jnp.concatenate([x, x], axis=0), 0.0).astype(BF16)

    def head_mean(t):
        hi, lo = _split2(t)
        return _dot(jnp.concatenate([hi, lo], axis=1), ones2) * (1.0 / N)

    units = [(bi, p) for bi in range(nb) for p in range(npair)]
    nlev = max(1, (C - 1).bit_length())
    for g0 in range(0, len(units), RWKV_GROUP):
        grp = units[g0:g0 + RWKV_GROUP]
        n = len(grp)
        ld = lambda ref: [ref[bi, :, p * P:(p + 1) * P] for bi, p in grp]
        r, k, v, kk, b, lw = ld(r_ref), ld(k_ref), ld(v_ref), ld(kk_ref), ld(b_ref), ld(lw_ref)

        cum = []
        for u in range(n):
            hi = lw[u].astype(BF16)
            r1 = lw[u] - hi.astype(F32)
            mid = r1.astype(BF16)
            lo = (r1 - mid.astype(F32)).astype(BF16)
            cum.append(_dot(tri3, jnp.concatenate([hi, mid, lo], axis=0)))
        cum_last = [c[C - 1:C, :] for c in cum]
        p_inv = [jnp.exp(-c) for c in cum]
        a_t = [-kk[u] * jnp.exp(cum[u] - lw[u]) for u in range(n)]
        r_t = [r[u] * jnp.exp(cum[u]) for u in range(n)]
        dec = [jnp.exp(cum_last[u] - cum[u]) for u in range(n)]
        ar = [jnp.concatenate([a_t[u], r_t[u]], axis=0).astype(BF16) for u in range(n)]
        sb = [_dot_nt(ar[u], bd(b[u] * p_inv[u])) for u in range(n)]
        sk = [_dot_nt(ar[u], bd(k[u] * p_inv[u])) for u in range(n)]
        bd_v = [bd(v[u]) for u in range(n)]
        lp = [jnp.where(strict, sb[u][:C], 0.0) for u in range(n)]
        l_ak = [jnp.where(strict, sk[u][:C], 0.0).astype(BF16) for u in range(n)]
        m_rb = [jnp.where(incl, sb[u][C:], 0.0).astype(BF16) for u in range(n)]
        m_rk = [jnp.where(incl, sk[u][C:], 0.0).astype(BF16) for u in range(n)]

        xa = list(a_t)
        xu = [_dot(l_ak[u], bd_v[u]) for u in range(n)]
        for lev in range(nlev):
            last = lev == nlev - 1
            for u in range(n):
                parts = [bd(xa[u]), bd(xu[u])] + ([] if last else [bd(lp[u])])
                res = _dot(lp[u].astype(BF16), jnp.concatenate(parts, axis=1))
                xa[u] = xa[u] + res[:, :P]
                xu[u] = xu[u] + res[:, P:2 * P]
                if not last:
                    lp[u] = res[:, 2 * P:]

        res = [_dot(m_rb[u], jnp.concatenate([bd(xa[u]), bd(xu[u])], axis=1)) for u in range(n)]
        r_hat = [r_t[u] + res[u][:, :P] for u in range(n)]
        y0 = [res[u][:, P:] + _dot(m_rk[u], bd_v[u]) for u in range(n)]
        zt = [jnp.concatenate([b[u] * dec[u], k[u] * dec[u]], axis=0).T.astype(BF16) for u in range(n)]
        pct = [jnp.broadcast_to(cum_last[u], (P, P)).T for u in range(n)]
        pcm = [jnp.exp(jnp.where(lane_lo_n, pct[u][:N], pct[u][N:])) for u in range(n)]

        y = []
        for u, (bi, p) in enumerate(grp):
            st = st_sc[bi, p]
            ws = _dot(jnp.concatenate([xa[u], r_hat[u]], axis=0).astype(BF16), bd(st))
            wc = ws[:C] + xu[u]
            y.append(ws[C:] + y0[u])
            full = _dot(zt[u], jnp.concatenate([wc, v[u]], axis=0).astype(BF16))
            st_sc[bi, p] = pcm[u] * st + jnp.where(lane_lo_n, full[:N], full[N:])

        for u, (bi, p) in enumerate(grp):
            sl = slice(p * P, (p + 1) * P)
            yc = y[u] - head_mean(y[u])
            yn = yc * lax.rsqrt(head_mean(yc * yc) + GN_EPS) * lg_ref[:, sl] + lb_ref[:, sl]
            o_ref[bi, :, sl] = ((yn + bonus_ref[bi, :, sl]) * g_ref[bi, :, sl]).astype(o_ref.dtype)


def _rwkv_chunk(r, k, v, kk, bvec, lw, bonus, g, lnx_g, lnx_b):
    b, s, w = r.shape
    C = RWKV_CHUNK
    spec = pl.BlockSpec((b, C, w), lambda j: (0, j, 0))
    pspec = pl.BlockSpec((1, w), lambda j: (0, 0))
    return pl.pallas_call(
        _rwkv_chunk_kernel,
        out_shape=jax.ShapeDtypeStruct((b, s, w), BF16),
        grid=(s // C,),
        in_specs=[spec] * 8 + [pspec, pspec],
        out_specs=spec,
        scratch_shapes=[pltpu.VMEM((b, w // (2 * RWKV_HEAD_DIM), RWKV_HEAD_DIM, 2 * RWKV_HEAD_DIM), F32)],
        compiler_params=_cparams(("arbitrary",)),
        name="rwkv_chunk",
    )(r, k, v, kk, bvec, lw, bonus, g, lnx_g.reshape(1, w), lnx_b.reshape(1, w))


def _mem_attn_kernel(q_ref, kv_ref, o_ref):
    W = MEM_HEADS * MEM_HEAD_DIM
    scale = MEM_HEAD_DIM ** -0.5
    for h in range(MEM_HEADS):
        sl = slice(h * MEM_HEAD_DIM, (h + 1) * MEM_HEAD_DIM)
        q = q_ref[0, :, sl]
        k = kv_ref[0, :, sl]
        v = kv_ref[0, :, W + h * MEM_HEAD_DIM:W + (h + 1) * MEM_HEAD_DIM]
        s = _dot_nt(q, k) * scale
        s = s - jnp.max(s, axis=-1, keepdims=True)
        p = jnp.exp(s)
        p = p / jnp.sum(p, axis=-1, keepdims=True)
        o_ref[0, :, sl] = _dot(p.astype(BF16), v).astype(o_ref.dtype)


def _mem_attn(q, kv, tq):
    b, s, w = q.shape
    m = kv.shape[1]
    return pl.pallas_call(
        _mem_attn_kernel,
        out_shape=jax.ShapeDtypeStruct((b, s, w), BF16),
        grid=(b, s // tq),
        in_specs=[pl.BlockSpec((1, tq, w), lambda i, j: (i, j, 0)),
                  pl.BlockSpec((1, m, 2 * w), lambda i, j: (i, 0, 0))],
        out_specs=pl.BlockSpec((1, tq, w), lambda i, j: (i, j, 0)),
        compiler_params=_cparams(("parallel", "parallel")),
        name="mem_attn",
    )(q, kv)


def _layernorm(z, g, b):
    mu = jnp.mean(z, axis=-1, keepdims=True)
    zc = z - mu
    var = jnp.mean(zc * zc, axis=-1, keepdims=True)
    return zc * lax.rsqrt(var + LN_EPS) * g + b


def _merge_kernel(x_ref, yd_ref, yr_ref, ym_ref, gt_ref, wd_ref, wr_ref, wm_ref, wo_ref,
                  g_ref, b_ref, x1_ref, x1b_ref, x1t_ref, *, alpha):
    d = x_ref.shape[-1]
    br_d = _dot(yd_ref[0], wd_ref[...])
    br_r = _dot(yr_ref[0], wr_ref[...])
    br_m = _dot(ym_ref[0], wm_ref[...])
    gt = gt_ref[0]
    merged = (gt[:, 0:d].astype(F32) * br_d + gt[:, d:2 * d].astype(F32) * br_r
              + gt[:, 2 * d:3 * d].astype(F32) * br_m)
    z = alpha * x_ref[0] + _dot(merged.astype(BF16), wo_ref[...])
    x1 = _layernorm(z, g_ref[...], b_ref[...])
    x1_ref[0] = x1
    x1b_ref[0] = x1.astype(BF16)
    x1t_ref[0] = x1.T.astype(BF16)


def _merge(x, y_dsa, y_rwkv, y_mem, gates, w_br_dsa, w_br_rwkv, w_br_mem, w_out, ln_g, ln_b, alpha, tm):
    b, s, d = x.shape
    full2 = lambda a: pl.BlockSpec(a.shape, lambda i, j: (0,) * a.ndim)
    wd = w_br_dsa.astype(BF16)
    wr = w_br_rwkv.astype(BF16)
    wm = w_br_mem.astype(BF16)
    wo = w_out.astype(BF16)
    g2 = ln_g.reshape(1, d)
    b2 = ln_b.reshape(1, d)
    tok = lambda w: pl.BlockSpec((1, tm, w), lambda i, j: (i, j, 0))
    return pl.pallas_call(
        functools.partial(_merge_kernel, alpha=alpha),
        out_shape=(jax.ShapeDtypeStruct((b, s, d), F32),
                   jax.ShapeDtypeStruct((b, s, d), BF16),
                   jax.ShapeDtypeStruct((b, d, s), BF16)),
        grid=(b, s // tm),
        in_specs=[tok(d), tok(y_dsa.shape[-1]), tok(y_rwkv.shape[-1]), tok(y_mem.shape[-1]), tok(3 * d),
                  full2(wd), full2(wr), full2(wm), full2(wo), full2(g2), full2(b2)],
        out_specs=(tok(d), tok(d), pl.BlockSpec((1, d, tm), lambda i, j: (i, 0, j))),
        compiler_params=_cparams(("parallel", "parallel")),
        name="merge",
    )(x, y_dsa, y_rwkv, y_mem, gates, wd, wr, wm, wo, g2, b2)


def _sort_network(n):
    pairs = []

    def merge(lo, cnt, r):
        step = r * 2
        if step < cnt:
            merge(lo, cnt, step)
            merge(lo + r, cnt, step)
            for i in range(lo + r, lo + cnt - r, step):
                pairs.append((i, i + r))
        else:
            pairs.append((lo, lo + r))

    def sort(lo, cnt):
        if cnt > 1:
            half = cnt // 2
            sort(lo, half)
            sort(lo + half, half)
            merge(lo, cnt, 1)

    sort(0, n)
    return pairs


SUBLANES = 8
PEER_ROUTE_GROUP = 512
PEER_SLABS = PEER_N_KEYS // SUBLANES
PEER_SORT_PAIRS = _sort_network(PEER_SLABS)
PEER_CAND = [(i, j) for i in range(PEER_TOPK) for j in range(PEER_TOPK // (i + 1))]
PEER_CAND += [None] * (-len(PEER_CAND) % SUBLANES)


def _top_sorted(sub, k):
    slabs = [sub[SUBLANES * s:SUBLANES * (s + 1), :] for s in range(PEER_SLABS)]
    for i, j in PEER_SORT_PAIRS:
        slabs[i], slabs[j] = jnp.maximum(slabs[i], slabs[j]), jnp.minimum(slabs[i], slabs[j])
    tops = []
    for r in range(k):
        m = jnp.max(slabs[0], axis=0, keepdims=True)
        tops.append(m)
        if r < k - 1:
            hit = slabs[0] == m
            for i in range(k - 1 - r):
                slabs[i] = jnp.where(hit, slabs[i + 1], slabs[i])
    return tops


def _prefix_count(pred, rows):
    full = pred(rows[15])
    c1 = pred(rows[7])
    c2 = pred(jnp.where(c1, rows[11], rows[3]))
    c3 = pred(jnp.where(c1, jnp.where(c2, rows[13], rows[9]), jnp.where(c2, rows[5], rows[1])))
    c4 = pred(jnp.where(c1,
                        jnp.where(c2, jnp.where(c3, rows[14], rows[12]), jnp.where(c3, rows[10], rows[8])),
                        jnp.where(c2, jnp.where(c3, rows[6], rows[4]), jnp.where(c3, rows[2], rows[0]))))
    cnt = (jnp.where(c1, 8.0, 0.0) + jnp.where(c2, 4.0, 0.0)) + (jnp.where(c3, 2.0, 0.0) + jnp.where(c4, 1.0, 0.0))
    return jnp.where(full, 16.0, cnt)


def _peer_route_kernel(x_ref, wq_ref, keys_ref, r1_o, c1_o, rk2_o, p2_o):
    tt = x_ref.shape[1]
    K = PEER_TOPK
    q = _dot(x_ref[0], wq_ref[...])
    keys0 = keys_ref[0]
    keys1 = keys_ref[1]
    TG = min(tt, PEER_ROUTE_GROUP)
    sub_id = lax.broadcasted_iota(I32, (SUBLANES, TG), 0)

    def route(h, ts, s1, s2):
        a1 = _top_sorted(s1, K)
        a2 = _top_sorted(s2, K)
        best = a1[0] + a2[0]
        cand = []
        for g in range(len(PEER_CAND) // SUBLANES):
            slab = jnp.full((SUBLANES, TG), NEG_INF, F32)
            for s, ij in enumerate(PEER_CAND[SUBLANES * g:SUBLANES * (g + 1)]):
                if ij is not None:
                    slab = jnp.where(sub_id == s, a1[ij[0]] + a2[ij[1]], slab)
            cand.append(slab)
        zsum = jnp.zeros((1, TG), F32)
        v_k = best
        for rnk in range(K):
            m = cand[0]
            for slab in cand[1:]:
                m = jnp.maximum(m, slab)
            v_k = jnp.max(m, axis=0, keepdims=True)
            zsum = zsum + jnp.exp(v_k - best)
            if rnk < K - 1:
                cand = [jnp.where(slab == v_k, NEG_INF, slab) for slab in cand]
        r1_o[0, h, :, ts] = _prefix_count(lambda t: s1 + t >= v_k, a2)
        c1_o[0, h, :, ts] = jnp.exp(s1 - a1[0]) / zsum
        rk2_o[0, h, :, ts] = _prefix_count(lambda t: t > s2, a2).astype(BF16)
        p2_o[0, h, :, ts] = jnp.exp(s2 - a2[0]).astype(BF16)

    for h in range(PEER_HEADS):
        q1 = q[:, (2 * h) * PEER_HALF:(2 * h + 1) * PEER_HALF]
        q2 = q[:, (2 * h + 1) * PEER_HALF:(2 * h + 2) * PEER_HALF]
        s1 = _dot3(keys0, q1, nt=True)
        s2 = _dot3(keys1, q2, nt=True)
        for g in range(tt // TG):
            ts = slice(g * TG, (g + 1) * TG)
            route(h, ts, s1[:, ts], s2[:, ts])


def _peer_route(x1b, w_peer_q, peer_keys, tt):
    b, s, d = x1b.shape
    nk = PEER_N_KEYS
    wq = w_peer_q.astype(BF16)
    o32 = jax.ShapeDtypeStruct((b, PEER_HEADS, nk, s), F32)
    o16 = jax.ShapeDtypeStruct((b, PEER_HEADS, nk, s), BF16)
    ospec = pl.BlockSpec((1, PEER_HEADS, nk, tt), lambda i, j: (i, 0, 0, j))
    return pl.pallas_call(
        _peer_route_kernel,
        out_shape=(o32, o32, o16, o16),
        grid=(b, s // tt),
        in_specs=[pl.BlockSpec((1, tt, d), lambda i, j: (i, j, 0)),
                  pl.BlockSpec(wq.shape, lambda i, j: (0, 0)),
                  pl.BlockSpec(peer_keys.shape, lambda i, j: (0, 0, 0))],
        out_specs=(ospec,) * 4,
        compiler_params=_cparams(("parallel", "parallel")),
        name="peer_route",
    )(x1b, wq, peer_keys)


PEER_ET = 1024


def _peer_ffn_kernel(x1_ref, x1t_ref, r1_ref, c1_ref, rk2_ref, p2_ref, u_ref, vt_ref,
                     g_ref, b_ref, o_ref, acc_sc, *, alpha):
    e = pl.program_id(2)
    nk = PEER_N_KEYS
    tt = x1t_ref.shape[2]

    @pl.when(e == 0)
    def _():
        acc_sc[...] = jnp.zeros(acc_sc.shape, F32)

    act = _dot(u_ref[...], x1t_ref[0])
    act = 0.5 * act * (1.0 + lax.erf(act * (2.0 ** -0.5)))
    gate_rows = []
    for rr in range(PEER_ET // nk):
        gsum = jnp.zeros((nk, tt), BF16)
        for h in range(PEER_HEADS):
            r1 = r1_ref[0, h, rr:rr + 1, :].astype(BF16)
            c1 = c1_ref[0, h, rr:rr + 1, :].astype(BF16)
            sel = rk2_ref[0, h] < r1
            gsum = gsum + jnp.where(sel, p2_ref[0, h], jnp.zeros((), BF16)) * c1
        gate_rows.append(gsum)
    gate = jnp.concatenate(gate_rows, axis=0)
    w = gate * act.astype(BF16)
    acc_sc[...] += _dot(vt_ref[...], w)

    @pl.when(e == pl.num_programs(2) - 1)
    def _():
        z = alpha * x1_ref[0] + acc_sc[...].T
        o_ref[0] = _layernorm(z, g_ref[...], b_ref[...])


def _peer_ffn(x1, x1t, r1, c1, rk2, p2, peer_u, peer_v, ln_g, ln_b, alpha, tt):
    b, s, d = x1.shape
    ne = peer_u.shape[0]
    nk = PEER_N_KEYS
    rows = PEER_ET // nk
    u = peer_u.astype(BF16)
    vt = peer_v.T.astype(BF16)
    row_spec = pl.BlockSpec((1, PEER_HEADS, rows, tt), lambda i, j, e: (i, 0, e, j))
    key_spec = pl.BlockSpec((1, PEER_HEADS, nk, tt), lambda i, j, e: (i, 0, 0, j))
    return pl.pallas_call(
        functools.partial(_peer_ffn_kernel, alpha=alpha),
        out_shape=jax.ShapeDtypeStruct((b, s, d), F32),
        grid=(b, s // tt, ne // PEER_ET),
        in_specs=[pl.BlockSpec((1, tt, d), lambda i, j, e: (i, j, 0)),
                  pl.BlockSpec((1, d, tt), lambda i, j, e: (i, 0, j)),
                  row_spec, row_spec, key_spec, key_spec,
                  pl.BlockSpec((PEER_ET, d), lambda i, j, e: (e, 0)),
                  pl.BlockSpec((d, PEER_ET), lambda i, j, e: (0, e)),
                  pl.BlockSpec((1, d), lambda i, j, e: (0, 0)),
                  pl.BlockSpec((1, d), lambda i, j, e: (0, 0))],
        out_specs=pl.BlockSpec((1, tt, d), lambda i, j, e: (i, j, 0)),
        scratch_shapes=[pltpu.VMEM((d, tt), F32)],
        compiler_params=_cparams(("parallel", "parallel", "arbitrary")),
        name="peer_ffn",
    )(x1, x1t, r1, c1, rk2, p2, u, vt, ln_g.reshape(1, d), ln_b.reshape(1, d))


def _tile(n, pref):
    t = min(n, pref)
    while n % t:
        t //= 2
    return t


def _layer(x, mem, rel_bias, w_in, q_norm_g, kv_norm_g, w_uq, w_uk, w_uv, w_idx_q,
           rwkv_mu, rwkv_w0, rwkv_w2, rwkv_a0, rwkv_a2, rwkv_g2, rwkv_k_k, rwkv_k_a, rwkv_r_k,
           rwkv_lnx_g, rwkv_lnx_b, w_mem_kv, w_br_dsa, w_br_rwkv, w_br_mem, w_out, ln1_g, ln1_b,
           w_peer_q, peer_keys, peer_u, peer_v, ln2_g, ln2_b, alpha):
    b, s, d = x.shape
    t = b * s
    tm = _tile(s, 512)
    x2 = x.reshape(t, d).astype(BF16)
    n_dsa = DSA_Q_RANK + DSA_KV_RANK + IDX_DIM + IDX_HEADS
    n_rwkv = 3 * RWKV_WIDTH + DECAY_RANK + AAA_RANK + GATE_RANK
    n_mem = MEM_HEADS * MEM_HEAD_DIM
    o1, o2, o3 = n_dsa, n_dsa + n_rwkv, n_dsa + n_rwkv + n_mem
    wb = w_in.astype(BF16)
    tl = _tile(t, 2048)
    dcols = _linear(x2, wb[:, :o1], F32, tl, n_dsa, name="in_dsa")
    rcols = _linear(x2, wb[:, o1:o2], F32, tl, n_rwkv // 2, name="in_rwkv")
    mem_q = _linear(x2, wb[:, o2:o3], BF16, tl, n_mem, name="in_memq")
    gates = _linear(x2, wb[:, o3:], BF16, tl, d, act="sigmoid", name="in_gates")

    q_lat, q_idx, w_idx, k_idx, c_kv = _dsa_prep(dcols.reshape(b, s, n_dsa), q_norm_g, kv_norm_g,
                                                 w_uq, w_uk, w_idx_q, tm)
    y_dsa = _dsa_attn(rel_bias, q_idx, w_idx, q_lat, k_idx, c_kv, w_uv)

    prep = _rwkv_prep(rcols, s, rwkv_mu, rwkv_w0, rwkv_w2, rwkv_a0, rwkv_a2, rwkv_g2,
                      rwkv_k_k, rwkv_k_a, rwkv_r_k, tm)
    y_rwkv = _rwkv_chunk(*[a.reshape(b, s, RWKV_WIDTH) for a in prep], rwkv_lnx_g, rwkv_lnx_b)

    m = mem.shape[1]
    mkv = _linear(mem.reshape(b * m, d), w_mem_kv.astype(BF16), BF16, _tile(b * m, 512), 2 * n_mem,
                  name="mem_kv").reshape(b, m, 2 * n_mem)
    y_mem = _mem_attn(mem_q.reshape(b, s, n_mem), mkv, tm)

    x1, x1b, x1t = _merge(x, y_dsa, y_rwkv, y_mem, gates.reshape(b, s, 3 * d),
                          w_br_dsa, w_br_rwkv, w_br_mem, w_out, ln1_g, ln1_b, alpha, tm)

    r1, c1, rk2, p2 = _peer_route(x1b, w_peer_q, peer_keys, _tile(s, 512))
    return _peer_ffn(x1, x1t, r1, c1, rk2, p2, peer_u, peer_v, ln2_g, ln2_b, alpha, _tile(s, 1024))


def kernel(x, mem, rel_bias, w_in, q_norm_g, kv_norm_g, w_uq, w_uk, w_uv, w_idx_q, rwkv_mu, rwkv_w0, rwkv_w2, rwkv_a0, rwkv_a2, rwkv_g2, rwkv_k_k, rwkv_k_a, rwkv_r_k, rwkv_lnx_g, rwkv_lnx_b, w_mem_kv, w_br_dsa, w_br_rwkv, w_br_mem, w_out, ln1_g, ln1_b, w_peer_q, peer_keys, peer_u, peer_v, ln2_g, ln2_b):
    depth = w_in.shape[0]
    alpha = (2 * depth) ** 0.25
    for l in range(depth):
        x = _layer(x, mem, rel_bias, w_in[l], q_norm_g[l], kv_norm_g[l], w_uq[l], w_uk[l], w_uv[l],
                   w_idx_q[l], rwkv_mu[l], rwkv_w0[l], rwkv_w2[l], rwkv_a0[l], rwkv_a2[l], rwkv_g2[l],
                   rwkv_k_k[l], rwkv_k_a[l], rwkv_r_k[l], rwkv_lnx_g[l], rwkv_lnx_b[l], w_mem_kv[l],
                   w_br_dsa[l], w_br_rwkv[l], w_br_mem[l], w_out[l], ln1_g[l], ln1_b[l],
                   w_peer_q[l], peer_keys[l], peer_u[l], peer_v[l], ln2_g[l], ln2_b[l], alpha)
    return x
```

```python
import functools
import math

import jax
import jax.numpy as jnp
from jax import lax
from jax.experimental import pallas as pl
from jax.experimental.pallas import tpu as pltpu

F32 = jnp.float32
BF16 = jnp.bfloat16
I32 = jnp.int32

DSA_HEADS = 8
DSA_HEAD_DIM = 64
DSA_Q_RANK = 256
DSA_KV_RANK = 128
IDX_HEADS = 8
IDX_DIM = 32
INDEX_TOPK_MAX = 256
NUM_BUCKETS = 32
MAX_DISTANCE = 128
RWKV_HEADS = 8
RWKV_HEAD_DIM = 64
RWKV_WIDTH = RWKV_HEADS * RWKV_HEAD_DIM
DECAY_RANK = 64
AAA_RANK = 64
GATE_RANK = 128
GN_EPS = 64e-5
MEM_HEADS = 4
MEM_HEAD_DIM = 128
PEER_HEADS = 8
PEER_N_KEYS = 128
PEER_HALF = 128
PEER_TOPK = 16
LN_EPS = 1e-5
RMS_EPS = 1e-6

LANES = 128
VMEM_LIMIT = 56 * 1024 * 1024

INT_MIN = -(2 ** 31)
NEG_INF = float("-inf")


def _cparams(sem, vmem=VMEM_LIMIT):
    return pltpu.CompilerParams(dimension_semantics=sem, vmem_limit_bytes=vmem)


def _dot(a, b):
    return jnp.dot(a, b, preferred_element_type=F32)


def _dot_nt(a, b):
    return lax.dot_general(a, b, (((1,), (1,)), ((), ())), preferred_element_type=F32)


def _split2(x):
    hi = x.astype(BF16)
    lo = (x - hi.astype(F32)).astype(BF16)
    return hi, lo


def _dot3(a, b, nt=False):
    d = _dot_nt if nt else _dot
    ah, al = _split2(a)
    bh, bl = _split2(b)
    return d(ah, bh) + (d(ah, bl) + d(al, bh))


def _linear_kernel(x_ref, w_ref, o_ref, *, act):
    y = _dot(x_ref[...].astype(BF16), w_ref[...])
    if act == "sigmoid":
        y = jax.nn.sigmoid(y)
    o_ref[...] = y.astype(o_ref.dtype)


def _linear(x, w, out_dtype, tm, tn, act=None, name="linear"):
    m, k = x.shape
    n = w.shape[1]
    return pl.pallas_call(
        functools.partial(_linear_kernel, act=act),
        out_shape=jax.ShapeDtypeStruct((m, n), out_dtype),
        grid=(m // tm, n // tn),
        in_specs=[pl.BlockSpec((tm, k), lambda i, j: (i, 0)),
                  pl.BlockSpec((k, tn), lambda i, j: (0, j))],
        out_specs=pl.BlockSpec((tm, tn), lambda i, j: (i, j)),
        compiler_params=_cparams(("parallel", "parallel")),
        name=name,
    )(x, w)


def _dsa_prep_kernel(dc_ref, qg_ref, kvg_ref, wuq_ref, wuk_ref, wiq_ref,
                     ql_ref, qi_ref, wi_ref, ki_ref, kv_ref):
    dc = dc_ref[0]
    c_q = dc[:, :DSA_Q_RANK]
    c_kv = dc[:, DSA_Q_RANK:DSA_Q_RANK + DSA_KV_RANK]
    off = DSA_Q_RANK + DSA_KV_RANK
    k_idx = dc[:, off:off + IDX_DIM]
    w_idx = dc[:, off + IDX_DIM:off + IDX_DIM + IDX_HEADS]
    c_q = c_q * lax.rsqrt(jnp.mean(c_q * c_q, axis=-1, keepdims=True) + RMS_EPS) * qg_ref[...]
    c_kv = c_kv * lax.rsqrt(jnp.mean(c_kv * c_kv, axis=-1, keepdims=True) + RMS_EPS) * kvg_ref[...]
    cqb = c_q.astype(BF16)
    scale = DSA_HEAD_DIM ** -0.5
    for h in range(DSA_HEADS):
        q_h = _dot(cqb, wuq_ref[h])
        ql_ref[0, h] = (_dot(q_h.astype(BF16), wuk_ref[h]) * scale).astype(BF16)
        qi_ref[0, h] = _dot(cqb, wiq_ref[h]).astype(BF16)
    wi_ref[0] = w_idx * (IDX_HEADS ** -0.5 * IDX_DIM ** -0.5)
    ki_ref[0] = k_idx.astype(BF16)
    kv_ref[0] = c_kv.astype(BF16)


def _dsa_prep(dcols, q_norm_g, kv_norm_g, w_uq, w_uk, w_idx_q, tm):
    b, s, wd = dcols.shape
    h = DSA_HEADS
    wuq_h = w_uq.reshape(DSA_Q_RANK, h, DSA_HEAD_DIM).transpose(1, 0, 2).astype(BF16)
    wuk_h = w_uk.transpose(1, 2, 0).astype(BF16)
    wiq_h = w_idx_q.reshape(DSA_Q_RANK, IDX_HEADS, IDX_DIM).transpose(1, 0, 2).astype(BF16)
    full = lambda shape: pl.BlockSpec(shape, lambda i, j: (0,) * len(shape))
    return pl.pallas_call(
        _dsa_prep_kernel,
        out_shape=(jax.ShapeDtypeStruct((b, h, s, DSA_KV_RANK), BF16),
                   jax.ShapeDtypeStruct((b, IDX_HEADS, s, IDX_DIM), BF16),
                   jax.ShapeDtypeStruct((b, s, IDX_HEADS), F32),
                   jax.ShapeDtypeStruct((b, s, IDX_DIM), BF16),
                   jax.ShapeDtypeStruct((b, s, DSA_KV_RANK), BF16)),
        grid=(b, s // tm),
        in_specs=[pl.BlockSpec((1, tm, wd), lambda i, j: (i, j, 0)),
                  full((1, DSA_Q_RANK)), full((1, DSA_KV_RANK)),
                  full(wuq_h.shape), full(wuk_h.shape), full(wiq_h.shape)],
        out_specs=(pl.BlockSpec((1, h, tm, DSA_KV_RANK), lambda i, j: (i, 0, j, 0)),
                   pl.BlockSpec((1, IDX_HEADS, tm, IDX_DIM), lambda i, j: (i, 0, j, 0)),
                   pl.BlockSpec((1, tm, IDX_HEADS), lambda i, j: (i, j, 0)),
                   pl.BlockSpec((1, tm, IDX_DIM), lambda i, j: (i, j, 0)),
                   pl.BlockSpec((1, tm, DSA_KV_RANK), lambda i, j: (i, j, 0))),
        compiler_params=_cparams(("parallel", "parallel")),
        name="dsa_prep",
    )(dcols, q_norm_g.reshape(1, -1), kv_norm_g.reshape(1, -1), wuq_h, wuk_h, wiq_h)


DSA_QB = 128
DSA_KB = 128
DSA_SC = 512


def _t5_bias_tile(rel_ref, h, off):
    row = lax.broadcasted_iota(I32, (DSA_QB, DSA_KB), 0)
    col = lax.broadcasted_iota(I32, (DSA_QB, DSA_KB), 1)
    n = jnp.maximum(row - col + off, 0)
    max_exact = NUM_BUCKETS // 2
    nf = jnp.maximum(n, max_exact).astype(F32)
    large = max_exact + jnp.floor(jnp.log(nf / max_exact) / math.log(MAX_DISTANCE / max_exact)
                                  * (NUM_BUCKETS - max_exact)).astype(I32)
    large = jnp.minimum(large, NUM_BUCKETS - 1)
    bucket = jnp.where(n < max_exact, n, large)
    last = rel_ref[NUM_BUCKETS - 1, h]
    out = jnp.zeros((DSA_QB, DSA_KB), F32)
    for bk in range(NUM_BUCKETS - 1):
        out = jnp.where(bucket == bk, rel_ref[bk, h] - last, out)
    return out


def _fold_lanes(x, op):
    out = x[..., :LANES]
    for c in range(1, x.shape[-1] // LANES):
        out = op(out, x[..., c * LANES:(c + 1) * LANES])
    return out


def _dsa_attn_kernel(rel_ref, qi_ref, wi_ref, ql_ref, ki_ref, kv_ref, wuv_ref, o_ref,
                     key_sc, bias_sc, mx_sc, ref_sc, ls_sc, acc_sc, *, topk, seq):
    H = DSA_HEADS
    QB, KB, SC = DSA_QB, DSA_KB, DSA_SC
    qb = pl.program_id(1)
    q0 = qb * QB

    @pl.when((pl.program_id(0) == 0) & (qb == 0))
    def _():
        for h in range(H):
            bias_sc[0, h] = _t5_bias_tile(rel_ref, h, 0)
            bias_sc[1, h] = _t5_bias_tile(rel_ref, h, KB)

    n_sc = (q0 + QB + SC - 1) // SC
    wi = wi_ref[0]
    HALF = SC // 2
    row_g = q0 + lax.broadcasted_iota(I32, (QB, HALF), 0)
    col_h = lax.broadcasted_iota(I32, (QB, HALF), 1)

    def score_body(j, carry):
        for c in range(2):
            k0 = pl.multiple_of(j * SC + c * HALF, HALF)
            ks = ki_ref[0, pl.ds(k0, HALF), :]
            acc = jnp.zeros((QB, HALF), F32)
            for h in range(IDX_HEADS):
                lg = _dot_nt(qi_ref[0, h], ks)
                acc = acc + wi[:, h:h + 1] * jnp.maximum(lg, 0.0)
            key_sc[:, pl.ds(k0, HALF)] = jnp.where(col_h + k0 <= row_g, acc, NEG_INF)
        return carry

    lax.fori_loop(0, n_sc, score_body, 0)

    col_l = lax.broadcasted_iota(I32, (QB, LANES), 1)

    def count(ind):
        def body(j, acc):
            for c in range(SC // LANES):
                k0 = pl.multiple_of(j * SC + c * LANES, LANES)
                acc = acc + ind(key_sc[:, pl.ds(k0, LANES)], k0)
            return acc
        acc = lax.fori_loop(0, n_sc, body, jnp.zeros((QB, LANES), I32))
        return jnp.sum(acc, axis=1, keepdims=True)

    def code_to_float(ru):
        c = ru ^ INT_MIN
        return pltpu.bitcast(c ^ ((c >> 31) & 0x7FFFFFFF), jnp.float32)

    HB = QB // 2

    def descent(n):
        def count_half(r0, ru, bit):
            cand = code_to_float(ru | bit)
            acc = jnp.zeros((HB, LANES), I32)
            for c in range(n * SC // LANES):
                acc = acc + jnp.where(key_sc[r0:r0 + HB, c * LANES:(c + 1) * LANES] >= cand, 1, 0)
            return acc

        def update(state, acc, bit):
            ru, cnt_ru = state
            cnt = jnp.sum(acc, axis=1, keepdims=True, dtype=I32)
            ok = cnt >= topk
            return jnp.where(ok, ru | bit, ru), jnp.where(ok, cnt, cnt_ru)

        def run():
            def bis_body(i, carry):
                st_a, st_b, acc_b = carry
                sh = lax.convert_element_type(31 - i, I32)
                bit = lax.shift_left(jnp.int32(1), sh)
                bit_b = jnp.where(sh == 31, 0, lax.shift_left(jnp.int32(2), jnp.minimum(sh, 30)))
                st_b = update(st_b, acc_b, bit_b)
                st_a = update(st_a, count_half(0, st_a[0], bit), bit)
                return st_a, st_b, count_half(HB, st_b[0], bit)

            zero = jnp.minimum(lax.broadcasted_iota(I32, (HB, LANES), 0), 0)
            st_a, st_b, acc_b = lax.fori_loop(0, 32, bis_body, ((zero, zero), (zero, zero), zero))
            st_b = update(st_b, acc_b, 1)
            return (jnp.concatenate([st_a[0], st_b[0]], axis=0), jnp.concatenate([st_a[1], st_b[1]], axis=0))
        return run

    skip = lambda: (jnp.zeros((QB, LANES), I32), jnp.zeros((QB, LANES), I32))
    ru, cnt_ge = lax.switch(jnp.where(q0 + QB <= topk, 0, n_sc),
                            [skip] + [descent(n) for n in range(1, seq // SC + 1)])
    ru = ru[:, :1]
    cnt_ge = cnt_ge[:, :1]
    few = q0 + lax.broadcasted_iota(I32, (QB, 1), 0) < topk
    tau = jnp.where(few, jnp.finfo(jnp.float32).min, code_to_float(ru))
    tau_b = jnp.broadcast_to(tau, (QB, LANES))

    @pl.when(jnp.max(cnt_ge) > topk)
    def _():
        cnt_gt = count(lambda blk, k0: jnp.where(blk > tau_b, 1, 0))
        need = topk - cnt_gt

        def pos_body(_, lohi):
            lo, hi = lohi
            mid = (lo + hi) >> 1
            mid_b = jnp.broadcast_to(mid, (QB, LANES))
            f = count(lambda blk, k0: jnp.where(blk == tau_b, jnp.where(col_l + k0 <= mid_b, 1, 0), 0))
            ok = f >= need
            return jnp.where(ok, lo, mid + 1), jnp.where(ok, mid, hi)

        lo, _hi = lax.fori_loop(0, max(1, (seq - 1).bit_length()), pos_body,
                                (jnp.zeros((QB, 1), I32), jnp.full((QB, 1), seq - 1, I32)))
        jstar = jnp.broadcast_to(lo, (QB, LANES))

        def fix_body(j, carry):
            for c in range(SC // LANES):
                k0 = pl.multiple_of(j * SC + c * LANES, LANES)
                blk = key_sc[:, pl.ds(k0, LANES)]
                dropped = jnp.where(col_l + k0 > jstar, NEG_INF, blk)
                key_sc[:, pl.ds(k0, LANES)] = jnp.where(blk == tau_b, dropped, blk)
            return carry

        lax.fori_loop(0, n_sc, fix_body, 0)

    ql = ql_ref[0].reshape(H * QB, DSA_KV_RANK)
    near0 = jnp.maximum(qb - 1, 0) * KB
    n_far = (near0 + SC - 1) // SC
    tau_s = jnp.broadcast_to(tau, (QB, SC))
    tau_k = jnp.broadcast_to(tau, (QB, KB))
    col_s = lax.broadcasted_iota(I32, (QB, SC), 1)

    def far_logits(j):
        k0 = pl.multiple_of(j * SC, SC)
        kvb = kv_ref[0, pl.ds(k0, SC), :]
        blk = key_sc[:, pl.ds(k0, SC)]
        madd = jnp.where(blk >= tau_s, jnp.where(col_s + k0 < near0, 0.0, NEG_INF), NEG_INF)
        return _dot_nt(ql, kvb).reshape(H, QB, SC) + madd[None], kvb

    def near_logits(jb, bias):
        k0 = pl.multiple_of(jb * KB, KB)
        kvb = kv_ref[0, pl.ds(k0, KB), :]
        madd = jnp.where(key_sc[:, pl.ds(k0, KB)] >= tau_k, 0.0, NEG_INF)
        return _dot_nt(ql, kvb).reshape(H, QB, KB) + (bias + madd[None]), kvb

    SHIFT_SLACK = 60.0
    row_k = lax.broadcasted_iota(I32, (QB, KB), 0)
    col_k = lax.broadcasted_iota(I32, (QB, KB), 1)
    causal_add = jnp.where(col_k <= row_k, 0.0, NEG_INF)

    def near_raw(jb, bias):
        k0 = pl.multiple_of(jb * KB, KB)
        return _dot_nt(ql, kv_ref[0, pl.ds(k0, KB), :]).reshape(H, QB, KB) + bias

    ref_sc[...] = near_raw(qb, bias_sc[0]) + causal_add[None]

    @pl.when(qb >= 1)
    def _():
        ref_sc[...] = jnp.maximum(ref_sc[...], near_raw(qb - 1, bias_sc[1]))

    ref_sc[...] = jnp.broadcast_to(jnp.max(ref_sc[...], axis=-1, keepdims=True), ref_sc.shape)

    def run_pass(track_max):
        ls_sc[...] = jnp.zeros(ls_sc.shape, F32)
        acc_sc[...] = jnp.zeros(acc_sc.shape, F32)
        if track_max:
            mx_sc[...] = jnp.full(mx_sc.shape, NEG_INF, F32)

        def far_acc(j, carry):
            s, kvb = far_logits(j)
            if track_max:
                mx_sc[...] = jnp.maximum(mx_sc[...], _fold_lanes(s, jnp.maximum))
            m_b = ref_sc[...]
            p = jnp.concatenate([jnp.exp(s[..., c * LANES:(c + 1) * LANES] - m_b)
                                 for c in range(SC // LANES)], axis=-1)
            ls_sc[...] += _fold_lanes(p, jnp.add)
            acc_sc[...] += _dot(p.reshape(H * QB, SC).astype(BF16), kvb).reshape(H, QB, DSA_KV_RANK)
            return carry

        lax.fori_loop(0, n_far, far_acc, 0)

        def near_acc(jb, bias):
            s, kvb = near_logits(jb, bias)
            if track_max:
                mx_sc[...] = jnp.maximum(mx_sc[...], s)
            p = jnp.exp(s - ref_sc[...])
            ls_sc[...] += p
            acc_sc[...] += _dot(p.reshape(H * QB, KB).astype(BF16), kvb).reshape(H, QB, DSA_KV_RANK)

        @pl.when(qb >= 1)
        def _():
            near_acc(qb - 1, bias_sc[1])

        near_acc(qb, bias_sc[0])

    run_pass(True)
    mx = jnp.max(mx_sc[...], axis=-1, keepdims=True)

    @pl.when(jnp.max(jnp.abs(mx - ref_sc[:, :, 0:1])) > SHIFT_SLACK)
    def _():
        ref_sc[...] = jnp.broadcast_to(mx, ref_sc.shape)
        run_pass(False)

    o_lat = acc_sc[...] / jnp.sum(ls_sc[...], axis=-1, keepdims=True)
    y = jnp.zeros((QB, DSA_HEADS * DSA_HEAD_DIM), F32)
    for h in range(H):
        y = y + _dot(o_lat[h].astype(BF16), wuv_ref[h])
    o_ref[0] = y.astype(o_ref.dtype)


def _dsa_attn(rel_bias, q_idx, w_idx, q_lat, k_idx, c_kv, w_uv):
    b, h, s, _ = q_lat.shape
    topk = min(INDEX_TOPK_MAX, s // 4)
    width = DSA_HEADS * DSA_HEAD_DIM
    eye = jnp.eye(DSA_HEADS, dtype=F32)
    wuv_e = jnp.einsum("chd,hg->hcgd", w_uv, eye).reshape(DSA_HEADS, DSA_KV_RANK, width).astype(BF16)
    QB = DSA_QB
    assert s % DSA_SC == 0
    return pl.pallas_call(
        functools.partial(_dsa_attn_kernel, topk=topk, seq=s),
        out_shape=jax.ShapeDtypeStruct((b, s, width), BF16),
        grid=(b, s // QB),
        in_specs=[pl.BlockSpec(memory_space=pltpu.SMEM),
                  pl.BlockSpec((1, IDX_HEADS, QB, IDX_DIM), lambda i, j: (i, 0, j, 0)),
                  pl.BlockSpec((1, QB, IDX_HEADS), lambda i, j: (i, j, 0)),
                  pl.BlockSpec((1, h, QB, DSA_KV_RANK), lambda i, j: (i, 0, j, 0)),
                  pl.BlockSpec((1, s, IDX_DIM), lambda i, j: (i, 0, 0)),
                  pl.BlockSpec((1, s, DSA_KV_RANK), lambda i, j: (i, 0, 0)),
                  pl.BlockSpec(wuv_e.shape, lambda i, j: (0, 0, 0))],
        out_specs=pl.BlockSpec((1, QB, width), lambda i, j: (i, j, 0)),
        scratch_shapes=[pltpu.VMEM((QB, s), F32),
                        pltpu.VMEM((2, h, QB, DSA_KB), F32),
                        pltpu.VMEM((h, QB, LANES), F32),
                        pltpu.VMEM((h, QB, LANES), F32),
                        pltpu.VMEM((h, QB, LANES), F32),
                        pltpu.VMEM((h, QB, DSA_KV_RANK), F32)],
        compiler_params=_cparams(("arbitrary", "arbitrary")),
        name="dsa_attn",
    )(rel_bias, q_idx, w_idx, q_lat, k_idx, c_kv, wuv_e)


def _rwkv_prep_kernel(c_ref, p_ref, mu_ref, w0_ref, a0_ref, kk_ref, ka_ref, rk_ref,
                      wwa_ref, g2_ref, ones_ref,
                      r_o, k_o, v_o, kk_o, b_o, lw_o, bonus_o, g_o, *, tiles_per_seq):
    W = RWKV_WIDTH
    cols = c_ref[...]
    first = (pl.program_id(0) % tiles_per_seq) == 0
    prev_row = jnp.where(first, 0.0, p_ref[7:8, :])
    rolled = pltpu.roll(cols, 1, 0)
    row = lax.broadcasted_iota(I32, cols.shape, 0)
    prev = jnp.where(row == 0, prev_row, rolled)
    xs = cols + (prev - cols) * mu_ref[...]
    r = xs[:, 0:W]
    k = xs[:, W:2 * W]
    v = xs[:, 2 * W:3 * W]
    lora = xs[:, 3 * W:3 * W + DECAY_RANK + AAA_RANK]
    gl = xs[:, 3 * W + DECAY_RANK + AAA_RANK:]
    lane = lax.broadcasted_iota(I32, lora.shape, 1)
    lora = jnp.where(lane < DECAY_RANK, jnp.tanh(lora), lora)
    wa = _dot(lora.astype(BF16), wwa_ref[...])
    w = -jax.nn.softplus(-(w0_ref[...] + wa[:, :W])) - 0.5
    a = jax.nn.sigmoid(a0_ref[...] + wa[:, W:])
    ones_bd = ones_ref[...]

    def head_sum(t):
        hi, lo = _split2(t)
        return _dot(hi, ones_bd) + _dot(lo, ones_bd)

    kk = k * kk_ref[...]
    kk = kk / jnp.maximum(jnp.sqrt(head_sum(kk * kk)), 1e-12)
    k2 = k * (1.0 + (a - 1.0) * ka_ref[...])
    r_o[...] = r
    k_o[...] = k2
    v_o[...] = v
    kk_o[...] = kk
    b_o[...] = kk * a
    lw_o[...] = -jnp.exp(w)
    bonus_o[...] = head_sum(r * k2 * rk_ref[...]) * v
    g_o[...] = _dot(jax.nn.sigmoid(gl).astype(BF16), g2_ref[...])


def _rwkv_prep(cols, s, mu, w0, w2, a0, a2, g2, k_k, k_a, r_k, tm):
    t, wc = cols.shape
    W = RWKV_WIDTH
    wwa = jnp.zeros((DECAY_RANK + AAA_RANK, 2 * W), F32)
    wwa = wwa.at[:DECAY_RANK, :W].set(w2).at[DECAY_RANK:, W:].set(a2).astype(BF16)
    head_id = jnp.arange(W) // RWKV_HEAD_DIM
    ones_bd = (head_id[:, None] == head_id[None, :]).astype(BF16)
    row = lambda a: a.reshape(1, -1)
    vec = lambda n: pl.BlockSpec((1, n), lambda i: (0, 0))
    tok = jax.ShapeDtypeStruct((t, W), F32)
    tok_spec = pl.BlockSpec((tm, W), lambda i: (i, 0))
    return pl.pallas_call(
        functools.partial(_rwkv_prep_kernel, tiles_per_seq=s // tm),
        out_shape=(tok,) * 8,
        grid=(t // tm,),
        in_specs=[pl.BlockSpec((tm, wc), lambda i: (i, 0)),
                  pl.BlockSpec((8, wc), lambda i: (jnp.maximum(i * (tm // 8) - 1, 0), 0)),
                  vec(wc), vec(W), vec(W), vec(W), vec(W), vec(W),
                  pl.BlockSpec(wwa.shape, lambda i: (0, 0)),
                  pl.BlockSpec((GATE_RANK, W), lambda i: (0, 0)),
                  pl.BlockSpec((W, W), lambda i: (0, 0))],
        out_specs=(tok_spec,) * 8,
        compiler_params=_cparams(("parallel",)),
        name="rwkv_prep",
    )(cols, cols, row(mu), row(w0), row(a0), row(k_k), row(k_a), row(r_k.reshape(-1)),
      wwa, g2.astype(BF16), ones_bd)


RWKV_CHUNK = 64
RWKV_GROUP = 16


def _rwkv_chunk_kernel(r_ref, k_ref, v_ref, kk_ref, b_ref, lw_ref, bonus_ref, g_ref,
                       lg_ref, lb_ref, o_ref, st_sc):
    C = RWKV_CHUNK
    N = RWKV_HEAD_DIM
    P = 2 * N
    nb = r_ref.shape[0]
    npair = RWKV_WIDTH // P

    @pl.when(pl.program_id(0) == 0)
    def _():
        st_sc[...] = jnp.zeros(st_sc.shape, F32)

    row_c = lax.broadcasted_iota(I32, (C, P), 0)
    lane_c = lax.broadcasted_iota(I32, (C, P), 1)
    tcol = lane_c & (N - 1)
    strict = tcol < row_c
    incl = tcol <= row_c
    lane_lo_n = lax.broadcasted_iota(I32, (N, P), 1) < N
    r2 = lax.broadcasted_iota(I32, (P, P), 0)
    l2 = lax.broadcasted_iota(I32, (P, P), 1)
    bdmask = (r2 < N) == (l2 < N)
    ones_bd = jnp.where(bdmask, 1.0, 0.0).astype(BF16)
    ones2 = jnp.concatenate([ones_bd, ones_bd], axis=0)
    ti = lax.broadcasted_iota(I32, (C, C), 0)
    si = lax.broadcasted_iota(I32, (C, C), 1)
    tri = jnp.where(si <= ti, 1.0, 0.0).astype(BF16)
    tri3 = jnp.concatenate([tri, tri, tri], axis=1)

    def bd(x):
        return jnp.where(bdmask, jnp.concatenate([x, x], axis=0), 0.0).astype(BF16)

    def head_mean(t):
        hi, lo = _split2(t)
        return _dot(jnp.concatenate([hi, lo], axis=1), ones2) * (1.0 / N)

    units = [(bi, p) for bi in range(nb) for p in range(npair)]
    nlev = max(1, (C - 1).bit_length())
    for g0 in range(0, len(units), RWKV_GROUP):
        grp = units[g0:g0 + RWKV_GROUP]
        n = len(grp)
        ld = lambda ref: [ref[bi, :, p * P:(p + 1) * P] for bi, p in grp]
        r, k, v, kk, b, lw = ld(r_ref), ld(k_ref), ld(v_ref), ld(kk_ref), ld(b_ref), ld(lw_ref)

        cum = []
        for u in range(n):
            hi = lw[u].astype(BF16)
            r1 = lw[u] - hi.astype(F32)
            mid = r1.astype(BF16)
            lo = (r1 - mid.astype(F32)).astype(BF16)
            cum.append(_dot(tri3, jnp.concatenate([hi, mid, lo], axis=0)))
        cum_last = [c[C - 1:C, :] for c in cum]
        p_inv = [jnp.exp(-c) for c in cum]
        a_t = [-kk[u] * jnp.exp(cum[u] - lw[u]) for u in range(n)]
        r_t = [r[u] * jnp.exp(cum[u]) for u in range(n)]
        dec = [jnp.exp(cum_last[u] - cum[u]) for u in range(n)]
        ar = [jnp.concatenate([a_t[u], r_t[u]], axis=0).astype(BF16) for u in range(n)]
        sb = [_dot_nt(ar[u], bd(b[u] * p_inv[u])) for u in range(n)]
        sk = [_dot_nt(ar[u], bd(k[u] * p_inv[u])) for u in range(n)]
        bd_v = [bd(v[u]) for u in range(n)]
        lp = [jnp.where(strict, sb[u][:C], 0.0) for u in range(n)]
        l_ak = [jnp.where(strict, sk[u][:C], 0.0).astype(BF16) for u in range(n)]
        m_rb = [jnp.where(incl, sb[u][C:], 0.0).astype(BF16) for u in range(n)]
        m_rk = [jnp.where(incl, sk[u][C:], 0.0).astype(BF16) for u in range(n)]

        xa = list(a_t)
        xu = [_dot(l_ak[u], bd_v[u]) for u in range(n)]
        for lev in range(nlev):
            last = lev == nlev - 1
            for u in range(n):
                parts = [bd(xa[u]), bd(xu[u])] + ([] if last else [bd(lp[u])])
                res = _dot(lp[u].astype(BF16), jnp.concatenate(parts, axis=1))
                xa[u] = xa[u] + res[:, :P]
                xu[u] = xu[u] + res[:, P:2 * P]
                if not last:
                    lp[u] = res[:, 2 * P:]

        res = [_dot(m_rb[u], jnp.concatenate([bd(xa[u]), bd(xu[u])], axis=1)) for u in range(n)]
        r_hat = [r_t[u] + res[u][:, :P] for u in range(n)]
        y0 = [res[u][:, P:] + _dot(m_rk[u], bd_v[u]) for u in range(n)]
        zt = [jnp.concatenate([b[u] * dec[u], k[u] * dec[u]], axis=0).T.astype(BF16) for u in range(n)]
        pct = [jnp.broadcast_to(cum_last[u], (P, P)).T for u in range(n)]
        pcm = [jnp.exp(jnp.where(lane_lo_n, pct[u][:N], pct[u][N:])) for u in range(n)]

        y = []
        for u, (bi, p) in enumerate(grp):
            st = st_sc[bi, p]
            ws = _dot(jnp.concatenate([xa[u], r_hat[u]], axis=0).astype(BF16), bd(st))
            wc = ws[:C] + xu[u]
            y.append(ws[C:] + y0[u])
            full = _dot(zt[u], jnp.concatenate([wc, v[u]], axis=0).astype(BF16))
            st_sc[bi, p] = pcm[u] * st + jnp.where(lane_lo_n, full[:N], full[N:])

        for u, (bi, p) in enumerate(grp):
            sl = slice(p * P, (p + 1) * P)
            yc = y[u] - head_mean(y[u])
            yn = yc * lax.rsqrt(head_mean(yc * yc) + GN_EPS) * lg_ref[:, sl] + lb_ref[:, sl]
            o_ref[bi, :, sl] = ((yn + bonus_ref[bi, :, sl]) * g_ref[bi, :, sl]).astype(o_ref.dtype)


def _rwkv_chunk(r, k, v, kk, bvec, lw, bonus, g, lnx_g, lnx_b):
    b, s, w = r.shape
    C = RWKV_CHUNK
    spec = pl.BlockSpec((b, C, w), lambda j: (0, j, 0))
    pspec = pl.BlockSpec((1, w), lambda j: (0, 0))
    return pl.pallas_call(
        _rwkv_chunk_kernel,
        out_shape=jax.ShapeDtypeStruct((b, s, w), BF16),
        grid=(s // C,),
        in_specs=[spec] * 8 + [pspec, pspec],
        out_specs=spec,
        scratch_shapes=[pltpu.VMEM((b, w // (2 * RWKV_HEAD_DIM), RWKV_HEAD_DIM, 2 * RWKV_HEAD_DIM), F32)],
        compiler_params=_cparams(("arbitrary",)),
        name="rwkv_chunk",
    )(r, k, v, kk, bvec, lw, bonus, g, lnx_g.reshape(1, w), lnx_b.reshape(1, w))


def _mem_attn_kernel(q_ref, kv_ref, o_ref):
    W = MEM_HEADS * MEM_HEAD_DIM
    scale = MEM_HEAD_DIM ** -0.5
    for h in range(MEM_HEADS):
        sl = slice(h * MEM_HEAD_DIM, (h + 1) * MEM_HEAD_DIM)
        q = q_ref[0, :, sl]
        k = kv_ref[0, :, sl]
        v = kv_ref[0, :, W + h * MEM_HEAD_DIM:W + (h + 1) * MEM_HEAD_DIM]
        s = _dot_nt(q, k) * scale
        s = s - jnp.max(s, axis=-1, keepdims=True)
        p = jnp.exp(s)
        p = p / jnp.sum(p, axis=-1, keepdims=True)
        o_ref[0, :, sl] = _dot(p.astype(BF16), v).astype(o_ref.dtype)


def _mem_attn(q, kv, tq):
    b, s, w = q.shape
    m = kv.shape[1]
    return pl.pallas_call(
        _mem_attn_kernel,
        out_shape=jax.ShapeDtypeStruct((b, s, w), BF16),
        grid=(b, s // tq),
        in_specs=[pl.BlockSpec((1, tq, w), lambda i, j: (i, j, 0)),
                  pl.BlockSpec((1, m, 2 * w), lambda i, j: (i, 0, 0))],
        out_specs=pl.BlockSpec((1, tq, w), lambda i, j: (i, j, 0)),
        compiler_params=_cparams(("parallel", "parallel")),
        name="mem_attn",
    )(q, kv)


def _layernorm(z, g, b):
    mu = jnp.mean(z, axis=-1, keepdims=True)
    zc = z - mu
    var = jnp.mean(zc * zc, axis=-1, keepdims=True)
    return zc * lax.rsqrt(var + LN_EPS) * g + b


def _merge_kernel(x_ref, yd_ref, yr_ref, ym_ref, gt_ref, wd_ref, wr_ref, wm_ref, wo_ref,
                  g_ref, b_ref, x1_ref, x1b_ref, x1t_ref, *, alpha):
    d = x_ref.shape[-1]
    br_d = _dot(yd_ref[0], wd_ref[...])
    br_r = _dot(yr_ref[0], wr_ref[...])
    br_m = _dot(ym_ref[0], wm_ref[...])
    gt = gt_ref[0]
    merged = (gt[:, 0:d].astype(F32) * br_d + gt[:, d:2 * d].astype(F32) * br_r
              + gt[:, 2 * d:3 * d].astype(F32) * br_m)
    z = alpha * x_ref[0] + _dot(merged.astype(BF16), wo_ref[...])
    x1 = _layernorm(z, g_ref[...], b_ref[...])
    x1_ref[0] = x1
    x1b_ref[0] = x1.astype(BF16)
    x1t_ref[0] = x1.T.astype(BF16)


def _merge(x, y_dsa, y_rwkv, y_mem, gates, w_br_dsa, w_br_rwkv, w_br_mem, w_out, ln_g, ln_b, alpha, tm):
    b, s, d = x.shape
    full2 = lambda a: pl.BlockSpec(a.shape, lambda i, j: (0,) * a.ndim)
    wd = w_br_dsa.astype(BF16)
    wr = w_br_rwkv.astype(BF16)
    wm = w_br_mem.astype(BF16)
    wo = w_out.astype(BF16)
    g2 = ln_g.reshape(1, d)
    b2 = ln_b.reshape(1, d)
    tok = lambda w: pl.BlockSpec((1, tm, w), lambda i, j: (i, j, 0))
    return pl.pallas_call(
        functools.partial(_merge_kernel, alpha=alpha),
        out_shape=(jax.ShapeDtypeStruct((b, s, d), F32),
                   jax.ShapeDtypeStruct((b, s, d), BF16),
                   jax.ShapeDtypeStruct((b, d, s), BF16)),
        grid=(b, s // tm),
        in_specs=[tok(d), tok(y_dsa.shape[-1]), tok(y_rwkv.shape[-1]), tok(y_mem.shape[-1]), tok(3 * d),
                  full2(wd), full2(wr), full2(wm), full2(wo), full2(g2), full2(b2)],
        out_specs=(tok(d), tok(d), pl.BlockSpec((1, d, tm), lambda i, j: (i, 0, j))),
        compiler_params=_cparams(("parallel", "parallel")),
        name="merge",
    )(x, y_dsa, y_rwkv, y_mem, gates, wd, wr, wm, wo, g2, b2)


def _sort_network(n):
    pairs = []

    def merge(lo, cnt, r):
        step = r * 2
        if step < cnt:
            merge(lo, cnt, step)
            merge(lo + r, cnt, step)
            for i in range(lo + r, lo + cnt - r, step):
                pairs.append((i, i + r))
        else:
            pairs.append((lo, lo + r))

    def sort(lo, cnt):
        if cnt > 1:
            half = cnt // 2
            sort(lo, half)
            sort(lo + half, half)
            merge(lo, cnt, 1)

    sort(0, n)
    return pairs


SUBLANES = 8
PEER_ROUTE_GROUP = 512
PEER_SLABS = PEER_N_KEYS // SUBLANES
PEER_SORT_PAIRS = _sort_network(PEER_SLABS)
PEER_CAND = [(i, j) for i in range(PEER_TOPK) for j in range(PEER_TOPK // (i + 1))]
PEER_CAND += [None] * (-len(PEER_CAND) % SUBLANES)


def _top_sorted(sub, k):
    slabs = [sub[SUBLANES * s:SUBLANES * (s + 1), :] for s in range(PEER_SLABS)]
    for i, j in PEER_SORT_PAIRS:
        slabs[i], slabs[j] = jnp.maximum(slabs[i], slabs[j]), jnp.minimum(slabs[i], slabs[j])
    tops = []
    for r in range(k):
        m = jnp.max(slabs[0], axis=0, keepdims=True)
        tops.append(m)
        if r < k - 1:
            hit = slabs[0] == m
            for i in range(k - 1 - r):
                slabs[i] = jnp.where(hit, slabs[i + 1], slabs[i])
    return tops


def _prefix_count(pred, rows):
    full = pred(rows[15])
    c1 = pred(rows[7])
    c2 = pred(jnp.where(c1, rows[11], rows[3]))
    c3 = pred(jnp.where(c1, jnp.where(c2, rows[13], rows[9]), jnp.where(c2, rows[5], rows[1])))
    c4 = pred(jnp.where(c1,
                        jnp.where(c2, jnp.where(c3, rows[14], rows[12]), jnp.where(c3, rows[10], rows[8])),
                        jnp.where(c2, jnp.where(c3, rows[6], rows[4]), jnp.where(c3, rows[2], rows[0]))))
    cnt = (jnp.where(c1, 8.0, 0.0) + jnp.where(c2, 4.0, 0.0)) + (jnp.where(c3, 2.0, 0.0) + jnp.where(c4, 1.0, 0.0))
    return jnp.where(full, 16.0, cnt)


def _peer_route_kernel(x_ref, wq_ref, keys_ref, r1_o, c1_o, rk2_o, p2_o):
    tt = x_ref.shape[1]
    K = PEER_TOPK
    q = _dot(x_ref[0], wq_ref[...])
    keys0 = keys_ref[0]
    keys1 = keys_ref[1]
    TG = min(tt, PEER_ROUTE_GROUP)
    sub_id = lax.broadcasted_iota(I32, (SUBLANES, TG), 0)

    def route(h, ts, s1, s2):
        a1 = _top_sorted(s1, K)
        a2 = _top_sorted(s2, K)
        best = a1[0] + a2[0]
        cand = []
        for g in range(len(PEER_CAND) // SUBLANES):
            slab = jnp.full((SUBLANES, TG), NEG_INF, F32)
            for s, ij in enumerate(PEER_CAND[SUBLANES * g:SUBLANES * (g + 1)]):
                if ij is not None:
                    slab = jnp.where(sub_id == s, a1[ij[0]] + a2[ij[1]], slab)
            cand.append(slab)
        zsum = jnp.zeros((1, TG), F32)
        v_k = best
        for rnk in range(K):
            m = cand[0]
            for slab in cand[1:]:
                m = jnp.maximum(m, slab)
            v_k = jnp.max(m, axis=0, keepdims=True)
            zsum = zsum + jnp.exp(v_k - best)
            if rnk < K - 1:
                cand = [jnp.where(slab == v_k, NEG_INF, slab) for slab in cand]
        r1_o[0, h, :, ts] = _prefix_count(lambda t: s1 + t >= v_k, a2)
        c1_o[0, h, :, ts] = jnp.exp(s1 - a1[0]) / zsum
        rk2_o[0, h, :, ts] = _prefix_count(lambda t: t > s2, a2).astype(BF16)
        p2_o[0, h, :, ts] = jnp.exp(s2 - a2[0]).astype(BF16)

    for h in range(PEER_HEADS):
        q1 = q[:, (2 * h) * PEER_HALF:(2 * h + 1) * PEER_HALF]
        q2 = q[:, (2 * h + 1) * PEER_HALF:(2 * h + 2) * PEER_HALF]
        s1 = _dot3(keys0, q1, nt=True)
        s2 = _dot3(keys1, q2, nt=True)
        for g in range(tt // TG):
            ts = slice(g * TG, (g + 1) * TG)
            route(h, ts, s1[:, ts], s2[:, ts])


def _peer_route(x1b, w_peer_q, peer_keys, tt):
    b, s, d = x1b.shape
    nk = PEER_N_KEYS
    wq = w_peer_q.astype(BF16)
    o32 = jax.ShapeDtypeStruct((b, PEER_HEADS, nk, s), F32)
    o16 = jax.ShapeDtypeStruct((b, PEER_HEADS, nk, s), BF16)
    ospec = pl.BlockSpec((1, PEER_HEADS, nk, tt), lambda i, j: (i, 0, 0, j))
    return pl.pallas_call(
        _peer_route_kernel,
        out_shape=(o32, o32, o16, o16),
        grid=(b, s // tt),
        in_specs=[pl.BlockSpec((1, tt, d), lambda i, j: (i, j, 0)),
                  pl.BlockSpec(wq.shape, lambda i, j: (0, 0)),
                  pl.BlockSpec(peer_keys.shape, lambda i, j: (0, 0, 0))],
        out_specs=(ospec,) * 4,
        compiler_params=_cparams(("parallel", "parallel")),
        name="peer_route",
    )(x1b, wq, peer_keys)


PEER_ET = 1024


def _peer_ffn_kernel(x1_ref, x1t_ref, r1_ref, c1_ref, rk2_ref, p2_ref, u_ref, vt_ref,
                     g_ref, b_ref, o_ref, acc_sc, *, alpha):
    e = pl.program_id(2)
    nk = PEER_N_KEYS
    tt = x1t_ref.shape[2]

    @pl.when(e == 0)
    def _():
        acc_sc[...] = jnp.zeros(acc_sc.shape, F32)

    act = _dot(u_ref[...], x1t_ref[0])
    act = 0.5 * act * (1.0 + lax.erf(act * (2.0 ** -0.5)))
    gate_rows = []
    for rr in range(PEER_ET // nk):
        gsum = jnp.zeros((nk, tt), BF16)
        for h in range(PEER_HEADS):
            r1 = r1_ref[0, h, rr:rr + 1, :].astype(BF16)
            c1 = c1_ref[0, h, rr:rr + 1, :].astype(BF16)
            sel = rk2_ref[0, h] < r1
            gsum = gsum + jnp.where(sel, p2_ref[0, h], jnp.zeros((), BF16)) * c1
        gate_rows.append(gsum)
    gate = jnp.concatenate(gate_rows, axis=0)
    w = gate * act.astype(BF16)
    acc_sc[...] += _dot(vt_ref[...], w)

    @pl.when(e == pl.num_programs(2) - 1)
    def _():
        z = alpha * x1_ref[0] + acc_sc[...].T
        o_ref[0] = _layernorm(z, g_ref[...], b_ref[...])


def _peer_ffn(x1, x1t, r1, c1, rk2, p2, peer_u, peer_v, ln_g, ln_b, alpha, tt):
    b, s, d = x1.shape
    ne = peer_u.shape[0]
    nk = PEER_N_KEYS
    rows = PEER_ET // nk
    u = peer_u.astype(BF16)
    vt = peer_v.T.astype(BF16)
    row_spec = pl.BlockSpec((1, PEER_HEADS, rows, tt), lambda i, j, e: (i, 0, e, j))
    key_spec = pl.BlockSpec((1, PEER_HEADS, nk, tt), lambda i, j, e: (i, 0, 0, j))
    return pl.pallas_call(
        functools.partial(_peer_ffn_kernel, alpha=alpha),
        out_shape=jax.ShapeDtypeStruct((b, s, d), F32),
        grid=(b, s // tt, ne // PEER_ET),
        in_specs=[pl.BlockSpec((1, tt, d), lambda i, j, e: (i, j, 0)),
                  pl.BlockSpec((1, d, tt), lambda i, j, e: (i, 0, j)),
                  row_spec, row_spec, key_spec, key_spec,
                  pl.BlockSpec((PEER_ET, d), lambda i, j, e: (e, 0)),
                  pl.BlockSpec((d, PEER_ET), lambda i, j, e: (0, e)),
                  pl.BlockSpec((1, d), lambda i, j, e: (0, 0)),
                  pl.BlockSpec((1, d), lambda i, j, e: (0, 0))],
        out_specs=pl.BlockSpec((1, tt, d), lambda i, j, e: (i, j, 0)),
        scratch_shapes=[pltpu.VMEM((d, tt), F32)],
        compiler_params=_cparams(("parallel", "parallel", "arbitrary")),
        name="peer_ffn",
    )(x1, x1t, r1, c1, rk2, p2, u, vt, ln_g.reshape(1, d), ln_b.reshape(1, d))


def _tile(n, pref):
    t = min(n, pref)
    while n % t:
        t //= 2
    return t


def _layer(x, mem, rel_bias, w_in, q_norm_g, kv_norm_g, w_uq, w_uk, w_uv, w_idx_q,
           rwkv_mu, rwkv_w0, rwkv_w2, rwkv_a0, rwkv_a2, rwkv_g2, rwkv_k_k, rwkv_k_a, rwkv_r_k,
           rwkv_lnx_g, rwkv_lnx_b, w_mem_kv, w_br_dsa, w_br_rwkv, w_br_mem, w_out, ln1_g, ln1_b,
           w_peer_q, peer_keys, peer_u, peer_v, ln2_g, ln2_b, alpha):
    b, s, d = x.shape
    t = b * s
    tm = _tile(s, 512)
    x2 = x.reshape(t, d).astype(BF16)
    n_dsa = DSA_Q_RANK + DSA_KV_RANK + IDX_DIM + IDX_HEADS
    n_rwkv = 3 * RWKV_WIDTH + DECAY_RANK + AAA_RANK + GATE_RANK
    n_mem = MEM_HEADS * MEM_HEAD_DIM
    o1, o2, o3 = n_dsa, n_dsa + n_rwkv, n_dsa + n_rwkv + n_mem
    wb = w_in.astype(BF16)
    tl = _tile(t, 2048)
    dcols = _linear(x2, wb[:, :o1], F32, tl, n_dsa, name="in_dsa")
    rcols = _linear(x2, wb[:, o1:o2], F32, tl, n_rwkv // 2, name="in_rwkv")
    mem_q = _linear(x2, wb[:, o2:o3], BF16, tl, n_mem, name="in_memq")
    gates = _linear(x2, wb[:, o3:], BF16, tl, d, act="sigmoid", name="in_gates")

    q_lat, q_idx, w_idx, k_idx, c_kv = _dsa_prep(dcols.reshape(b, s, n_dsa), q_norm_g, kv_norm_g,
                                                 w_uq, w_uk, w_idx_q, tm)
    y_dsa = _dsa_attn(rel_bias, q_idx, w_idx, q_lat, k_idx, c_kv, w_uv)

    prep = _rwkv_prep(rcols, s, rwkv_mu, rwkv_w0, rwkv_w2, rwkv_a0, rwkv_a2, rwkv_g2,
                      rwkv_k_k, rwkv_k_a, rwkv_r_k, tm)
    y_rwkv = _rwkv_chunk(*[a.reshape(b, s, RWKV_WIDTH) for a in prep], rwkv_lnx_g, rwkv_lnx_b)

    m = mem.shape[1]
    mkv = _linear(mem.reshape(b * m, d), w_mem_kv.astype(BF16), BF16, _tile(b * m, 512), 2 * n_mem,
                  name="mem_kv").reshape(b, m, 2 * n_mem)
    y_mem = _mem_attn(mem_q.reshape(b, s, n_mem), mkv, tm)

    x1, x1b, x1t = _merge(x, y_dsa, y_rwkv, y_mem, gates.reshape(b, s, 3 * d),
                          w_br_dsa, w_br_rwkv, w_br_mem, w_out, ln1_g, ln1_b, alpha, tm)

    r1, c1, rk2, p2 = _peer_route(x1b, w_peer_q, peer_keys, _tile(s, 512))
    return _peer_ffn(x1, x1t, r1, c1, rk2, p2, peer_u, peer_v, ln2_g, ln2_b, alpha, _tile(s, 1024))


def kernel(x, mem, rel_bias, w_in, q_norm_g, kv_norm_g, w_uq, w_uk, w_uv, w_idx_q, rwkv_mu, rwkv_w0, rwkv_w2, rwkv_a0, rwkv_a2, rwkv_g2, rwkv_k_k, rwkv_k_a, rwkv_r_k, rwkv_lnx_g, rwkv_lnx_b, w_mem_kv, w_br_dsa, w_br_rwkv, w_br_mem, w_out, ln1_g, ln1_b, w_peer_q, peer_keys, peer_u, peer_v, ln2_g, ln2_b):
    depth = w_in.shape[0]
    alpha = (2 * depth) ** 0.25
    for l in range(depth):
        x = _layer(x, mem, rel_bias, w_in[l], q_norm_g[l], kv_norm_g[l], w_uq[l], w_uk[l], w_uv[l],
                   w_idx_q[l], rwkv_mu[l], rwkv_w0[l], rwkv_w2[l], rwkv_a0[l], rwkv_a2[l], rwkv_g2[l],
                   rwkv_k_k[l], rwkv_k_a[l], rwkv_r_k[l], rwkv_lnx_g[l], rwkv_lnx_b[l], w_mem_kv[l],
                   w_br_dsa[l], w_br_rwkv[l], w_br_mem[l], w_out[l], ln1_g[l], ln1_b[l],
                   w_peer_q[l], peer_keys[l], peer_u[l], peer_v[l], ln2_g[l], ln2_b[l], alpha)
    return x
```

```python
import functools
import math

import jax
import jax.numpy as jnp
from jax import lax
from jax.experimental import pallas as pl
from jax.experimental.pallas import tpu as pltpu

F32 = jnp.float32
BF16 = jnp.bfloat16
I32 = jnp.int32

DSA_HEADS = 8
DSA_HEAD_DIM = 64
DSA_Q_RANK = 256
DSA_KV_RANK = 128
IDX_HEADS = 8
IDX_DIM = 32
INDEX_TOPK_MAX = 256
NUM_BUCKETS = 32
MAX_DISTANCE = 128
RWKV_HEADS = 8
RWKV_HEAD_DIM = 64
RWKV_WIDTH = RWKV_HEADS * RWKV_HEAD_DIM
DECAY_RANK = 64
AAA_RANK = 64
GATE_RANK = 128
GN_EPS = 64e-5
MEM_HEADS = 4
MEM_HEAD_DIM = 128
PEER_HEADS = 8
PEER_N_KEYS = 128
PEER_HALF = 128
PEER_TOPK = 16
LN_EPS = 1e-5
RMS_EPS = 1e-6

LANES = 128
VMEM_LIMIT = 56 * 1024 * 1024

INT_MIN = -(2 ** 31)
NEG_INF = float("-inf")


def _cparams(sem, vmem=VMEM_LIMIT):
    return pltpu.CompilerParams(dimension_semantics=sem, vmem_limit_bytes=vmem)


def _dot(a, b):
    return jnp.dot(a, b, preferred_element_type=F32)


def _dot_nt(a, b):
    return lax.dot_general(a, b, (((1,), (1,)), ((), ())), preferred_element_type=F32)


def _split2(x):
    hi = x.astype(BF16)
    lo = (x - hi.astype(F32)).astype(BF16)
    return hi, lo


def _dot3(a, b, nt=False):
    d = _dot_nt if nt else _dot
    ah, al = _split2(a)
    bh, bl = _split2(b)
    return d(ah, bh) + (d(ah, bl) + d(al, bh))


def _linear_kernel(x_ref, w_ref, o_ref, *, act):
    y = _dot(x_ref[...].astype(BF16), w_ref[...])
    if act == "sigmoid":
        y = jax.nn.sigmoid(y)
    o_ref[...] = y.astype(o_ref.dtype)


def _linear(x, w, out_dtype, tm, tn, act=None, name="linear"):
    m, k = x.shape
    n = w.shape[1]
    return pl.pallas_call(
        functools.partial(_linear_kernel, act=act),
        out_shape=jax.ShapeDtypeStruct((m, n), out_dtype),
        grid=(m // tm, n // tn),
        in_specs=[pl.BlockSpec((tm, k), lambda i, j: (i, 0)),
                  pl.BlockSpec((k, tn), lambda i, j: (0, j))],
        out_specs=pl.BlockSpec((tm, tn), lambda i, j: (i, j)),
        compiler_params=_cparams(("parallel", "parallel")),
        name=name,
    )(x, w)


def _dsa_prep_kernel(dc_ref, qg_ref, kvg_ref, wuq_ref, wuk_ref, wiq_ref,
                     ql_ref, qi_ref, wi_ref, ki_ref, kv_ref):
    dc = dc_ref[0]
    c_q = dc[:, :DSA_Q_RANK]
    c_kv = dc[:, DSA_Q_RANK:DSA_Q_RANK + DSA_KV_RANK]
    off = DSA_Q_RANK + DSA_KV_RANK
    k_idx = dc[:, off:off + IDX_DIM]
    w_idx = dc[:, off + IDX_DIM:off + IDX_DIM + IDX_HEADS]
    c_q = c_q * lax.rsqrt(jnp.mean(c_q * c_q, axis=-1, keepdims=True) + RMS_EPS) * qg_ref[...]
    c_kv = c_kv * lax.rsqrt(jnp.mean(c_kv * c_kv, axis=-1, keepdims=True) + RMS_EPS) * kvg_ref[...]
    cqb = c_q.astype(BF16)
    scale = DSA_HEAD_DIM ** -0.5
    for h in range(DSA_HEADS):
        q_h = _dot(cqb, wuq_ref[h])
        ql_ref[0, h] = (_dot(q_h.astype(BF16), wuk_ref[h]) * scale).astype(BF16)
        qi_ref[0, h] = _dot(cqb, wiq_ref[h]).astype(BF16)
    wi_ref[0] = w_idx * (IDX_HEADS ** -0.5 * IDX_DIM ** -0.5)
    ki_ref[0] = k_idx.astype(BF16)
    kv_ref[0] = c_kv.astype(BF16)


def _dsa_prep(dcols, q_norm_g, kv_norm_g, w_uq, w_uk, w_idx_q, tm):
    b, s, wd = dcols.shape
    h = DSA_HEADS
    wuq_h = w_uq.reshape(DSA_Q_RANK, h, DSA_HEAD_DIM).transpose(1, 0, 2).astype(BF16)
    wuk_h = w_uk.transpose(1, 2, 0).astype(BF16)
    wiq_h = w_idx_q.reshape(DSA_Q_RANK, IDX_HEADS, IDX_DIM).transpose(1, 0, 2).astype(BF16)
    full = lambda shape: pl.BlockSpec(shape, lambda i, j: (0,) * len(shape))
    return pl.pallas_call(
        _dsa_prep_kernel,
        out_shape=(jax.ShapeDtypeStruct((b, h, s, DSA_KV_RANK), BF16),
                   jax.ShapeDtypeStruct((b, IDX_HEADS, s, IDX_DIM), BF16),
                   jax.ShapeDtypeStruct((b, s, IDX_HEADS), F32),
                   jax.ShapeDtypeStruct((b, s, IDX_DIM), BF16),
                   jax.ShapeDtypeStruct((b, s, DSA_KV_RANK), BF16)),
        grid=(b, s // tm),
        in_specs=[pl.BlockSpec((1, tm, wd), lambda i, j: (i, j, 0)),
                  full((1, DSA_Q_RANK)), full((1, DSA_KV_RANK)),
                  full(wuq_h.shape), full(wuk_h.shape), full(wiq_h.shape)],
        out_specs=(pl.BlockSpec((1, h, tm, DSA_KV_RANK), lambda i, j: (i, 0, j, 0)),
                   pl.BlockSpec((1, IDX_HEADS, tm, IDX_DIM), lambda i, j: (i, 0, j, 0)),
                   pl.BlockSpec((1, tm, IDX_HEADS), lambda i, j: (i, j, 0)),
                   pl.BlockSpec((1, tm, IDX_DIM), lambda i, j: (i, j, 0)),
                   pl.BlockSpec((1, tm, DSA_KV_RANK), lambda i, j: (i, j, 0))),
        compiler_params=_cparams(("parallel", "parallel")),
        name="dsa_prep",
    )(dcols, q_norm_g.reshape(1, -1), kv_norm_g.reshape(1, -1), wuq_h, wuk_h, wiq_h)


DSA_QB = 128
DSA_KB = 128
DSA_SC = 512


def _t5_bias_tile(rel_ref, h, off):
    row = lax.broadcasted_iota(I32, (DSA_QB, DSA_KB), 0)
    col = lax.broadcasted_iota(I32, (DSA_QB, DSA_KB), 1)
    n = jnp.maximum(row - col + off, 0)
    max_exact = NUM_BUCKETS // 2
    nf = jnp.maximum(n, max_exact).astype(F32)
    large = max_exact + jnp.floor(jnp.log(nf / max_exact) / math.log(MAX_DISTANCE / max_exact)
                                  * (NUM_BUCKETS - max_exact)).astype(I32)
    large = jnp.minimum(large, NUM_BUCKETS - 1)
    bucket = jnp.where(n < max_exact, n, large)
    last = rel_ref[NUM_BUCKETS - 1, h]
    out = jnp.zeros((DSA_QB, DSA_KB), F32)
    for bk in range(NUM_BUCKETS - 1):
        out = jnp.where(bucket == bk, rel_ref[bk, h] - last, out)
    return out


def _fold_lanes(x, op):
    out = x[..., :LANES]
    for c in range(1, x.shape[-1] // LANES):
        out = op(out, x[..., c * LANES:(c + 1) * LANES])
    return out


def _dsa_attn_kernel(rel_ref, qi_ref, wi_ref, ql_ref, ki_ref, kv_ref, wuv_ref, o_ref,
                     key_sc, bias_sc, mx_sc, ref_sc, ls_sc, acc_sc, *, topk, seq):
    H = DSA_HEADS
    QB, KB, SC = DSA_QB, DSA_KB, DSA_SC
    qb = pl.program_id(1)
    q0 = qb * QB

    @pl.when((pl.program_id(0) == 0) & (qb == 0))
    def _():
        for h in range(H):
            bias_sc[0, h] = _t5_bias_tile(rel_ref, h, 0)
            bias_sc[1, h] = _t5_bias_tile(rel_ref, h, KB)

    n_sc = (q0 + QB + SC - 1) // SC
    wi = wi_ref[0]
    HALF = SC // 2
    row_g = q0 + lax.broadcasted_iota(I32, (QB, HALF), 0)
    col_h = lax.broadcasted_iota(I32, (QB, HALF), 1)

    def score_body(j, carry):
        for c in range(2):
            k0 = pl.multiple_of(j * SC + c * HALF, HALF)
            ks = ki_ref[0, pl.ds(k0, HALF), :]
            acc = jnp.zeros((QB, HALF), F32)
            for h in range(IDX_HEADS):
                lg = _dot_nt(qi_ref[0, h], ks)
                acc = acc + wi[:, h:h + 1] * jnp.maximum(lg, 0.0)
            key_sc[:, pl.ds(k0, HALF)] = jnp.where(col_h + k0 <= row_g, acc, NEG_INF)
        return carry

    lax.fori_loop(0, n_sc, score_body, 0)

    col_l = lax.broadcasted_iota(I32, (QB, LANES), 1)

    def count(ind):
        def body(j, acc):
            for c in range(SC // LANES):
                k0 = pl.multiple_of(j * SC + c * LANES, LANES)
                acc = acc + ind(key_sc[:, pl.ds(k0, LANES)], k0)
            return acc
        acc = lax.fori_loop(0, n_sc, body, jnp.zeros((QB, LANES), I32))
        return jnp.sum(acc, axis=1, keepdims=True)

    def code_to_float(ru):
        c = ru ^ INT_MIN
        return pltpu.bitcast(c ^ ((c >> 31) & 0x7FFFFFFF), jnp.float32)

    HB = QB // 2

    def descent(n):
        def count_half(r0, ru, bit):
            cand = code_to_float(ru | bit)
            acc = jnp.zeros((HB, LANES), I32)
            for c in range(n * SC // LANES):
                acc = acc + jnp.where(key_sc[r0:r0 + HB, c * LANES:(c + 1) * LANES] >= cand, 1, 0)
            return acc

        def update(state, acc, bit):
            ru, cnt_ru = state
            cnt = jnp.sum(acc, axis=1, keepdims=True, dtype=I32)
            ok = cnt >= topk
            return jnp.where(ok, ru | bit, ru), jnp.where(ok, cnt, cnt_ru)

        def run():
            def bis_body(i, carry):
                st_a, st_b, acc_b = carry
                sh = lax.convert_element_type(31 - i, I32)
                bit = lax.shift_left(jnp.int32(1), sh)
                bit_b = jnp.where(sh == 31, 0, lax.shift_left(jnp.int32(2), jnp.minimum(sh, 30)))
                st_b = update(st_b, acc_b, bit_b)
                st_a = update(st_a, count_half(0, st_a[0], bit), bit)
                return st_a, st_b, count_half(HB, st_b[0], bit)

            zero = jnp.minimum(lax.broadcasted_iota(I32, (HB, LANES), 0), 0)
            st_a, st_b, acc_b = lax.fori_loop(0, 32, bis_body, ((zero, zero), (zero, zero), zero))
            st_b = update(st_b, acc_b, 1)
            return (jnp.concatenate([st_a[0], st_b[0]], axis=0), jnp.concatenate([st_a[1], st_b[1]], axis=0))
        return run

    skip = lambda: (jnp.zeros((QB, LANES), I32), jnp.zeros((QB, LANES), I32))
    ru, cnt_ge = lax.switch(jnp.where(q0 + QB <= topk, 0, n_sc),
                            [skip] + [descent(n) for n in range(1, seq // SC + 1)])
    ru = ru[:, :1]
    cnt_ge = cnt_ge[:, :1]
    few = q0 + lax.broadcasted_iota(I32, (QB, 1), 0) < topk
    tau = jnp.where(few, jnp.finfo(jnp.float32).min, code_to_float(ru))
    tau_b = jnp.broadcast_to(tau, (QB, LANES))

    @pl.when(jnp.max(cnt_ge) > topk)
    def _():
        cnt_gt = count(lambda blk, k0: jnp.where(blk > tau_b, 1, 0))
        need = topk - cnt_gt

        def pos_body(_, lohi):
            lo, hi = lohi
            mid = (lo + hi) >> 1
            mid_b = jnp.broadcast_to(mid, (QB, LANES))
            f = count(lambda blk, k0: jnp.where(blk == tau_b, jnp.where(col_l + k0 <= mid_b, 1, 0), 0))
            ok = f >= need
            return jnp.where(ok, lo, mid + 1), jnp.where(ok, mid, hi)

        lo, _hi = lax.fori_loop(0, max(1, (seq - 1).bit_length()), pos_body,
                                (jnp.zeros((QB, 1), I32), jnp.full((QB, 1), seq - 1, I32)))
        jstar = jnp.broadcast_to(lo, (QB, LANES))

        def fix_body(j, carry):
            for c in range(SC // LANES):
                k0 = pl.multiple_of(j * SC + c * LANES, LANES)
                blk = key_sc[:, pl.ds(k0, LANES)]
                dropped = jnp.where(col_l + k0 > jstar, NEG_INF, blk)
                key_sc[:, pl.ds(k0, LANES)] = jnp.where(blk == tau_b, dropped, blk)
            return carry

        lax.fori_loop(0, n_sc, fix_body, 0)

    ql = ql_ref[0].reshape(H * QB, DSA_KV_RANK)
    near0 = jnp.maximum(qb - 1, 0) * KB
    n_far = (near0 + SC - 1) // SC
    tau_s = jnp.broadcast_to(tau, (QB, SC))
    tau_k = jnp.broadcast_to(tau, (QB, KB))
    col_s = lax.broadcasted_iota(I32, (QB, SC), 1)

    def far_logits(j):
        k0 = pl.multiple_of(j * SC, SC)
        kvb = kv_ref[0, pl.ds(k0, SC), :]
        blk = key_sc[:, pl.ds(k0, SC)]
        madd = jnp.where(blk >= tau_s, jnp.where(col_s + k0 < near0, 0.0, NEG_INF), NEG_INF)
        return _dot_nt(ql, kvb).reshape(H, QB, SC) + madd[None], kvb

    def near_logits(jb, bias):
        k0 = pl.multiple_of(jb * KB, KB)
        kvb = kv_ref[0, pl.ds(k0, KB), :]
        madd = jnp.where(key_sc[:, pl.ds(k0, KB)] >= tau_k, 0.0, NEG_INF)
        return _dot_nt(ql, kvb).reshape(H, QB, KB) + (bias + madd[None]), kvb

    SHIFT_SLACK = 60.0
    row_k = lax.broadcasted_iota(I32, (QB, KB), 0)
    col_k = lax.broadcasted_iota(I32, (QB, KB), 1)
    causal_add = jnp.where(col_k <= row_k, 0.0, NEG_INF)

    def near_raw(jb, bias):
        k0 = pl.multiple_of(jb * KB, KB)
        return _dot_nt(ql, kv_ref[0, pl.ds(k0, KB), :]).reshape(H, QB, KB) + bias

    ref_sc[...] = near_raw(qb, bias_sc[0]) + causal_add[None]

    @pl.when(qb >= 1)
    def _():
        ref_sc[...] = jnp.maximum(ref_sc[...], near_raw(qb - 1, bias_sc[1]))

    ref_sc[...] = jnp.broadcast_to(jnp.max(ref_sc[...], axis=-1, keepdims=True), ref_sc.shape)

    def run_pass(track_max):
        ls_sc[...] = jnp.zeros(ls_sc.shape, F32)
        acc_sc[...] = jnp.zeros(acc_sc.shape, F32)
        if track_max:
            mx_sc[...] = jnp.full(mx_sc.shape, NEG_INF, F32)

        def far_acc(j, carry):
            s, kvb = far_logits(j)
            if track_max:
                mx_sc[...] = jnp.maximum(mx_sc[...], _fold_lanes(s, jnp.maximum))
            m_b = ref_sc[...]
            p = jnp.concatenate([jnp.exp(s[..., c * LANES:(c + 1) * LANES] - m_b)
                                 for c in range(SC // LANES)], axis=-1)
            ls_sc[...] += _fold_lanes(p, jnp.add)
            acc_sc[...] += _dot(p.reshape(H * QB, SC).astype(BF16), kvb).reshape(H, QB, DSA_KV_RANK)
            return carry

        lax.fori_loop(0, n_far, far_acc, 0)

        def near_acc(jb, bias):
            s, kvb = near_logits(jb, bias)
            if track_max:
                mx_sc[...] = jnp.maximum(mx_sc[...], s)
            p = jnp.exp(s - ref_sc[...])
            ls_sc[...] += p
            acc_sc[...] += _dot(p.reshape(H * QB, KB).astype(BF16), kvb).reshape(H, QB, DSA_KV_RANK)

        @pl.when(qb >= 1)
        def _():
            near_acc(qb - 1, bias_sc[1])

        near_acc(qb, bias_sc[0])

    run_pass(True)
    mx = jnp.max(mx_sc[...], axis=-1, keepdims=True)

    @pl.when(jnp.max(jnp.abs(mx - ref_sc[:, :, 0:1])) > SHIFT_SLACK)
    def _():
        ref_sc[...] = jnp.broadcast_to(mx, ref_sc.shape)
        run_pass(False)

    o_lat = acc_sc[...] / jnp.sum(ls_sc[...], axis=-1, keepdims=True)
    y = jnp.zeros((QB, DSA_HEADS * DSA_HEAD_DIM), F32)
    for h in range(H):
        y = y + _dot(o_lat[h].astype(BF16), wuv_ref[h])
    o_ref[0] = y.astype(o_ref.dtype)


def _dsa_attn(rel_bias, q_idx, w_idx, q_lat, k_idx, c_kv, w_uv):
    b, h, s, _ = q_lat.shape
    topk = min(INDEX_TOPK_MAX, s // 4)
    width = DSA_HEADS * DSA_HEAD_DIM
    eye = jnp.eye(DSA_HEADS, dtype=F32)
    wuv_e = jnp.einsum("chd,hg->hcgd", w_uv, eye).reshape(DSA_HEADS, DSA_KV_RANK, width).astype(BF16)
    QB = DSA_QB
    assert s % DSA_SC == 0
    return pl.pallas_call(
        functools.partial(_dsa_attn_kernel, topk=topk, seq=s),
        out_shape=jax.ShapeDtypeStruct((b, s, width), BF16),
        grid=(b, s // QB),
        in_specs=[pl.BlockSpec(memory_space=pltpu.SMEM),
                  pl.BlockSpec((1, IDX_HEADS, QB, IDX_DIM), lambda i, j: (i, 0, j, 0)),
                  pl.BlockSpec((1, QB, IDX_HEADS), lambda i, j: (i, j, 0)),
                  pl.BlockSpec((1, h, QB, DSA_KV_RANK), lambda i, j: (i, 0, j, 0)),
                  pl.BlockSpec((1, s, IDX_DIM), lambda i, j: (i, 0, 0)),
                  pl.BlockSpec((1, s, DSA_KV_RANK), lambda i, j: (i, 0, 0)),
                  pl.BlockSpec(wuv_e.shape, lambda i, j: (0, 0, 0))],
        out_specs=pl.BlockSpec((1, QB, width), lambda i, j: (i, j, 0)),
        scratch_shapes=[pltpu.VMEM((QB, s), F32),
                        pltpu.VMEM((2, h, QB, DSA_KB), F32),
                        pltpu.VMEM((h, QB, LANES), F32),
                        pltpu.VMEM((h, QB, LANES), F32),
                        pltpu.VMEM((h, QB, LANES), F32),
                        pltpu.VMEM((h, QB, DSA_KV_RANK), F32)],
        compiler_params=_cparams(("arbitrary", "arbitrary")),
        name="dsa_attn",
    )(rel_bias, q_idx, w_idx, q_lat, k_idx, c_kv, wuv_e)


def _rwkv_prep_kernel(c_ref, p_ref, mu_ref, w0_ref, a0_ref, kk_ref, ka_ref, rk_ref,
                      wwa_ref, g2_ref, ones_ref,
                      r_o, k_o, v_o, kk_o, b_o, lw_o, bonus_o, g_o, *, tiles_per_seq):
    W = RWKV_WIDTH
    cols = c_ref[...]
    first = (pl.program_id(0) % tiles_per_seq) == 0
    prev_row = jnp.where(first, 0.0, p_ref[7:8, :])
    rolled = pltpu.roll(cols, 1, 0)
    row = lax.broadcasted_iota(I32, cols.shape, 0)
    prev = jnp.where(row == 0, prev_row, rolled)
    xs = cols + (prev - cols) * mu_ref[...]
    r = xs[:, 0:W]
    k = xs[:, W:2 * W]
    v = xs[:, 2 * W:3 * W]
    lora = xs[:, 3 * W:3 * W + DECAY_RANK + AAA_RANK]
    gl = xs[:, 3 * W + DECAY_RANK + AAA_RANK:]
    lane = lax.broadcasted_iota(I32, lora.shape, 1)
    lora = jnp.where(lane < DECAY_RANK, jnp.tanh(lora), lora)
    wa = _dot(lora.astype(BF16), wwa_ref[...])
    w = -jax.nn.softplus(-(w0_ref[...] + wa[:, :W])) - 0.5
    a = jax.nn.sigmoid(a0_ref[...] + wa[:, W:])
    ones_bd = ones_ref[...]

    def head_sum(t):
        hi, lo = _split2(t)
        return _dot(hi, ones_bd) + _dot(lo, ones_bd)

    kk = k * kk_ref[...]
    kk = kk / jnp.maximum(jnp.sqrt(head_sum(kk * kk)), 1e-12)
    k2 = k * (1.0 + (a - 1.0) * ka_ref[...])
    r_o[...] = r
    k_o[...] = k2
    v_o[...] = v
    kk_o[...] = kk
    b_o[...] = kk * a
    lw_o[...] = -jnp.exp(w)
    bonus_o[...] = head_sum(r * k2 * rk_ref[...]) * v
    g_o[...] = _dot(jax.nn.sigmoid(gl).astype(BF16), g2_ref[...])


def _rwkv_prep(cols, s, mu, w0, w2, a0, a2, g2, k_k, k_a, r_k, tm):
    t, wc = cols.shape
    W = RWKV_WIDTH
    wwa = jnp.zeros((DECAY_RANK + AAA_RANK, 2 * W), F32)
    wwa = wwa.at[:DECAY_RANK, :W].set(w2).at[DECAY_RANK:, W:].set(a2).astype(BF16)
    head_id = jnp.arange(W) // RWKV_HEAD_DIM
    ones_bd = (head_id[:, None] == head_id[None, :]).astype(BF16)
    row = lambda a: a.reshape(1, -1)
    vec = lambda n: pl.BlockSpec((1, n), lambda i: (0, 0))
    tok = jax.ShapeDtypeStruct((t, W), F32)
    tok_spec = pl.BlockSpec((tm, W), lambda i: (i, 0))
    return pl.pallas_call(
        functools.partial(_rwkv_prep_kernel, tiles_per_seq=s // tm),
        out_shape=(tok,) * 8,
        grid=(t // tm,),
        in_specs=[pl.BlockSpec((tm, wc), lambda i: (i, 0)),
                  pl.BlockSpec((8, wc), lambda i: (jnp.maximum(i * (tm // 8) - 1, 0), 0)),
                  vec(wc), vec(W), vec(W), vec(W), vec(W), vec(W),
                  pl.BlockSpec(wwa.shape, lambda i: (0, 0)),
                  pl.BlockSpec((GATE_RANK, W), lambda i: (0, 0)),
                  pl.BlockSpec((W, W), lambda i: (0, 0))],
        out_specs=(tok_spec,) * 8,
        compiler_params=_cparams(("parallel",)),
        name="rwkv_prep",
    )(cols, cols, row(mu), row(w0), row(a0), row(k_k), row(k_a), row(r_k.reshape(-1)),
      wwa, g2.astype(BF16), ones_bd)


RWKV_CHUNK = 64
RWKV_GROUP = 16


def _rwkv_chunk_kernel(r_ref, k_ref, v_ref, kk_ref, b_ref, lw_ref, bonus_ref, g_ref,
                       lg_ref, lb_ref, o_ref, st_sc):
    C = RWKV_CHUNK
    N = RWKV_HEAD_DIM
    P = 2 * N
    nb = r_ref.shape[0]
    npair = RWKV_WIDTH // P

    @pl.when(pl.program_id(0) == 0)
    def _():
        st_sc[...] = jnp.zeros(st_sc.shape, F32)

    row_c = lax.broadcasted_iota(I32, (C, P), 0)
    lane_c = lax.broadcasted_iota(I32, (C, P), 1)
    tcol = lane_c & (N - 1)
    strict = tcol < row_c
    incl = tcol <= row_c
    lane_lo_n = lax.broadcasted_iota(I32, (N, P), 1) < N
    r2 = lax.broadcasted_iota(I32, (P, P), 0)
    l2 = lax.broadcasted_iota(I32, (P, P), 1)
    bdmask = (r2 < N) == (l2 < N)
    ones_bd = jnp.where(bdmask, 1.0, 0.0).astype(BF16)
    ones2 = jnp.concatenate([ones_bd, ones_bd], axis=0)
    ti = lax.broadcasted_iota(I32, (C, C), 0)
    si = lax.broadcasted_iota(I32, (C, C), 1)
    tri = jnp.where(si <= ti, 1.0, 0.0).astype(BF16)
    tri3 = jnp.concatenate([tri, tri, tri], axis=1)

    def bd(x):
        return jnp.where(bdmask, jnp.concatenate([x, x], axis=0), 0.0).astype(BF16)

    def head_mean(t):
        hi, lo = _split2(t)
        return _dot(jnp.concatenate([hi, lo], axis=1), ones2) * (1.0 / N)

    units = [(bi, p) for bi in range(nb) for p in range(npair)]
    nlev = max(1, (C - 1).bit_length())
    for g0 in range(0, len(units), RWKV_GROUP):
        grp = units[g0:g0 + RWKV_GROUP]
        n = len(grp)
        ld = lambda ref: [ref[bi, :, p * P:(p + 1) * P] for bi, p in grp]
        r, k, v, kk, b, lw = ld(r_ref), ld(k_ref), ld(v_ref), ld(kk_ref), ld(b_ref), ld(lw_ref)

        cum = []
        for u in range(n):
            hi = lw[u].astype(BF16)
            r1 = lw[u] - hi.astype(F32)
            mid = r1.astype(BF16)
            lo = (r1 - mid.astype(F32)).astype(BF16)
            cum.append(_dot(tri3, jnp.concatenate([hi, mid, lo], axis=0)))
        cum_last = [c[C - 1:C, :] for c in cum]
        p_inv = [jnp.exp(-c) for c in cum]
        a_t = [-kk[u] * jnp.exp(cum[u] - lw[u]) for u in range(n)]
        r_t = [r[u] * jnp.exp(cum[u]) for u in range(n)]
        dec = [jnp.exp(cum_last[u] - cum[u]) for u in range(n)]
        ar = [jnp.concatenate([a_t[u], r_t[u]], axis=0).astype(BF16) for u in range(n)]
        sb = [_dot_nt(ar[u], bd(b[u] * p_inv[u])) for u in range(n)]
        sk = [_dot_nt(ar[u], bd(k[u] * p_inv[u])) for u in range(n)]
        bd_v = [bd(v[u]) for u in range(n)]
        lp = [jnp.where(strict, sb[u][:C], 0.0) for u in range(n)]
        l_ak = [jnp.where(strict, sk[u][:C], 0.0).astype(BF16) for u in range(n)]
        m_rb = [jnp.where(incl, sb[u][C:], 0.0).astype(BF16) for u in range(n)]
        m_rk = [jnp.where(incl, sk[u][C:], 0.0).astype(BF16) for u in range(n)]

        xa = list(a_t)
        xu = [_dot(l_ak[u], bd_v[u]) for u in range(n)]
        for lev in range(nlev):
            last = lev == nlev - 1
            for u in range(n):
                parts = [bd(xa[u]), bd(xu[u])] + ([] if last else [bd(lp[u])])
                res = _dot(lp[u].astype(BF16), jnp.concatenate(parts, axis=1))
                xa[u] = xa[u] + res[:, :P]
                xu[u] = xu[u] + res[:, P:2 * P]
                if not last:
                    lp[u] = res[:, 2 * P:]

        res = [_dot(m_rb[u], jnp.concatenate([bd(xa[u]), bd(xu[u])], axis=1)) for u in range(n)]
        r_hat = [r_t[u] + res[u][:, :P] for u in range(n)]
        y0 = [res[u][:, P:] + _dot(m_rk[u], bd_v[u]) for u in range(n)]
        zt = [jnp.concatenate([b[u] * dec[u], k[u] * dec[u]], axis=0).T.astype(BF16) for u in range(n)]
        pct = [jnp.broadcast_to(cum_last[u], (P, P)).T for u in range(n)]
        pcm = [jnp.exp(jnp.where(lane_lo_n, pct[u][:N], pct[u][N:])) for u in range(n)]

        y = []
        for u, (bi, p) in enumerate(grp):
            st = st_sc[bi, p]
            ws = _dot(jnp.concatenate([xa[u], r_hat[u]], axis=0).astype(BF16), bd(st))
            wc = ws[:C] + xu[u]
            y.append(ws[C:] + y0[u])
            full = _dot(zt[u], jnp.concatenate([wc, v[u]], axis=0).astype(BF16))
            st_sc[bi, p] = pcm[u] * st + jnp.where(lane_lo_n, full[:N], full[N:])

        for u, (bi, p) in enumerate(grp):
            sl = slice(p * P, (p + 1) * P)
            yc = y[u] - head_mean(y[u])
            yn = yc * lax.rsqrt(head_mean(yc * yc) + GN_EPS) * lg_ref[:, sl] + lb_ref[:, sl]
            o_ref[bi, :, sl] = ((yn + bonus_ref[bi, :, sl]) * g_ref[bi, :, sl]).astype(o_ref.dtype)


def _rwkv_chunk(r, k, v, kk, bvec, lw, bonus, g, lnx_g, lnx_b):
    b, s, w = r.shape
    C = RWKV_CHUNK
    spec = pl.BlockSpec((b, C, w), lambda j: (0, j, 0))
    pspec = pl.BlockSpec((1, w), lambda j: (0, 0))
    return pl.pallas_call(
        _rwkv_chunk_kernel,
        out_shape=jax.ShapeDtypeStruct((b, s, w), BF16),
        grid=(s // C,),
        in_specs=[spec] * 8 + [pspec, pspec],
        out_specs=spec,
        scratch_shapes=[pltpu.VMEM((b, w // (2 * RWKV_HEAD_DIM), RWKV_HEAD_DIM, 2 * RWKV_HEAD_DIM), F32)],
        compiler_params=_cparams(("arbitrary",)),
        name="rwkv_chunk",
    )(r, k, v, kk, bvec, lw, bonus, g, lnx_g.reshape(1, w), lnx_b.reshape(1, w))


def _mem_attn_kernel(q_ref, kv_ref, o_ref):
    W = MEM_HEADS * MEM_HEAD_DIM
    scale = MEM_HEAD_DIM ** -0.5
    for h in range(MEM_HEADS):
        sl = slice(h * MEM_HEAD_DIM, (h + 1) * MEM_HEAD_DIM)
        q = q_ref[0, :, sl]
        k = kv_ref[0, :, sl]
        v = kv_ref[0, :, W + h * MEM_HEAD_DIM:W + (h + 1) * MEM_HEAD_DIM]
        s = _dot_nt(q, k) * scale
        s = s - jnp.max(s, axis=-1, keepdims=True)
        p = jnp.exp(s)
        p = p / jnp.sum(p, axis=-1, keepdims=True)
        o_ref[0, :, sl] = _dot(p.astype(BF16), v).astype(o_ref.dtype)


def _mem_attn(q, kv, tq):
    b, s, w = q.shape
    m = kv.shape[1]
    return pl.pallas_call(
        _mem_attn_kernel,
        out_shape=jax.ShapeDtypeStruct((b, s, w), BF16),
        grid=(b, s // tq),
        in_specs=[pl.BlockSpec((1, tq, w), lambda i, j: (i, j, 0)),
                  pl.BlockSpec((1, m, 2 * w), lambda i, j: (i, 0, 0))],
        out_specs=pl.BlockSpec((1, tq, w), lambda i, j: (i, j, 0)),
        compiler_params=_cparams(("parallel", "parallel")),
        name="mem_attn",
    )(q, kv)


def _layernorm(z, g, b):
    mu = jnp.mean(z, axis=-1, keepdims=True)
    zc = z - mu
    var = jnp.mean(zc * zc, axis=-1, keepdims=True)
    return zc * lax.rsqrt(var + LN_EPS) * g + b


def _merge_kernel(x_ref, yd_ref, yr_ref, ym_ref, wg_ref, wd_ref, wr_ref, wm_ref, wo_ref,
                  g_ref, b_ref, x1_ref, x1b_ref, x1t_ref, *, alpha):
    d = x_ref.shape[-1]
    x = x_ref[0]
    xb = x.astype(BF16)
    merged = jnp.zeros(x.shape, F32)
    for k, (y_ref, w_ref) in enumerate(((yd_ref, wd_ref), (yr_ref, wr_ref), (ym_ref, wm_ref))):
        gate = jax.nn.sigmoid(_dot(xb, wg_ref[:, k * d:(k + 1) * d]))
        merged = merged + gate * _dot(y_ref[0], w_ref[...])
    z = alpha * x + _dot(merged.astype(BF16), wo_ref[...])
    x1 = _layernorm(z, g_ref[...], b_ref[...])
    x1_ref[0] = x1
    x1b_ref[0] = x1.astype(BF16)
    x1t_ref[0] = x1.T.astype(BF16)


def _merge(x, y_dsa, y_rwkv, y_mem, w_gate, w_br_dsa, w_br_rwkv, w_br_mem, w_out, ln_g, ln_b, alpha, tm):
    b, s, d = x.shape
    full2 = lambda a: pl.BlockSpec(a.shape, lambda i, j: (0,) * a.ndim)
    wd = w_br_dsa.astype(BF16)
    wr = w_br_rwkv.astype(BF16)
    wm = w_br_mem.astype(BF16)
    wo = w_out.astype(BF16)
    g2 = ln_g.reshape(1, d)
    b2 = ln_b.reshape(1, d)
    tok = lambda w: pl.BlockSpec((1, tm, w), lambda i, j: (i, j, 0))
    return pl.pallas_call(
        functools.partial(_merge_kernel, alpha=alpha),
        out_shape=(jax.ShapeDtypeStruct((b, s, d), F32),
                   jax.ShapeDtypeStruct((b, s, d), BF16),
                   jax.ShapeDtypeStruct((b, d, s), BF16)),
        grid=(b, s // tm),
        in_specs=[tok(d), tok(y_dsa.shape[-1]), tok(y_rwkv.shape[-1]), tok(y_mem.shape[-1]), full2(w_gate),
                  full2(wd), full2(wr), full2(wm), full2(wo), full2(g2), full2(b2)],
        out_specs=(tok(d), tok(d), pl.BlockSpec((1, d, tm), lambda i, j: (i, 0, j))),
        compiler_params=_cparams(("parallel", "parallel")),
        name="merge",
    )(x, y_dsa, y_rwkv, y_mem, w_gate, wd, wr, wm, wo, g2, b2)


def _sort_network(n):
    pairs = []

    def merge(lo, cnt, r):
        step = r * 2
        if step < cnt:
            merge(lo, cnt, step)
            merge(lo + r, cnt, step)
            for i in range(lo + r, lo + cnt - r, step):
                pairs.append((i, i + r))
        else:
            pairs.append((lo, lo + r))

    def sort(lo, cnt):
        if cnt > 1:
            half = cnt // 2
            sort(lo, half)
            sort(lo + half, half)
            merge(lo, cnt, 1)

    sort(0, n)
    return pairs


SUBLANES = 8
PEER_ROUTE_GROUP = 512
PEER_SLABS = PEER_N_KEYS // SUBLANES
PEER_SORT_PAIRS = _sort_network(PEER_SLABS)
PEER_CAND = [(i, j) for i in range(PEER_TOPK) for j in range(PEER_TOPK // (i + 1))]
PEER_CAND += [None] * (-len(PEER_CAND) % SUBLANES)


def _top_sorted(sub, k):
    slabs = [sub[SUBLANES * s:SUBLANES * (s + 1), :] for s in range(PEER_SLABS)]
    for i, j in PEER_SORT_PAIRS:
        slabs[i], slabs[j] = jnp.maximum(slabs[i], slabs[j]), jnp.minimum(slabs[i], slabs[j])
    tops = []
    for r in range(k):
        m = jnp.max(slabs[0], axis=0, keepdims=True)
        tops.append(m)
        if r < k - 1:
            hit = slabs[0] == m
            for i in range(k - 1 - r):
                slabs[i] = jnp.where(hit, slabs[i + 1], slabs[i])
    return tops


def _prefix_count(pred, rows):
    full = pred(rows[15])
    c1 = pred(rows[7])
    c2 = pred(jnp.where(c1, rows[11], rows[3]))
    c3 = pred(jnp.where(c1, jnp.where(c2, rows[13], rows[9]), jnp.where(c2, rows[5], rows[1])))
    c4 = pred(jnp.where(c1,
                        jnp.where(c2, jnp.where(c3, rows[14], rows[12]), jnp.where(c3, rows[10], rows[8])),
                        jnp.where(c2, jnp.where(c3, rows[6], rows[4]), jnp.where(c3, rows[2], rows[0]))))
    cnt = (jnp.where(c1, 8.0, 0.0) + jnp.where(c2, 4.0, 0.0)) + (jnp.where(c3, 2.0, 0.0) + jnp.where(c4, 1.0, 0.0))
    return jnp.where(full, 16.0, cnt)


def _peer_route_kernel(x_ref, wq_ref, keys_ref, r1_o, c1_o, rk2_o, p2_o):
    tt = x_ref.shape[1]
    K = PEER_TOPK
    q = _dot(x_ref[0], wq_ref[...])
    keys0 = keys_ref[0]
    keys1 = keys_ref[1]
    TG = min(tt, PEER_ROUTE_GROUP)
    sub_id = lax.broadcasted_iota(I32, (SUBLANES, TG), 0)

    def route(h, ts, s1, s2):
        a1 = _top_sorted(s1, K)
        a2 = _top_sorted(s2, K)
        best = a1[0] + a2[0]
        cand = []
        for g in range(len(PEER_CAND) // SUBLANES):
            slab = jnp.full((SUBLANES, TG), NEG_INF, F32)
            for s, ij in enumerate(PEER_CAND[SUBLANES * g:SUBLANES * (g + 1)]):
                if ij is not None:
                    slab = jnp.where(sub_id == s, a1[ij[0]] + a2[ij[1]], slab)
            cand.append(slab)
        zsum = jnp.zeros((1, TG), F32)
        v_k = best
        for rnk in range(K):
            m = cand[0]
            for slab in cand[1:]:
                m = jnp.maximum(m, slab)
            v_k = jnp.max(m, axis=0, keepdims=True)
            zsum = zsum + jnp.exp(v_k - best)
            if rnk < K - 1:
                cand = [jnp.where(slab == v_k, NEG_INF, slab) for slab in cand]
        r1_o[0, h, :, ts] = _prefix_count(lambda t: s1 + t >= v_k, a2)
        c1_o[0, h, :, ts] = jnp.exp(s1 - a1[0]) / zsum
        rk2_o[0, h, :, ts] = _prefix_count(lambda t: t > s2, a2).astype(BF16)
        p2_o[0, h, :, ts] = jnp.exp(s2 - a2[0]).astype(BF16)

    for h in range(PEER_HEADS):
        q1 = q[:, (2 * h) * PEER_HALF:(2 * h + 1) * PEER_HALF]
        q2 = q[:, (2 * h + 1) * PEER_HALF:(2 * h + 2) * PEER_HALF]
        s1 = _dot3(keys0, q1, nt=True)
        s2 = _dot3(keys1, q2, nt=True)
        for g in range(tt // TG):
            ts = slice(g * TG, (g + 1) * TG)
            route(h, ts, s1[:, ts], s2[:, ts])


def _peer_route(x1b, w_peer_q, peer_keys, tt):
    b, s, d = x1b.shape
    nk = PEER_N_KEYS
    wq = w_peer_q.astype(BF16)
    o32 = jax.ShapeDtypeStruct((b, PEER_HEADS, nk, s), F32)
    o16 = jax.ShapeDtypeStruct((b, PEER_HEADS, nk, s), BF16)
    ospec = pl.BlockSpec((1, PEER_HEADS, nk, tt), lambda i, j: (i, 0, 0, j))
    return pl.pallas_call(
        _peer_route_kernel,
        out_shape=(o32, o32, o16, o16),
        grid=(b, s // tt),
        in_specs=[pl.BlockSpec((1, tt, d), lambda i, j: (i, j, 0)),
                  pl.BlockSpec(wq.shape, lambda i, j: (0, 0)),
                  pl.BlockSpec(peer_keys.shape, lambda i, j: (0, 0, 0))],
        out_specs=(ospec,) * 4,
        compiler_params=_cparams(("parallel", "parallel")),
        name="peer_route",
    )(x1b, wq, peer_keys)


PEER_ET = 1024


def _peer_ffn_kernel(x1_ref, x1t_ref, r1_ref, c1_ref, rk2_ref, p2_ref, u_ref, vt_ref,
                     g_ref, b_ref, o_ref, acc_sc, *, alpha):
    e = pl.program_id(2)
    nk = PEER_N_KEYS
    tt = x1t_ref.shape[2]

    @pl.when(e == 0)
    def _():
        acc_sc[...] = jnp.zeros(acc_sc.shape, F32)

    act = _dot(u_ref[...], x1t_ref[0])
    act = 0.5 * act * (1.0 + lax.erf(act * (2.0 ** -0.5)))
    gate_rows = []
    for rr in range(PEER_ET // nk):
        gsum = jnp.zeros((nk, tt), BF16)
        for h in range(PEER_HEADS):
            r1 = r1_ref[0, h, rr:rr + 1, :].astype(BF16)
            c1 = c1_ref[0, h, rr:rr + 1, :].astype(BF16)
            sel = rk2_ref[0, h] < r1
            gsum = gsum + jnp.where(sel, p2_ref[0, h], jnp.zeros((), BF16)) * c1
        gate_rows.append(gsum)
    gate = jnp.concatenate(gate_rows, axis=0)
    w = gate * act.astype(BF16)
    acc_sc[...] += _dot(vt_ref[...], w)

    @pl.when(e == pl.num_programs(2) - 1)
    def _():
        z = alpha * x1_ref[0] + acc_sc[...].T
        o_ref[0] = _layernorm(z, g_ref[...], b_ref[...])


def _peer_ffn(x1, x1t, r1, c1, rk2, p2, peer_u, peer_v, ln_g, ln_b, alpha, tt):
    b, s, d = x1.shape
    ne = peer_u.shape[0]
    nk = PEER_N_KEYS
    rows = PEER_ET // nk
    u = peer_u.astype(BF16)
    vt = peer_v.T.astype(BF16)
    row_spec = pl.BlockSpec((1, PEER_HEADS, rows, tt), lambda i, j, e: (i, 0, e, j))
    key_spec = pl.BlockSpec((1, PEER_HEADS, nk, tt), lambda i, j, e: (i, 0, 0, j))
    return pl.pallas_call(
        functools.partial(_peer_ffn_kernel, alpha=alpha),
        out_shape=jax.ShapeDtypeStruct((b, s, d), F32),
        grid=(b, s // tt, ne // PEER_ET),
        in_specs=[pl.BlockSpec((1, tt, d), lambda i, j, e: (i, j, 0)),
                  pl.BlockSpec((1, d, tt), lambda i, j, e: (i, 0, j)),
                  row_spec, row_spec, key_spec, key_spec,
                  pl.BlockSpec((PEER_ET, d), lambda i, j, e: (e, 0)),
                  pl.BlockSpec((d, PEER_ET), lambda i, j, e: (0, e)),
                  pl.BlockSpec((1, d), lambda i, j, e: (0, 0)),
                  pl.BlockSpec((1, d), lambda i, j, e: (0, 0))],
        out_specs=pl.BlockSpec((1, tt, d), lambda i, j, e: (i, j, 0)),
        scratch_shapes=[pltpu.VMEM((d, tt), F32)],
        compiler_params=_cparams(("parallel", "parallel", "arbitrary")),
        name="peer_ffn",
    )(x1, x1t, r1, c1, rk2, p2, u, vt, ln_g.reshape(1, d), ln_b.reshape(1, d))


def _tile(n, pref):
    t = min(n, pref)
    while n % t:
        t //= 2
    return t


def _layer(x, mem, rel_bias, w_in, q_norm_g, kv_norm_g, w_uq, w_uk, w_uv, w_idx_q,
           rwkv_mu, rwkv_w0, rwkv_w2, rwkv_a0, rwkv_a2, rwkv_g2, rwkv_k_k, rwkv_k_a, rwkv_r_k,
           rwkv_lnx_g, rwkv_lnx_b, w_mem_kv, w_br_dsa, w_br_rwkv, w_br_mem, w_out, ln1_g, ln1_b,
           w_peer_q, peer_keys, peer_u, peer_v, ln2_g, ln2_b, alpha):
    b, s, d = x.shape
    t = b * s
    tm = _tile(s, 512)
    x2 = x.reshape(t, d).astype(BF16)
    n_dsa = DSA_Q_RANK + DSA_KV_RANK + IDX_DIM + IDX_HEADS
    n_rwkv = 3 * RWKV_WIDTH + DECAY_RANK + AAA_RANK + GATE_RANK
    n_mem = MEM_HEADS * MEM_HEAD_DIM
    o1, o2, o3 = n_dsa, n_dsa + n_rwkv, n_dsa + n_rwkv + n_mem
    wb = w_in.astype(BF16)
    tl = _tile(t, 2048)
    dcols = _linear(x2, wb[:, :o1], F32, tl, n_dsa, name="in_dsa")
    rcols = _linear(x2, wb[:, o1:o2], F32, tl, n_rwkv // 2, name="in_rwkv")
    mem_q = _linear(x2, wb[:, o2:o3], BF16, tl, n_mem, name="in_memq")
    w_gate = wb[:, o3:]

    q_lat, q_idx, w_idx, k_idx, c_kv = _dsa_prep(dcols.reshape(b, s, n_dsa), q_norm_g, kv_norm_g,
                                                 w_uq, w_uk, w_idx_q, tm)
    y_dsa = _dsa_attn(rel_bias, q_idx, w_idx, q_lat, k_idx, c_kv, w_uv)

    prep = _rwkv_prep(rcols, s, rwkv_mu, rwkv_w0, rwkv_w2, rwkv_a0, rwkv_a2, rwkv_g2,
                      rwkv_k_k, rwkv_k_a, rwkv_r_k, tm)
    y_rwkv = _rwkv_chunk(*[a.reshape(b, s, RWKV_WIDTH) for a in prep], rwkv_lnx_g, rwkv_lnx_b)

    m = mem.shape[1]
    mkv = _linear(mem.reshape(b * m, d), w_mem_kv.astype(BF16), BF16, _tile(b * m, 512), 2 * n_mem,
                  name="mem_kv").reshape(b, m, 2 * n_mem)
    y_mem = _mem_attn(mem_q.reshape(b, s, n_mem), mkv, tm)

    x1, x1b, x1t = _merge(x, y_dsa, y_rwkv, y_mem, w_gate,
                          w_br_dsa, w_br_rwkv, w_br_mem, w_out, ln1_g, ln1_b, alpha, tm)

    r1, c1, rk2, p2 = _peer_route(x1b, w_peer_q, peer_keys, _tile(s, 512))
    return _peer_ffn(x1, x1t, r1, c1, rk2, p2, peer_u, peer_v, ln2_g, ln2_b, alpha, _tile(s, 1024))


def kernel(x, mem, rel_bias, w_in, q_norm_g, kv_norm_g, w_uq, w_uk, w_uv, w_idx_q, rwkv_mu, rwkv_w0, rwkv_w2, rwkv_a0, rwkv_a2, rwkv_g2, rwkv_k_k, rwkv_k_a, rwkv_r_k, rwkv_lnx_g, rwkv_lnx_b, w_mem_kv, w_br_dsa, w_br_rwkv, w_br_mem, w_out, ln1_g, ln1_b, w_peer_q, peer_keys, peer_u, peer_v, ln2_g, ln2_b):
    depth = w_in.shape[0]
    alpha = (2 * depth) ** 0.25
    for l in range(depth):
        x = _layer(x, mem, rel_bias, w_in[l], q_norm_g[l], kv_norm_g[l], w_uq[l], w_uk[l], w_uv[l],
                   w_idx_q[l], rwkv_mu[l], rwkv_w0[l], rwkv_w2[l], rwkv_a0[l], rwkv_a2[l], rwkv_g2[l],
                   rwkv_k_k[l], rwkv_k_a[l], rwkv_r_k[l], rwkv_lnx_g[l], rwkv_lnx_b[l], w_mem_kv[l],
                   w_br_dsa[l], w_br_rwkv[l], w_br_mem[l], w_out[l], ln1_g[l], ln1_b[l],
                   w_peer_q[l], peer_keys[l], peer_u[l], peer_v[l], ln2_g[l], ln2_b[l], alpha)
    return x
```
